```python
import math
import jax, jax.numpy as jnp
from jax import lax
import numpy as np

D_MODEL = 1024
BATCH = 4
SEQ = 4096
DEPTH = 2
DEC_BATCH = 128
DEC_SEQ = 1
PAST_LEN = 16384
PAGE_SIZE = 128

N_EVEN = (DEPTH + 1) // 2
N_ODD = DEPTH // 2
D_FF = ((8 * D_MODEL // 3 + 127) // 128) * 128
RMS_EPS = 1e-6
ROPE_THETA = 10000.0
CHUNK = 128
Q_BLOCK = 128
RET_HEADS = 4
RET_DK = D_MODEL // 8
RET_DV = D_MODEL // 4
SSM_HEADS = 16
SSM_HEAD_DIM = 64
SSM_D_INNER = SSM_HEADS * SSM_HEAD_DIM
SSM_STATE = 64
SSM_GROUPS = 2
CONV_W = 4
CONV_DIM = SSM_D_INNER + 2 * SSM_GROUPS * SSM_STATE
HYB_IN = 2 * RET_HEADS * RET_DK + 2 * RET_HEADS * RET_DV + SSM_D_INNER + CONV_DIM + SSM_HEADS
HYB_MIX = RET_HEADS * RET_DV + SSM_D_INNER
MLA_HEADS = 16
Q_LORA = 512
KV_LORA = 256
QK_NOPE = 64
QK_ROPE = 32
V_DIM = 64
MLA_IN = Q_LORA + KV_LORA + QK_ROPE
MLA_SCALE = (QK_NOPE + QK_ROPE) ** -0.5

kernel_name = 'hybrid_retention_ssd_mla_macaron_step'


def rms_norm(x, w):
    xf = x.astype(jnp.float32)
    y = xf * lax.rsqrt(jnp.mean(xf * xf, axis=-1, keepdims=True) + RMS_EPS)
    return (y * w.astype(jnp.float32)).astype(x.dtype)


def swiglu(h, w_in, w_out):
    gate, up = jnp.split(h @ w_in, 2, axis=-1)
    return (jax.nn.silu(gate) * up) @ w_out


def rope(x, pos):
    half = x.shape[-1] // 2
    inv = ROPE_THETA ** (-jnp.arange(half, dtype=jnp.float32) / half)
    ang = pos.astype(jnp.float32)[:, None] * inv[None, :]
    cos = jnp.cos(ang)[None, :, None, :]
    sin = jnp.sin(ang)[None, :, None, :]
    xf = x.astype(jnp.float32)
    x1, x2 = xf[..., :half], xf[..., half:]
    return jnp.concatenate([x1 * cos - x2 * sin, x1 * sin + x2 * cos], axis=-1).astype(x.dtype)


def chunked_decay_recurrence(q, k, v, log_a, state0):
    f32 = jnp.float32
    b, L, h, dk = q.shape
    dv = v.shape[-1]
    c = math.gcd(L, CHUNK)
    n = L // c
    q = q.astype(f32).reshape(b, n, c, h, dk)
    k = k.astype(f32).reshape(b, n, c, h, dk)
    v = v.astype(f32).reshape(b, n, c, h, dv)
    cum = jnp.cumsum(log_a.astype(f32).reshape(b, n, c, h), axis=2).swapaxes(2, 3)
    causal = jnp.tril(jnp.ones((c, c), dtype=bool))
    seg = jnp.exp(jnp.where(causal, cum[..., :, None] - cum[..., None, :], -jnp.inf))
    scores = jnp.einsum('bnihd,bnjhd->bnhij', q, k) * seg
    y_intra = jnp.einsum('bnhij,bnjhe->bnihe', scores, v)
    k_end = k * jnp.exp(cum[..., -1:] - cum).swapaxes(2, 3)[..., None]
    chunk_kv = jnp.einsum('bnjhd,bnjhe->bnhde', k_end, v)
    chunk_decay = jnp.exp(cum[..., -1])

    def step(s, inp):
        dec, kv = inp
        return dec[..., None, None] * s + kv, s

    s_final, s_prev = lax.scan(step, state0.astype(f32),
                               (chunk_decay.swapaxes(0, 1), chunk_kv.swapaxes(0, 1)))
    s_prev = s_prev.swapaxes(0, 1)
    q_in = q * jnp.exp(cum).swapaxes(2, 3)[..., None]
    y_inter = jnp.einsum('bnihd,bnhde->bnihe', q_in, s_prev)
    return (y_intra + y_inter).reshape(b, L, h, dv), s_final


def causal_depthwise_conv(x_full, w, bias):
    y = lax.conv_general_dilated(x_full, w[:, None, :].astype(x_full.dtype), window_strides=(1,),
                                 padding='VALID', dimension_numbers=('NWC', 'WIO', 'NWC'),
                                 feature_group_count=x_full.shape[-1])
    return y + bias.astype(x_full.dtype)


def hybrid_mixer(h, pos, ret_s0, ssm_s0, conv_s0, w_in, w_out, conv_w, conv_b, dt_bias, a_log,
                 d_skip, norm_w):
    f32 = jnp.float32
    b, L, _ = h.shape
    qk = RET_HEADS * RET_DK
    vv = RET_HEADS * RET_DV
    cuts = [qk, 2 * qk, 2 * qk + vv, 2 * qk + 2 * vv, 2 * qk + 2 * vv + SSM_D_INNER,
            2 * qk + 2 * vv + SSM_D_INNER + CONV_DIM]
    q, k, v, g, z, xbc_raw, dt_raw = jnp.split(h @ w_in, cuts, axis=-1)
    q = rope(q.reshape(b, L, RET_HEADS, RET_DK), pos)
    k = rope(k.reshape(b, L, RET_HEADS, RET_DK), pos) * (RET_DK ** -0.5)
    v = v.reshape(b, L, RET_HEADS, RET_DV)
    log_gamma = jnp.log1p(-jnp.exp2(-5.0 - jnp.arange(RET_HEADS, dtype=f32)))
    o_ret, ret_s = chunked_decay_recurrence(q, k, v, jnp.broadcast_to(log_gamma, (b, L, RET_HEADS)),
                                            ret_s0)
    o_ret = o_ret * lax.rsqrt(jnp.mean(o_ret * o_ret, axis=-1, keepdims=True) + RMS_EPS)
    o_ret = o_ret.reshape(b, L, vv) * jax.nn.silu(g.astype(f32))
    xbc_full = jnp.concatenate([conv_s0.astype(h.dtype), xbc_raw], axis=1)
    new_conv = xbc_full[:, L:]
    xbc = jax.nn.silu(causal_depthwise_conv(xbc_full, conv_w, conv_b))
    xs, bm, cm = jnp.split(xbc, [SSM_D_INNER, SSM_D_INNER + SSM_GROUPS * SSM_STATE], axis=-1)
    xs = xs.reshape(b, L, SSM_HEADS, SSM_HEAD_DIM).astype(f32)
    rep = SSM_HEADS // SSM_GROUPS
    bm = jnp.repeat(bm.reshape(b, L, SSM_GROUPS, SSM_STATE), rep, axis=2)
    cm = jnp.repeat(cm.reshape(b, L, SSM_GROUPS, SSM_STATE), rep, axis=2)
    dt = jax.nn.softplus(dt_raw.astype(f32) + dt_bias.astype(f32))
    a = -jnp.exp(a_log.astype(f32))
    y, ssm_s = chunked_decay_recurrence(cm, bm, xs * dt[..., None], dt * a, ssm_s0)
    y = y + d_skip.astype(f32)[:, None] * xs
    yg = (y.reshape(b, L, SSM_D_INNER) * jax.nn.silu(z.astype(f32))).reshape(b, L, SSM_GROUPS, -1)
    yg = yg * lax.rsqrt(jnp.mean(yg * yg, axis=-1, keepdims=True) + RMS_EPS)
    y = yg.reshape(b, L, SSM_D_INNER) * norm_w.astype(f32)
    mixed = jnp.concatenate([o_ret, y], axis=-1).astype(h.dtype)
    return (mixed @ w_out, ret_s.astype(ret_s0.dtype), ssm_s.astype(ssm_s0.dtype), new_conv)


def mla_project(h, pos, w_in, q_norm_w, kv_norm_w, w_uq):
    b, L, _ = h.shape
    cq, ckv, kr = jnp.split(h @ w_in, [Q_LORA, Q_LORA + KV_LORA], axis=-1)
    q = (rms_norm(cq, q_norm_w) @ w_uq).reshape(b, L, MLA_HEADS, QK_NOPE + QK_ROPE)
    q_nope = q[..., :QK_NOPE]
    q_rope = rope(q[..., QK_NOPE:], pos)
    ckv = rms_norm(ckv, kv_norm_w)
    kr = rope(kr[:, :, None, :], pos)[:, :, 0, :]
    return q_nope, q_rope, ckv, kr


def mla_prompt_attend(q_nope, q_rope, ckv, kr, w_uk, w_uv):
    b, S, H, _ = q_nope.shape
    k_nope = jnp.einsum('bsc,chd->bshd', ckv, w_uk)
    v = jnp.einsum('bsc,chv->bshv', ckv, w_uv)
    nb = S // Q_BLOCK
    kpos = jnp.arange(S)

    def blocks(t):
        return t.reshape(b, nb, Q_BLOCK, *t.shape[2:]).swapaxes(0, 1)

    def attend(args):
        qn, qr, i = args
        s = (jnp.einsum('bqhd,bkhd->bhqk', qn, k_nope)
             + jnp.einsum('bqhr,bkr->bhqk', qr, kr)).astype(jnp.float32) * MLA_SCALE
        qpos = i * Q_BLOCK + jnp.arange(Q_BLOCK)
        s = jnp.where(kpos[None, :] <= qpos[:, None], s, -jnp.inf)
        p = jax.nn.softmax(s, axis=-1).astype(v.dtype)
        return jnp.einsum('bhqk,bkhv->bqhv', p, v)

    o = lax.map(attend, (blocks(q_nope), blocks(q_rope), jnp.arange(nb)))
    return o.swapaxes(0, 1).reshape(b, S, H, V_DIM)


def mla_sample_attend(q_nope, q_rope, ckv_new, kr_new, page_table, cache_ckv, cache_krope, layer_idx,
                      w_uk, w_uv):
    f32 = jnp.float32
    q_lat = jnp.einsum('bthd,chd->bthc', q_nope.astype(f32), w_uk.astype(f32)) * MLA_SCALE
    q_r = q_rope.astype(f32) * MLA_SCALE
    b, T, H, _ = q_lat.shape

    def scores(ckv, kr):
        return jnp.einsum('bthc,bpc->bhtp', q_lat, ckv) + jnp.einsum('bthr,bpr->bhtp', q_r, kr)

    def merge(carry, s, ckv):
        m, l, acc = carry
        m_new = jnp.maximum(m, jnp.max(s, axis=-1))
        corr = jnp.exp(m - m_new)
        p = jnp.exp(s - m_new[..., None])
        return (m_new, l * corr + jnp.sum(p, axis=-1),
                acc * corr[..., None] + jnp.einsum('bhtp,bpc->bhtc', p, ckv))

    def page_step(carry, pages):
        ckv = cache_ckv[layer_idx, pages].astype(f32)
        kr = cache_krope[layer_idx, pages].astype(f32)
        return merge(carry, scores(ckv, kr), ckv), None

    init = (jnp.full((b, H, T), -jnp.inf, f32), jnp.zeros((b, H, T), f32),
            jnp.zeros((b, H, T, KV_LORA), f32))
    carry, _ = lax.scan(page_step, init, page_table.T)
    ckv_n = ckv_new.astype(f32)
    s_new = scores(ckv_n, kr_new.astype(f32))
    s_new = jnp.where(jnp.tril(jnp.ones((T, T), dtype=bool)), s_new, -jnp.inf)
    m, l, acc = merge(carry, s_new, ckv_n)
    o_lat = (acc / l[..., None]).swapaxes(1, 2)
    return jnp.einsum('bthc,chv->bthv', o_lat, w_uv.astype(f32)).astype(q_nope.dtype)


def setup_inputs(seed: int = 0) -> dict:
    key = jax.random.key(seed)
    keys = jax.random.split(key, 40)
    f32 = jnp.float32

    def nrm(i, shape, scale):
        return jax.random.normal(keys[i], shape, f32) * scale

    n_pages = PAST_LEN // PAGE_SIZE
    n_used = DEC_BATCH * n_pages
    n_phys = n_used + (n_used + 3) // 4
    x_prompt = nrm(0, (BATCH, SEQ, D_MODEL), 1.0)
    x_sample = nrm(1, (DEC_BATCH, DEC_SEQ, D_MODEL), 1.0)
    state_ret = nrm(2, (N_EVEN, DEC_BATCH, RET_HEADS, RET_DK, RET_DV), 1.0)
    state_ssm = nrm(3, (N_EVEN, DEC_BATCH, SSM_HEADS, SSM_STATE, SSM_HEAD_DIM), 0.5)
    state_conv = nrm(4, (N_EVEN, DEC_BATCH, CONV_W - 1, CONV_DIM), 1.0)
    cache_ckv = nrm(5, (N_ODD, n_phys, PAGE_SIZE, KV_LORA), 1.0)
    cache_krope = nrm(6, (N_ODD, n_phys, PAGE_SIZE, QK_ROPE), 1.0)
    page_table = jax.random.permutation(keys[7], n_phys)[:n_used].reshape(DEC_BATCH, n_pages).astype(jnp.int32)
    norm_ffn1 = 1.0 + nrm(8, (DEPTH, D_MODEL), 0.05)
    ffn1_w_in = nrm(9, (DEPTH, D_MODEL, 2 * D_FF), D_MODEL ** -0.5)
    ffn1_w_out = nrm(10, (DEPTH, D_FF, D_MODEL), D_FF ** -0.5)
    norm_mix = 1.0 + nrm(11, (DEPTH, D_MODEL), 0.05)
    norm_ffn2 = 1.0 + nrm(12, (DEPTH, D_MODEL), 0.05)
    ffn2_w_in = nrm(13, (DEPTH, D_MODEL, 2 * D_FF), D_MODEL ** -0.5)
    ffn2_w_out = nrm(14, (DEPTH, D_FF, D_MODEL), D_FF ** -0.5)
    hyb_w_in = nrm(15, (N_EVEN, D_MODEL, HYB_IN), D_MODEL ** -0.5)
    hyb_w_out = nrm(16, (N_EVEN, HYB_MIX, D_MODEL), HYB_MIX ** -0.5)
    hyb_conv_w = nrm(17, (N_EVEN, CONV_W, CONV_DIM), CONV_W ** -0.5)
    hyb_conv_b = nrm(18, (N_EVEN, CONV_DIM), 0.02)
    dt0 = jnp.exp(jax.random.uniform(keys[19], (N_EVEN, SSM_HEADS), f32,
                                     minval=math.log(1e-3), maxval=math.log(1e-1)))
    hyb_dt_bias = dt0 + jnp.log(-jnp.expm1(-dt0))
    hyb_a_log = jnp.log(jax.random.uniform(keys[20], (N_EVEN, SSM_HEADS), f32, minval=1.0, maxval=16.0))
    hyb_d_skip = 1.0 + nrm(21, (N_EVEN, SSM_HEADS), 0.1)
    hyb_norm_w = 1.0 + nrm(22, (N_EVEN, SSM_D_INNER), 0.05)
    mla_w_in = nrm(23, (N_ODD, D_MODEL, MLA_IN), D_MODEL ** -0.5)
    mla_q_norm_w = 1.0 + nrm(24, (N_ODD, Q_LORA), 0.05)
    mla_kv_norm_w = 1.0 + nrm(25, (N_ODD, KV_LORA), 0.05)
    mla_w_uq = nrm(26, (N_ODD, Q_LORA, MLA_HEADS * (QK_NOPE + QK_ROPE)), Q_LORA ** -0.5)
    mla_w_uk = nrm(27, (N_ODD, KV_LORA, MLA_HEADS, QK_NOPE), KV_LORA ** -0.5)
    mla_w_uv = nrm(28, (N_ODD, KV_LORA, MLA_HEADS, V_DIM), KV_LORA ** -0.5)
    mla_w_o = nrm(29, (N_ODD, MLA_HEADS * V_DIM, D_MODEL), (MLA_HEADS * V_DIM) ** -0.5)
    final_norm_w = 1.0 + nrm(30, (D_MODEL,), 0.05)
    return {'x_prompt': x_prompt, 'x_sample': x_sample, 'state_ret': state_ret, 'state_ssm': state_ssm,
            'state_conv': state_conv, 'cache_ckv': cache_ckv, 'cache_krope': cache_krope,
            'page_table': page_table, 'norm_ffn1': norm_ffn1, 'ffn1_w_in': ffn1_w_in,
            'ffn1_w_out': ffn1_w_out, 'norm_mix': norm_mix, 'norm_ffn2': norm_ffn2,
            'ffn2_w_in': ffn2_w_in, 'ffn2_w_out': ffn2_w_out, 'hyb_w_in': hyb_w_in,
            'hyb_w_out': hyb_w_out, 'hyb_conv_w': hyb_conv_w, 'hyb_conv_b': hyb_conv_b,
            'hyb_dt_bias': hyb_dt_bias, 'hyb_a_log': hyb_a_log, 'hyb_d_skip': hyb_d_skip,
            'hyb_norm_w': hyb_norm_w, 'mla_w_in': mla_w_in, 'mla_q_norm_w': mla_q_norm_w,
            'mla_kv_norm_w': mla_kv_norm_w, 'mla_w_uq': mla_w_uq, 'mla_w_uk': mla_w_uk,
            'mla_w_uv': mla_w_uv, 'mla_w_o': mla_w_o, 'final_norm_w': final_norm_w}


def reference(x_prompt, x_sample, state_ret, state_ssm, state_conv, cache_ckv, cache_krope, page_table,
              norm_ffn1, ffn1_w_in, ffn1_w_out, norm_mix, norm_ffn2, ffn2_w_in, ffn2_w_out,
              hyb_w_in, hyb_w_out, hyb_conv_w, hyb_conv_b, hyb_dt_bias, hyb_a_log, hyb_d_skip,
              hyb_norm_w, mla_w_in, mla_q_norm_w, mla_kv_norm_w, mla_w_uq, mla_w_uk, mla_w_uv,
              mla_w_o, final_norm_w):
    xp, xs = x_prompt, x_sample
    bp, sp = xp.shape[0], xp.shape[1]
    pos_p = jnp.arange(sp)
    pos_s = PAST_LEN + jnp.arange(xs.shape[1])
    ret_p, ret_s, ssm_p, ssm_s, conv_p, conv_s = [], [], [], [], [], []
    ckv_p, ckv_s, kr_p, kr_s = [], [], [], []
    for layer in range(DEPTH):
        j = layer // 2
        xp = xp + 0.5 * swiglu(rms_norm(xp, norm_ffn1[layer]), ffn1_w_in[layer], ffn1_w_out[layer])
        xs = xs + 0.5 * swiglu(rms_norm(xs, norm_ffn1[layer]), ffn1_w_in[layer], ffn1_w_out[layer])
        hp = rms_norm(xp, norm_mix[layer])
        hs = rms_norm(xs, norm_mix[layer])
        if layer % 2 == 0:
            wts = (hyb_w_in[j], hyb_w_out[j], hyb_conv_w[j], hyb_conv_b[j], hyb_dt_bias[j],
                   hyb_a_log[j], hyb_d_skip[j], hyb_norm_w[j])
            mp, r_p, s_p, c_p = hybrid_mixer(
                hp, pos_p, jnp.zeros((bp, RET_HEADS, RET_DK, RET_DV), hp.dtype),
                jnp.zeros((bp, SSM_HEADS, SSM_STATE, SSM_HEAD_DIM), hp.dtype),
                jnp.zeros((bp, CONV_W - 1, CONV_DIM), hp.dtype), *wts)
            ms, r_s, s_s, c_s = hybrid_mixer(hs, pos_s, state_ret[j], state_ssm[j], state_conv[j], *wts)
            ret_p.append(r_p); ret_s.append(r_s); ssm_p.append(s_p); ssm_s.append(s_s)
            conv_p.append(c_p); conv_s.append(c_s)
        else:
            qn, qr, ck, kr = mla_project(hp, pos_p, mla_w_in[j], mla_q_norm_w[j], mla_kv_norm_w[j], mla_w_uq[j])
            op = mla_prompt_attend(qn, qr, ck, kr, mla_w_uk[j], mla_w_uv[j])
            mp = op.reshape(bp, sp, MLA_HEADS * V_DIM) @ mla_w_o[j]
            ckv_p.append(ck); kr_p.append(kr)
            qn, qr, ck, kr = mla_project(hs, pos_s, mla_w_in[j], mla_q_norm_w[j], mla_kv_norm_w[j], mla_w_uq[j])
            os_ = mla_sample_attend(qn, qr, ck, kr, page_table, cache_ckv, cache_krope, j,
                                    mla_w_uk[j], mla_w_uv[j])
            ms = os_.reshape(xs.shape[0], xs.shape[1], MLA_HEADS * V_DIM) @ mla_w_o[j]
            ckv_s.append(ck); kr_s.append(kr)
        xp = xp + mp
        xs = xs + ms
        xp = xp + 0.5 * swiglu(rms_norm(xp, norm_ffn2[layer]), ffn2_w_in[layer], ffn2_w_out[layer])
        xs = xs + 0.5 * swiglu(rms_norm(xs, norm_ffn2[layer]), ffn2_w_in[layer], ffn2_w_out[layer])
    y_prompt = rms_norm(xp, final_norm_w)
    y_sample = rms_norm(xs, final_norm_w)
    return (y_prompt, y_sample, jnp.stack(ret_p), jnp.stack(ret_s), jnp.stack(ssm_p), jnp.stack(ssm_s),
            jnp.stack(conv_p), jnp.stack(conv_s), jnp.stack(ckv_p), jnp.stack(ckv_s),
            jnp.stack(kr_p), jnp.stack(kr_s))
```

```python
import functools
import math

import jax
import jax.numpy as jnp
from jax import lax
from jax.experimental import pallas as pl
from jax.experimental.pallas import tpu as pltpu

F32 = jnp.float32
BF16 = jnp.bfloat16

D_MODEL = 1024
D_FF = 2816
RMS_EPS = 1e-6
ROPE_THETA = 10000.0
CHUNK = 128
RET_HEADS = 4
RET_DK = 128
RET_DV = 256
SSM_HEADS = 16
SSM_HEAD_DIM = 64
SSM_D_INNER = 1024
SSM_STATE = 64
SSM_GROUPS = 2
CONV_W = 4
CONV_DIM = 1280
HYB_IN = 5392
HYB_IN_PAD = 5632
HYB_MIX = 2048
MLA_HEADS = 16
Q_LORA = 512
KV_LORA = 256
QK_NOPE = 64
QK_ROPE = 32
V_DIM = 64
MLA_IN = 800
MLA_IN_PAD = 896
MLA_SCALE = (QK_NOPE + QK_ROPE) ** -0.5
PAST_LEN = 16384
PAGE_SIZE = 128

OFF_XBC = 0
OFF_DT = 1280
OFF_Q = 1408
OFF_K = 1920
OFF_V = 2560
OFF_G = 3584
OFF_Z = 4608

LANES = 128
SUBLANES = 8
VMEM_LIMIT_BYTES = 56 * 1024 * 1024

NT_DIMS = (((1,), (1,)), ((), ()))


def _cparams(sem):
    return pltpu.CompilerParams(dimension_semantics=sem, vmem_limit_bytes=VMEM_LIMIT_BYTES)


def _silu(x):
    return x / (1.0 + jnp.exp(-x))


def _rms(x, w):
    return x * lax.rsqrt(jnp.mean(x * x, axis=-1, keepdims=True) + RMS_EPS) * w


def _bf16_round(x):
    return x.astype(BF16).astype(F32)


def _pick_tile(m, pref):
    t = min(m, pref)
    while m % t:
        t //= 2
    return t


def _ffn_kernel(x_ref, nw_ref, wg_ref, wu_ref, wo_ref, fw_ref, o_ref, h_ref, acc_ref, *, final_norm):
    j = pl.program_id(1)

    @pl.when(j == 0)
    def _():
        h_ref[...] = _rms(x_ref[...], nw_ref[...]).astype(BF16)
        acc_ref[...] = jnp.zeros_like(acc_ref)

    h = h_ref[...]
    g = jnp.dot(h, wg_ref[...], preferred_element_type=F32)
    u = jnp.dot(h, wu_ref[...], preferred_element_type=F32)
    a = (_silu(g) * u).astype(BF16)
    acc_ref[...] += jnp.dot(a, wo_ref[...], preferred_element_type=F32)

    @pl.when(j == pl.num_programs(1) - 1)
    def _():
        y = x_ref[...] + 0.5 * acc_ref[...]
        if final_norm:
            y = _rms(y, fw_ref[...])
        o_ref[...] = y


def _ffn(x, nw, w_in, w_out, fw, *, final_norm=False, tf=1408):
    m, d = x.shape
    tm = _pick_tile(m, 512)
    nj = D_FF // tf
    return pl.pallas_call(
        functools.partial(_ffn_kernel, final_norm=final_norm),
        grid=(m // tm, nj),
        in_specs=[
            pl.BlockSpec((tm, d), lambda i, j: (i, 0)),
            pl.BlockSpec((1, d), lambda i, j: (0, 0)),
            pl.BlockSpec((d, tf), lambda i, j: (0, j)),
            pl.BlockSpec((d, tf), lambda i, j: (0, j + nj)),
            pl.BlockSpec((tf, d), lambda i, j: (j, 0)),
            pl.BlockSpec((1, d), lambda i, j: (0, 0)),
        ],
        out_specs=pl.BlockSpec((tm, d), lambda i, j: (i, 0)),
        out_shape=jax.ShapeDtypeStruct((m, d), F32),
        scratch_shapes=[pltpu.VMEM((tm, d), BF16), pltpu.VMEM((tm, d), F32)],
        compiler_params=_cparams(("parallel", "arbitrary")),
        name="ffn",
    )(x, nw, w_in, w_in, w_out, fw)


def _norm_matmul_kernel(x_ref, nw_ref, w_ref, o_ref, h_ref):
    @pl.when(pl.program_id(1) == 0)
    def _():
        h_ref[...] = _rms(x_ref[...], nw_ref[...]).astype(BF16)

    o_ref[...] = jnp.dot(h_ref[...], w_ref[...], preferred_element_type=F32)


def _norm_matmul(x, nw, w, *, tn):
    m, d = x.shape
    n = w.shape[1]
    tm = _pick_tile(m, 512)
    return pl.pallas_call(
        _norm_matmul_kernel,
        grid=(m // tm, n // tn),
        in_specs=[
            pl.BlockSpec((tm, d), lambda i, j: (i, 0)),
            pl.BlockSpec((1, d), lambda i, j: (0, 0)),
            pl.BlockSpec((d, tn), lambda i, j: (0, j)),
        ],
        out_specs=pl.BlockSpec((tm, tn), lambda i, j: (i, j)),
        out_shape=jax.ShapeDtypeStruct((m, n), F32),
        scratch_shapes=[pltpu.VMEM((tm, d), BF16)],
        compiler_params=_cparams(("parallel", "arbitrary")),
        name="norm_matmul",
    )(x, nw, w)


def _matmul_res_kernel(a_ref, w_ref, r_ref, o_ref):
    o_ref[...] = r_ref[...] + jnp.dot(a_ref[...].astype(BF16), w_ref[...], preferred_element_type=F32)


def _matmul_res(a, w, res):
    m, k = a.shape
    n = w.shape[1]
    tm = _pick_tile(m, 512)
    return pl.pallas_call(
        _matmul_res_kernel,
        grid=(m // tm,),
        in_specs=[
            pl.BlockSpec((tm, k), lambda i: (i, 0)),
            pl.BlockSpec((k, n), lambda i: (0, 0)),
            pl.BlockSpec((tm, n), lambda i: (i, 0)),
        ],
        out_specs=pl.BlockSpec((tm, n), lambda i: (i, 0)),
        out_shape=jax.ShapeDtypeStruct((m, n), F32),
        compiler_params=_cparams(("parallel",)),
        name="matmul_res",
    )(a, w, res)


def _rope_full(x, cosf, sinf):
    return x * cosf + pltpu.roll(x, RET_DK // 2, 1) * sinf


def _ret_kernel(q_ref, k_ref, v_ref, g_ref, cos_ref, sin_ref, lg_ref, o_ref, so_ref, s_ref, *, cps):
    c = pl.program_id(2)

    @pl.when(c == 0)
    def _():
        s_ref[...] = jnp.zeros_like(s_ref)

    lg = lg_ref[0]
    ii = lax.broadcasted_iota(jnp.int32, (CHUNK, CHUNK), 0)
    jj = lax.broadcasted_iota(jnp.int32, (CHUNK, CHUNK), 1)
    seg = jnp.where(ii >= jj, jnp.exp((ii - jj).astype(F32) * lg), 0.0)
    ri = lax.broadcasted_iota(jnp.int32, (CHUNK, RET_DK), 0).astype(F32)
    qdec = jnp.exp((ri + 1.0) * lg)
    kdec = jnp.exp((CHUNK - 1.0 - ri) * lg)
    cdec = jnp.exp(CHUNK * lg)[:, 0:1]

    s = s_ref[...]
    for t in range(cps):
        rows = pl.ds(t * CHUNK, CHUNK)
        cosf = cos_ref[rows, :]
        sinf = sin_ref[rows, :]
        q = _rope_full(q_ref[0, rows, :], cosf, sinf)
        k = _rope_full(k_ref[0, rows, :], cosf, sinf) * (RET_DK ** -0.5)
        vb = v_ref[0, rows, :].astype(BF16)
        sc = lax.dot_general(q.astype(BF16), k.astype(BF16), NT_DIMS, preferred_element_type=F32) * seg
        y = jnp.dot(sc.astype(BF16), vb, preferred_element_type=F32)
        y = y + jnp.dot((q * qdec).astype(BF16), s.astype(BF16), preferred_element_type=F32)
        kend_t = jnp.transpose(k * kdec).astype(BF16)
        s = cdec * s + jnp.dot(kend_t, vb, preferred_element_type=F32)
        y = y * lax.rsqrt(jnp.mean(y * y, axis=-1, keepdims=True) + RMS_EPS)
        o_ref[0, rows, :] = (y * _silu(g_ref[0, rows, :])).astype(BF16)
    s_ref[...] = s

    @pl.when(c == pl.num_programs(2) - 1)
    def _():
        so_ref[0, 0] = s


def _retention_prompt(proj, cosf, sinf, lg_rows, *, cps=4):
    b, l, _ = proj.shape
    rows = cps * CHUNK
    qb = OFF_Q // RET_DK
    kb = OFF_K // RET_DK
    vb = OFF_V // RET_DV
    gb = OFF_G // RET_DV
    return pl.pallas_call(
        functools.partial(_ret_kernel, cps=cps),
        grid=(b, RET_HEADS, l // rows),
        in_specs=[
            pl.BlockSpec((1, rows, RET_DK), lambda i, h, c: (i, c, qb + h)),
            pl.BlockSpec((1, rows, RET_DK), lambda i, h, c: (i, c, kb + h)),
            pl.BlockSpec((1, rows, RET_DV), lambda i, h, c: (i, c, vb + h)),
            pl.BlockSpec((1, rows, RET_DV), lambda i, h, c: (i, c, gb + h)),
            pl.BlockSpec((rows, RET_DK), lambda i, h, c: (c, 0)),
            pl.BlockSpec((rows, RET_DK), lambda i, h, c: (c, 0)),
            pl.BlockSpec((1, 1, LANES), lambda i, h, c: (h, 0, 0)),
        ],
        out_specs=[
            pl.BlockSpec((1, rows, RET_DV), lambda i, h, c: (i, c, h)),
            pl.BlockSpec((1, 1, RET_DK, RET_DV), lambda i, h, c: (i, h, 0, 0)),
        ],
        out_shape=[
            jax.ShapeDtypeStruct((b, l, RET_HEADS * RET_DV), BF16),
            jax.ShapeDtypeStruct((b, RET_HEADS, RET_DK, RET_DV), F32),
        ],
        scratch_shapes=[pltpu.VMEM((RET_DK, RET_DV), F32)],
        compiler_params=_cparams(("parallel", "parallel", "arbitrary")),
        name="retention_prompt",
    )(proj, proj, proj, proj, cosf, sinf, lg_rows)


def _split3(x):
    hi = x.astype(BF16)
    r1 = x - hi.astype(F32)
    mid = r1.astype(BF16)
    lo = (r1 - mid.astype(F32)).astype(BF16)
    return hi, mid, lo


def _cumsum_rows(tril_bf, x):
    hi, mid, lo = _split3(x)
    out = jnp.dot(tril_bf, lo, preferred_element_type=F32)
    out = out + jnp.dot(tril_bf, mid, preferred_element_type=F32)
    return out + jnp.dot(tril_bf, hi, preferred_element_type=F32)


def _shift_rows(cur, tail, s):
    r = pltpu.roll(cur, s, 0)
    pt = pltpu.roll(tail, s, 0)
    row = lax.broadcasted_iota(jnp.int32, (SUBLANES, cur.shape[1]), 0)
    top = jnp.where(row < s, pt, r[0:SUBLANES])
    return jnp.concatenate([top, r[SUBLANES:]], axis=0)


def _softplus(x):
    return jnp.maximum(x, 0.0) + jnp.log1p(jnp.exp(-jnp.abs(x)))


def _ssd_kernel(xbc_ref, z0_ref, z1_ref, dt_ref, cw_ref, cb_ref, dtb_ref, a_ref, dsk_ref, nw_ref,
                o_ref, so_ref, co_ref, s_ref, tail_ref):
    c = pl.program_id(1)
    z_refs = (z0_ref, z1_ref)

    @pl.when(c == 0)
    def _():
        s_ref[...] = jnp.zeros_like(s_ref)
        tail_ref[...] = jnp.zeros_like(tail_ref)

    raw = xbc_ref[0]
    tail = tail_ref[...]
    acc = raw * cw_ref[CONV_W - 1:CONV_W, :] + cb_ref[...]
    for s in range(1, CONV_W):
        acc = acc + _shift_rows(raw, tail, s) * cw_ref[CONV_W - 1 - s:CONV_W - s, :]
    xbc = _silu(acc)
    tail_ref[...] = raw[CHUNK - SUBLANES:, :]

    @pl.when(c == pl.num_programs(1) - 1)
    def _():
        co_ref[0] = raw[CHUNK - (CONV_W - 1):, :]

    ii = lax.broadcasted_iota(jnp.int32, (CHUNK, CHUNK), 0)
    jj = lax.broadcasted_iota(jnp.int32, (CHUNK, CHUNK), 1)
    causal = ii >= jj
    tril_bf = jnp.where(causal, 1.0, 0.0).astype(BF16)

    dt = _softplus(dt_ref[0] + dtb_ref[...])
    la = dt * a_ref[...]
    cum = _cumsum_rows(tril_bf, la)
    cum_t = jnp.transpose(cum)
    cum_last = cum[CHUNK - 1:CHUNK, :]

    bc_off = SSM_D_INNER
    b_all = xbc[:, bc_off:bc_off + LANES]
    c_all = xbc[:, bc_off + LANES:bc_off + 2 * LANES]
    b_all_t = jnp.transpose(b_all)
    rep = SSM_HEADS // SSM_GROUPS
    gw = rep * SSM_HEAD_DIM
    for g in range(SSM_GROUPS):
        bg = b_all[:, g * SSM_STATE:(g + 1) * SSM_STATE]
        cg = c_all[:, g * SSM_STATE:(g + 1) * SSM_STATE]
        bg_t = b_all_t[g * SSM_STATE:(g + 1) * SSM_STATE, :]
        cb = lax.dot_general(cg.astype(BF16), bg.astype(BF16), NT_DIMS, preferred_element_type=F32)
        ys = []
        for hh in range(rep):
            h = g * rep + hh
            cc = cum[:, h:h + 1]
            cr = cum_t[h:h + 1, :]
            cl = cum_last[:, h:h + 1]
            seg = jnp.exp(jnp.where(causal, cc - cr, -jnp.inf))
            xh = xbc[:, h * SSM_HEAD_DIM:(h + 1) * SSM_HEAD_DIM]
            xdt = (xh * dt[:, h:h + 1]).astype(BF16)
            sh = s_ref[h]
            y = jnp.dot((cb * seg).astype(BF16), xdt, preferred_element_type=F32)
            y = y + jnp.dot((cg * jnp.exp(cc)).astype(BF16), sh.astype(BF16), preferred_element_type=F32)
            y = y + dsk_ref[:, h:h + 1] * xh
            kend_t = (bg_t * jnp.exp(cl - cr)).astype(BF16)
            s_ref[h] = jnp.exp(cl) * sh + jnp.dot(kend_t, xdt, preferred_element_type=F32)
            ys.append(y)
        yg = jnp.concatenate(ys, axis=1) * _silu(z_refs[g][0])
        yg = yg * lax.rsqrt(jnp.mean(yg * yg, axis=-1, keepdims=True) + RMS_EPS)
        o_ref[0, :, g * gw:(g + 1) * gw] = (yg * nw_ref[:, g * gw:(g + 1) * gw]).astype(BF16)

    @pl.when(c == pl.num_programs(1) - 1)
    def _():
        so_ref[0] = s_ref[...]


def _ssd_prompt(proj, conv_w, conv_b, dtb_row, a_row, dsk_row, norm_w):
    b, l, _ = proj.shape
    gw = SSM_D_INNER // SSM_GROUPS
    full = lambda i, c: (0, 0)
    return pl.pallas_call(
        _ssd_kernel,
        grid=(b, l // CHUNK),
        in_specs=[
            pl.BlockSpec((1, CHUNK, CONV_DIM), lambda i, c: (i, c, OFF_XBC // CONV_DIM)),
            pl.BlockSpec((1, CHUNK, gw), lambda i, c: (i, c, OFF_Z // gw)),
            pl.BlockSpec((1, CHUNK, gw), lambda i, c: (i, c, OFF_Z // gw + 1)),
            pl.BlockSpec((1, CHUNK, LANES), lambda i, c: (i, c, OFF_DT // LANES)),
            pl.BlockSpec((CONV_W, CONV_DIM), full),
            pl.BlockSpec((1, CONV_DIM), full),
            pl.BlockSpec((1, LANES), full),
            pl.BlockSpec((1, LANES), full),
            pl.BlockSpec((1, LANES), full),
            pl.BlockSpec((1, SSM_D_INNER), full),
        ],
        out_specs=[
            pl.BlockSpec((1, CHUNK, SSM_D_INNER), lambda i, c: (i, c, 0)),
            pl.BlockSpec((1, SSM_HEADS, SSM_STATE, SSM_HEAD_DIM), lambda i, c: (i, 0, 0, 0)),
            pl.BlockSpec((1, CONV_W - 1, CONV_DIM), lambda i, c: (i, 0, 0)),
        ],
        out_shape=[
            jax.ShapeDtypeStruct((b, l, SSM_D_INNER), BF16),
            jax.ShapeDtypeStruct((b, SSM_HEADS, SSM_STATE, SSM_HEAD_DIM), F32),
            jax.ShapeDtypeStruct((b, CONV_W - 1, CONV_DIM), F32),
        ],
        scratch_shapes=[
            pltpu.VMEM((SSM_HEADS, SSM_STATE, SSM_HEAD_DIM), F32),
            pltpu.VMEM((SUBLANES, CONV_DIM), F32),
        ],
        compiler_params=_cparams(("parallel", "arbitrary")),
        name="ssd_prompt",
    )(proj, proj, proj, proj, conv_w, conv_b, dtb_row, a_row, dsk_row, norm_w)


def _bcast_rows(x, n):
    return jnp.broadcast_to(x, (n, x.shape[1]))


def _column_matrix(row):
    return jnp.transpose(_bcast_rows(row, LANES))


def _hyb_decode_kernel(row_ref, sr_ref, ss_ref, cs_ref, cos_ref, sin_ref, lg_ref, cw_ref, cb_ref,
                       dtb_ref, a_ref, dsk_ref, nw_ref, o_ref, sro_ref, sso_ref, co_ref):
    row = row_ref[0]
    cosf = cos_ref[...]
    sinf = sin_ref[...]
    for h in range(RET_HEADS):
        q = _rope_full(_bcast_rows(row[:, OFF_Q + h * RET_DK:OFF_Q + (h + 1) * RET_DK], SUBLANES), cosf, sinf)
        k = _rope_full(_bcast_rows(row[:, OFF_K + h * RET_DK:OFF_K + (h + 1) * RET_DK], SUBLANES), cosf, sinf)
        k = k * (RET_DK ** -0.5)
        v = _bf16_round(row[:, OFF_V + h * RET_DV:OFF_V + (h + 1) * RET_DV])
        g = row[:, OFF_G + h * RET_DV:OFF_G + (h + 1) * RET_DV]
        gamma = jnp.exp(lg_ref[h:h + 1, :])
        qb = _bf16_round(q)
        kb = _bf16_round(k)
        kcol = _column_matrix(kb[0:1])
        s0 = sr_ref[0, h]
        sro_ref[0, h] = gamma[:, 0:1] * s0 + jnp.concatenate([kcol, kcol], axis=1) * v
        qcol = _column_matrix(_bf16_round(q * gamma)[0:1])
        y = jnp.sum(jnp.concatenate([qcol, qcol], axis=1) * _bf16_round(s0), axis=0, keepdims=True)
        score = jnp.sum(qb * kb, axis=-1, keepdims=True)[0:1]
        y = y + _bf16_round(score) * v
        y = y * lax.rsqrt(jnp.mean(y * y, axis=-1, keepdims=True) + RMS_EPS)
        o_ref[0, :, h * RET_DV:(h + 1) * RET_DV] = y * _silu(g)
    cs = cs_ref[0]
    raw = row[:, OFF_XBC:OFF_XBC + CONV_DIM]
    acc = raw * cw_ref[CONV_W - 1:CONV_W, :] + cb_ref[...]
    for w in range(CONV_W - 1):
        acc = acc + cs[w:w + 1, :] * cw_ref[w:w + 1, :]
    xbc = _silu(acc)
    co_ref[0, 0:CONV_W - 2, :] = cs[1:CONV_W - 1, :]
    co_ref[0, CONV_W - 2:CONV_W - 1, :] = raw
    dt = _softplus(row[:, OFF_DT:OFF_DT + LANES] + dtb_ref[...])
    la = dt * a_ref[...]
    dec = jnp.exp(la)
    b_all = _bf16_round(xbc[:, SSM_D_INNER:SSM_D_INNER + LANES])
    c_all = xbc[:, SSM_D_INNER + LANES:SSM_D_INNER + 2 * LANES]
    bcol = _column_matrix(b_all)
    ccol = _column_matrix(_bf16_round(c_all))
    rep = SSM_HEADS // SSM_GROUPS
    ys = []
    for h in range(SSM_HEADS):
        g = h // rep
        s0 = ss_ref[0, h]
        xh = xbc[:, h * SSM_HEAD_DIM:(h + 1) * SSM_HEAD_DIM]
        xdt = _bf16_round(xh * dt[:, h:h + 1])
        dech = dec[:, h:h + 1]
        bg = b_all[:, g * SSM_STATE:(g + 1) * SSM_STATE]
        cg = c_all[:, g * SSM_STATE:(g + 1) * SSM_STATE]
        sso_ref[0, h] = dech * s0 + bcol[g * SSM_STATE:(g + 1) * SSM_STATE, 0:SSM_HEAD_DIM] * xdt
        y = dech * jnp.sum(ccol[g * SSM_STATE:(g + 1) * SSM_STATE, 0:SSM_HEAD_DIM] * _bf16_round(s0),
                           axis=0, keepdims=True)
        score = jnp.sum(_bf16_round(cg) * bg, axis=-1, keepdims=True)
        ys.append(y + _bf16_round(score) * xdt + dsk_ref[:, h:h + 1] * xh)
    gw = rep * SSM_HEAD_DIM
    for g in range(SSM_GROUPS):
        yg = jnp.concatenate(ys[g * rep:(g + 1) * rep], axis=1)
        yg = yg * _silu(row[:, OFF_Z + g * gw:OFF_Z + (g + 1) * gw])
        yg = yg * lax.rsqrt(jnp.mean(yg * yg, axis=-1, keepdims=True) + RMS_EPS)
        lo = RET_HEADS * RET_DV + g * gw
        o_ref[0, :, lo:lo + gw] = yg * nw_ref[:, g * gw:(g + 1) * gw]


def _hybrid_decode(proj_s, state_ret, state_ssm, state_conv, cos_row, sin_row, lg_rows, conv_w, conv_b,
                   dtb_row, a_row, dsk_row, norm_w):
    nb = proj_s.shape[0]
    full = lambda i: (0, 0)
    return pl.pallas_call(
        _hyb_decode_kernel,
        grid=(nb,),
        in_specs=[
            pl.BlockSpec((1, 1, HYB_IN_PAD), lambda i: (i, 0, 0)),
            pl.BlockSpec((1, RET_HEADS, RET_DK, RET_DV), lambda i: (i, 0, 0, 0)),
            pl.BlockSpec((1, SSM_HEADS, SSM_STATE, SSM_HEAD_DIM), lambda i: (i, 0, 0, 0)),
            pl.BlockSpec((1, CONV_W - 1, CONV_DIM), lambda i: (i, 0, 0)),
            pl.BlockSpec((1, RET_DK), full),
            pl.BlockSpec((1, RET_DK), full),
            pl.BlockSpec((RET_HEADS, LANES), full),
            pl.BlockSpec((CONV_W, CONV_DIM), full),
            pl.BlockSpec((1, CONV_DIM), full),
            pl.BlockSpec((1, LANES), full),
            pl.BlockSpec((1, LANES), full),
            pl.BlockSpec((1, LANES), full),
            pl.BlockSpec((1, SSM_D_INNER), full),
        ],
        out_specs=[
            pl.BlockSpec((1, 1, HYB_MIX), lambda i: (i, 0, 0)),
            pl.BlockSpec((1, RET_HEADS, RET_DK, RET_DV), lambda i: (i, 0, 0, 0)),
            pl.BlockSpec((1, SSM_HEADS, SSM_STATE, SSM_HEAD_DIM), lambda i: (i, 0, 0, 0)),
            pl.BlockSpec((1, CONV_W - 1, CONV_DIM), lambda i: (i, 0, 0)),
        ],
        out_shape=[
            jax.ShapeDtypeStruct((nb, 1, HYB_MIX), F32),
            jax.ShapeDtypeStruct(state_ret.shape, F32),
            jax.ShapeDtypeStruct(state_ssm.shape, F32),
            jax.ShapeDtypeStruct(state_conv.shape, F32),
        ],
        compiler_params=_cparams(("parallel",)),
        name="hybrid_decode",
    )(proj_s.reshape(nb, 1, HYB_IN_PAD), state_ret, state_ssm, state_conv, cos_row, sin_row, lg_rows,
      conv_w, conv_b, dtb_row, a_row, dsk_row, norm_w)


def _rope_group(x, c, s1, s2):
    half = QK_ROPE // 2
    return x * c + pltpu.roll(x, LANES - half, 1) * s1 + pltpu.roll(x, half, 1) * s2


def _mla_in_kernel(x_ref, nw_ref, w_ref, qnw_ref, kvnw_ref, c_ref, s1_ref, s2_ref,
                   cq_ref, ckv_ref, kr_ref, krp_ref):
    h = _rms(x_ref[...], nw_ref[...]).astype(BF16)
    p = jnp.dot(h, w_ref[...], preferred_element_type=F32)
    cq_ref[...] = _rms(p[:, :Q_LORA], qnw_ref[...]).astype(BF16)
    ckv_ref[...] = _rms(p[:, Q_LORA:Q_LORA + KV_LORA], kvnw_ref[...])
    kr = _rope_group(p[:, Q_LORA + KV_LORA:], c_ref[...], s1_ref[...], s2_ref[...])
    kr_ref[...] = kr[:, :QK_ROPE]
    krp_ref[...] = kr.astype(BF16)


def _mla_in(x, nw, w, qnw, kvnw, tabs):
    m, d = x.shape
    tm = _pick_tile(m, 512)
    nt = tabs[0].shape[0] // tm
    full = lambda i: (0, 0)
    tab = pl.BlockSpec((tm, LANES), lambda i: (i % nt, 0))
    return pl.pallas_call(
        _mla_in_kernel,
        grid=(m // tm,),
        in_specs=[
            pl.BlockSpec((tm, d), lambda i: (i, 0)),
            pl.BlockSpec((1, d), full),
            pl.BlockSpec((d, MLA_IN_PAD), full),
            pl.BlockSpec((1, Q_LORA), full),
            pl.BlockSpec((1, KV_LORA), full),
            tab, tab, tab,
        ],
        out_specs=[
            pl.BlockSpec((tm, Q_LORA), lambda i: (i, 0)),
            pl.BlockSpec((tm, KV_LORA), lambda i: (i, 0)),
            pl.BlockSpec((tm, QK_ROPE), lambda i: (i, 0)),
            pl.BlockSpec((tm, LANES), lambda i: (i, 0)),
        ],
        out_shape=[
            jax.ShapeDtypeStruct((m, Q_LORA), BF16),
            jax.ShapeDtypeStruct((m, KV_LORA), F32),
            jax.ShapeDtypeStruct((m, QK_ROPE), F32),
            jax.ShapeDtypeStruct((m, LANES), BF16),
        ],
        compiler_params=_cparams(("parallel",)),
        name="mla_in",
    )(x, nw, w, qnw, kvnw, *tabs)


Q_HEAD_PAD = 2 * LANES
Q_TN = 2 * Q_HEAD_PAD


def _mla_q_kernel(cq_ref, w_ref, c_ref, s1_ref, s2_ref, o_ref):
    p = jnp.dot(cq_ref[...], w_ref[...], preferred_element_type=F32)
    for gi in range(Q_TN // LANES):
        x = p[:, gi * LANES:(gi + 1) * LANES]
        if gi % 2 == 1:
            x = _rope_group(x, c_ref[...], s1_ref[...], s2_ref[...])
        o_ref[:, gi * LANES:(gi + 1) * LANES] = (x * MLA_SCALE).astype(BF16)


def _mla_q(cq, wq, tabs):
    m, k = cq.shape
    n = wq.shape[1]
    tm = _pick_tile(m, 512)
    nt = tabs[0].shape[0] // tm
    tab = pl.BlockSpec((tm, LANES), lambda i, j: (i % nt, 0))
    return pl.pallas_call(
        _mla_q_kernel,
        grid=(m // tm, n // Q_TN),
        in_specs=[
            pl.BlockSpec((tm, k), lambda i, j: (i, 0)),
            pl.BlockSpec((k, Q_TN), lambda i, j: (0, j)),
            tab, tab, tab,
        ],
        out_specs=pl.BlockSpec((tm, Q_TN), lambda i, j: (i, j)),
        out_shape=jax.ShapeDtypeStruct((m, n), BF16),
        compiler_params=_cparams(("parallel", "arbitrary")),
        name="mla_q",
    )(cq, wq, *tabs)


def _matmul_cast_kernel(x_ref, w_ref, o_ref):
    o_ref[...] = jnp.dot(x_ref[...].astype(BF16), w_ref[...], preferred_element_type=F32).astype(BF16)


def _matmul_cast(x, w):
    m, k = x.shape
    n = w.shape[1]
    tm = _pick_tile(m, 512)
    return pl.pallas_call(
        _matmul_cast_kernel,
        grid=(m // tm,),
        in_specs=[pl.BlockSpec((tm, k), lambda i: (i, 0)), pl.BlockSpec((k, n), lambda i: (0, 0))],
        out_specs=pl.BlockSpec((tm, n), lambda i: (i, 0)),
        out_shape=jax.ShapeDtypeStruct((m, n), BF16),
        compiler_params=_cparams(("parallel",)),
        name="matmul_cast",
    )(x, w)


def _flash_kernel(qi_ref, ki_ref, q_ref, kn_ref, kr_ref, v_ref, o_ref, m_ref, l_ref, acc_ref, *, t):
    step = pl.program_id(2)
    qi = qi_ref[step]
    ki = ki_ref[step]

    @pl.when(ki == 0)
    def _():
        m_ref[...] = jnp.full_like(m_ref, -jnp.inf)
        l_ref[...] = jnp.zeros_like(l_ref)
        acc_ref[...] = jnp.zeros_like(acc_ref)

    first = lax.broadcasted_iota(jnp.int32, (t, LANES), 1) < V_DIM

    def update(masked):
        kcat = jnp.concatenate([kn_ref[0], kr_ref[0]], axis=1)
        v = v_ref[0]
        pvs, alphas = [], []
        for hh in range(2):
            q = q_ref[0, :, hh * Q_HEAD_PAD:(hh + 1) * Q_HEAD_PAD]
            s = lax.dot_general(q, kcat, NT_DIMS, preferred_element_type=F32)
            if masked:
                row = lax.broadcasted_iota(jnp.int32, (t, t), 0)
                col = lax.broadcasted_iota(jnp.int32, (t, t), 1)
                s = jnp.where(row >= col, s, -jnp.inf)
            m_prev = m_ref[hh]
            m_new = jnp.maximum(m_prev, jnp.max(s, axis=-1, keepdims=True))
            alpha = jnp.exp(m_prev - m_new)
            p = jnp.exp(s - m_new)
            l_ref[hh] = alpha * l_ref[hh] + jnp.sum(p, axis=-1, keepdims=True)
            m_ref[hh] = m_new
            pvs.append(jnp.dot(p.astype(BF16), v, preferred_element_type=F32))
            alphas.append(alpha)
        acc_ref[...] = (jnp.where(first, alphas[0], alphas[1]) * acc_ref[...]
                        + jnp.where(first, pvs[0], pvs[1]))

    @pl.when(ki < qi)
    def _():
        update(False)

    @pl.when(ki == qi)
    def _():
        update(True)
        o_ref[0] = (acc_ref[...] / jnp.where(first, l_ref[0], l_ref[1])).astype(BF16)


def _flash_prompt(qp, kv, krp, *, t=512):
    b, l, _ = qp.shape
    n = l // t
    qi_tab = jnp.asarray([qi for qi in range(n) for _ in range(qi + 1)], jnp.int32)
    ki_tab = jnp.asarray([ki for qi in range(n) for ki in range(qi + 1)], jnp.int32)
    npairs = MLA_HEADS // 2
    vb = MLA_HEADS * QK_NOPE // LANES
    grid_spec = pltpu.PrefetchScalarGridSpec(
        num_scalar_prefetch=2,
        grid=(b, npairs, int(qi_tab.shape[0])),
        in_specs=[
            pl.BlockSpec((1, t, Q_TN), lambda i, p, s, qt, kt: (i, qt[s], p)),
            pl.BlockSpec((1, t, LANES), lambda i, p, s, qt, kt: (i, kt[s], p)),
            pl.BlockSpec((1, t, LANES), lambda i, p, s, qt, kt: (i, kt[s], 0)),
            pl.BlockSpec((1, t, LANES), lambda i, p, s, qt, kt: (i, kt[s], vb + p)),
        ],
        out_specs=pl.BlockSpec((1, t, LANES), lambda i, p, s, qt, kt: (i, qt[s], p)),
        scratch_shapes=[
            pltpu.VMEM((2, t, 1), F32),
            pltpu.VMEM((2, t, 1), F32),
            pltpu.VMEM((t, LANES), F32),
        ],
    )
    return pl.pallas_call(
        functools.partial(_flash_kernel, t=t),
        grid_spec=grid_spec,
        out_shape=jax.ShapeDtypeStruct((b, l, MLA_HEADS * V_DIM), BF16),
        compiler_params=_cparams(("parallel", "parallel", "arbitrary")),
        name="mla_flash",
    )(qi_tab, ki_tab, qp, kv, krp, kv)


Q_CAT = KV_LORA + LANES


def _qlat_kernel(q_ref, w_ref, o_ref):
    for hh in range(2):
        qh = q_ref[:, hh * Q_HEAD_PAD:(hh + 1) * Q_HEAD_PAD]
        lat = jnp.dot(qh[:, :LANES], w_ref[...], preferred_element_type=F32)
        o_ref[hh] = jnp.concatenate([lat.astype(BF16), qh[:, LANES:]], axis=1)


def _qlat(qp_s, w_uk_t):
    nb = qp_s.shape[0]
    npairs = MLA_HEADS // 2
    return pl.pallas_call(
        _qlat_kernel,
        grid=(npairs,),
        in_specs=[
            pl.BlockSpec((nb, Q_TN), lambda p: (0, p)),
            pl.BlockSpec((LANES, KV_LORA), lambda p: (p, 0)),
        ],
        out_specs=pl.BlockSpec((2, nb, Q_CAT), lambda p: (p, 0, 0)),
        out_shape=jax.ShapeDtypeStruct((MLA_HEADS, nb, Q_CAT), BF16),
        compiler_params=_cparams(("parallel",)),
        name="mla_qlat",
    )(qp_s, w_uk_t)


def _olat_kernel(o_ref, w_ref, out_ref):
    nb = o_ref.shape[0]
    first = lax.broadcasted_iota(jnp.int32, (nb, LANES), 1) < V_DIM
    r0 = jnp.dot(o_ref[:, :KV_LORA], w_ref[...], preferred_element_type=F32)
    r1 = jnp.dot(o_ref[:, KV_LORA:], w_ref[...], preferred_element_type=F32)
    out_ref[...] = jnp.where(first, r0, r1).astype(BF16)


def _olat(o_lat, w_uv):
    nb = o_lat.shape[0]
    npairs = MLA_HEADS // 2
    return pl.pallas_call(
        _olat_kernel,
        grid=(npairs,),
        in_specs=[
            pl.BlockSpec((nb, 2 * KV_LORA), lambda p: (0, p)),
            pl.BlockSpec((KV_LORA, LANES), lambda p: (0, p)),
        ],
        out_specs=pl.BlockSpec((nb, LANES), lambda p: (0, p)),
        out_shape=jax.ShapeDtypeStruct((nb, MLA_HEADS * V_DIM), BF16),
        compiler_params=_cparams(("parallel",)),
        name="mla_olat",
    )(o_lat, w_uv)


PAGES_PER_CHUNK = 8


def _mla_decode_kernel(pt_ref, q_ref, cn_ref, kn_ref, ckv_hbm, kr_hbm, o_ref, ckbuf, krbuf, sem,
                       *, layer, nch):
    ppc = PAGES_PER_CHUNK
    b = pl.program_id(0)
    nb = pl.num_programs(0)

    def copies(bb, c, slot):
        out = []
        for i in range(ppc):
            pg = pt_ref[bb, c * ppc + i]
            out.append(pltpu.make_async_copy(ckv_hbm.at[layer, pg], ckbuf.at[slot, i], sem.at[slot]))
            out.append(pltpu.make_async_copy(kr_hbm.at[layer, pg], krbuf.at[slot, i], sem.at[slot]))
        return out

    def start(bb, c, slot):
        for cp in copies(bb, c, slot):
            cp.start()

    @pl.when(b == 0)
    def _():
        start(0, 0, 0)

    q = q_ref[0]
    rows = ppc * PAGE_SIZE

    def chunk(c, slot, carry):
        m_prev, l_prev, acc = carry

        @pl.when(c + 1 < nch)
        def _():
            start(b, c + 1, 1 - slot)

        @pl.when(jnp.logical_and(c + 1 == nch, b + 1 < nb))
        def _():
            start(b + 1, 0, 1 - slot)

        for cp in copies(b, c, slot):
            cp.wait()
        ck = ckbuf[slot].reshape(rows, KV_LORA).astype(BF16)
        kr = krbuf[slot].reshape(rows, QK_ROPE).astype(BF16)
        kr = jnp.concatenate([kr, jnp.zeros((rows, LANES - QK_ROPE), BF16)], axis=1)
        kcat = jnp.concatenate([ck, kr], axis=1)
        s = lax.dot_general(q, kcat, NT_DIMS, preferred_element_type=F32)
        m_new = jnp.maximum(m_prev, jnp.max(s, axis=-1, keepdims=True))
        alpha = jnp.exp(m_prev - m_new)
        p = jnp.exp(s - m_new)
        l_new = alpha * l_prev + jnp.sum(p, axis=-1, keepdims=True)
        acc = alpha * acc + jnp.dot(p.astype(BF16), ck, preferred_element_type=F32)
        return m_new, l_new, acc

    def body(i, carry):
        carry = chunk(2 * i, 0, carry)
        return chunk(2 * i + 1, 1, carry)

    init = (jnp.full((MLA_HEADS, 1), -jnp.inf, F32), jnp.zeros((MLA_HEADS, 1), F32),
            jnp.zeros((MLA_HEADS, KV_LORA), F32))
    m_prev, l_prev, acc = lax.fori_loop(0, nch // 2, body, init)

    cn = cn_ref[0]
    knew = jnp.concatenate([cn.astype(BF16), kn_ref[0]], axis=1).astype(F32)
    s_new = jnp.sum(q.astype(F32) * knew, axis=-1, keepdims=True)
    m_new = jnp.maximum(m_prev, s_new)
    alpha = jnp.exp(m_prev - m_new)
    p = jnp.exp(s_new - m_new)
    l_new = alpha * l_prev + p
    acc = alpha * acc + _bf16_round(p) * _bf16_round(cn)
    o_ref[0] = (acc / l_new).astype(BF16)


def _mla_decode(page_table, qcat, ckv_new, krp_new, cache_ckv, cache_krope, layer):
    nb, npages = page_table.shape
    assert npages % (2 * PAGES_PER_CHUNK) == 0
    nch = npages // PAGES_PER_CHUNK
    grid_spec = pltpu.PrefetchScalarGridSpec(
        num_scalar_prefetch=1,
        grid=(nb,),
        in_specs=[
            pl.BlockSpec((1, MLA_HEADS, Q_CAT), lambda i, pt: (i, 0, 0)),
            pl.BlockSpec((1, 1, KV_LORA), lambda i, pt: (i, 0, 0)),
            pl.BlockSpec((1, 1, LANES), lambda i, pt: (i, 0, 0)),
            pl.BlockSpec(memory_space=pl.ANY),
            pl.BlockSpec(memory_space=pl.ANY),
        ],
        out_specs=pl.BlockSpec((1, MLA_HEADS, KV_LORA), lambda i, pt: (i, 0, 0)),
        scratch_shapes=[
            pltpu.VMEM((2, PAGES_PER_CHUNK, PAGE_SIZE, KV_LORA), F32),
            pltpu.VMEM((2, PAGES_PER_CHUNK, PAGE_SIZE, QK_ROPE), F32),
            pltpu.SemaphoreType.DMA((2,)),
        ],
    )
    return pl.pallas_call(
        functools.partial(_mla_decode_kernel, layer=layer, nch=nch),
        grid_spec=grid_spec,
        out_shape=jax.ShapeDtypeStruct((nb, MLA_HEADS, KV_LORA), BF16),
        compiler_params=_cparams(("arbitrary",)),
        name="mla_decode",
    )(page_table, qcat, ckv_new.reshape(nb, 1, KV_LORA), krp_new.reshape(nb, 1, LANES),
      cache_ckv, cache_krope)


def _rope_angles(pos, half):
    inv = ROPE_THETA ** (-jnp.arange(half, dtype=F32) / half)
    ang = pos.astype(F32)[:, None] * inv[None, :]
    return jnp.cos(ang), jnp.sin(ang)


def _ret_tables(pos):
    cos, sin = _rope_angles(pos, RET_DK // 2)
    return jnp.concatenate([cos, cos], axis=1), jnp.concatenate([-sin, sin], axis=1)


def _mla_tables(pos, rows):
    half = QK_ROPE // 2
    cos, sin = _rope_angles(pos, half)
    n = pos.shape[0]
    c = jnp.concatenate([cos, cos, jnp.ones((n, LANES - QK_ROPE), F32)], axis=1)
    s1 = jnp.concatenate([-sin, jnp.zeros((n, LANES - half), F32)], axis=1)
    s2 = jnp.concatenate([jnp.zeros((n, half), F32), sin, jnp.zeros((n, LANES - QK_ROPE), F32)], axis=1)
    return tuple(jnp.broadcast_to(t, (rows, LANES)) if n == 1 else t for t in (c, s1, s2))


def _pad_lanes(v):
    return jnp.pad(v.astype(F32), (0, LANES - v.shape[0])).reshape(1, LANES)


def _hyb_w_in_layout(w):
    d = w.shape[0]
    qk = 2 * RET_HEADS * RET_DK
    vg = 2 * RET_HEADS * RET_DV
    q_k = w[:, :qk]
    v_g = w[:, qk:qk + vg]
    z = w[:, qk + vg:qk + vg + SSM_D_INNER]
    xbc = w[:, qk + vg + SSM_D_INNER:qk + vg + SSM_D_INNER + CONV_DIM]
    dt = w[:, qk + vg + SSM_D_INNER + CONV_DIM:]
    pieces = [xbc, dt, jnp.zeros((d, OFF_Q - OFF_DT - SSM_HEADS), w.dtype), q_k,
              jnp.zeros((d, OFF_V - OFF_K - RET_HEADS * RET_DK), w.dtype), v_g, z]
    out = jnp.concatenate(pieces, axis=1)
    assert out.shape[1] == HYB_IN_PAD
    return out.astype(BF16)


def _mla_wq_layout(w_uq):
    k = w_uq.shape[0]
    w3 = w_uq.reshape(k, MLA_HEADS, QK_NOPE + QK_ROPE)
    nope, ropew = w3[:, :, :QK_NOPE], w3[:, :, QK_NOPE:]
    z = jnp.zeros_like(nope)
    even = (jnp.arange(MLA_HEADS) % 2 == 0)[None, :, None]
    first = jnp.where(even, jnp.concatenate([nope, z], -1), jnp.concatenate([z, nope], -1))
    second = jnp.concatenate([ropew, jnp.zeros((k, MLA_HEADS, LANES - QK_ROPE), w_uq.dtype)], -1)
    return jnp.concatenate([first, second], -1).reshape(k, MLA_HEADS * Q_HEAD_PAD).astype(BF16)


def kernel(x_prompt, x_sample, state_ret, state_ssm, state_conv, cache_ckv, cache_krope, page_table,
           norm_ffn1, ffn1_w_in, ffn1_w_out, norm_mix, norm_ffn2, ffn2_w_in, ffn2_w_out,
           hyb_w_in, hyb_w_out, hyb_conv_w, hyb_conv_b, hyb_dt_bias, hyb_a_log, hyb_d_skip,
           hyb_norm_w, mla_w_in, mla_q_norm_w, mla_kv_norm_w, mla_w_uq, mla_w_uk, mla_w_uv,
           mla_w_o, final_norm_w):
    bp, sp, d = x_prompt.shape
    bs, ss, _ = x_sample.shape
    assert ss == 1 and sp % CHUNK == 0
    depth = norm_ffn1.shape[0]
    mp = bp * sp
    xp = x_prompt.reshape(mp, d)
    xs = x_sample.reshape(bs, d)
    pos_p = jnp.arange(sp)
    pos_s = PAST_LEN + jnp.arange(1)
    fw = final_norm_w.reshape(1, d)

    ret_cos_p, ret_sin_p = _ret_tables(pos_p)
    ret_cos_s, ret_sin_s = _ret_tables(pos_s)
    mla_tabs_p = _mla_tables(pos_p, sp)
    mla_tabs_s = _mla_tables(pos_s, bs)
    log_gamma = jnp.log1p(-jnp.exp2(-5.0 - jnp.arange(RET_HEADS, dtype=F32)))
    lg_rows = jnp.broadcast_to(log_gamma[:, None], (RET_HEADS, LANES))

    outs = {k: [] for k in ("ret_p", "ret_s", "ssm_p", "ssm_s", "conv_p", "conv_s",
                            "ckv_p", "ckv_s", "kr_p", "kr_s")}
    for layer in range(depth):
        j = layer // 2
        last = layer == depth - 1
        w1i, w1o = ffn1_w_in[layer].astype(BF16), ffn1_w_out[layer].astype(BF16)
        w2i, w2o = ffn2_w_in[layer].astype(BF16), ffn2_w_out[layer].astype(BF16)
        n1 = norm_ffn1[layer].reshape(1, d)
        nm = norm_mix[layer].reshape(1, d)
        n2 = norm_ffn2[layer].reshape(1, d)
        xp = _ffn(xp, n1, w1i, w1o, fw)
        xs = _ffn(xs, n1, w1i, w1o, fw)
        if layer % 2 == 0:
            w_in = _hyb_w_in_layout(hyb_w_in[j])
            w_out = hyb_w_out[j].astype(BF16)
            conv_w = hyb_conv_w[j].astype(F32)
            conv_b = hyb_conv_b[j].reshape(1, CONV_DIM).astype(F32)
            dtb_row = _pad_lanes(hyb_dt_bias[j])
            a_row = _pad_lanes(-jnp.exp(hyb_a_log[j].astype(F32)))
            dsk_row = _pad_lanes(hyb_d_skip[j])
            gnw = hyb_norm_w[j].reshape(1, SSM_D_INNER).astype(F32)
            proj = _norm_matmul(xp, nm, w_in, tn=1408).reshape(bp, sp, HYB_IN_PAD)
            o_ret, r_p = _retention_prompt(proj, ret_cos_p, ret_sin_p, lg_rows.reshape(RET_HEADS, 1, LANES))
            o_ssd, s_p, c_p = _ssd_prompt(proj, conv_w, conv_b, dtb_row, a_row, dsk_row, gnw)
            mixed = jnp.concatenate([o_ret, o_ssd], axis=-1).reshape(mp, HYB_MIX)
            xp = _matmul_res(mixed, w_out, xp)
            proj_s = _norm_matmul(xs, nm, w_in, tn=1408)
            mixed_s, r_s, s_s, c_s = _hybrid_decode(
                proj_s, state_ret[j], state_ssm[j], state_conv[j], ret_cos_s, ret_sin_s, lg_rows,
                conv_w, conv_b, dtb_row, a_row, dsk_row, gnw)
            xs = _matmul_res(mixed_s.reshape(bs, HYB_MIX), w_out, xs)
            outs["ret_p"].append(r_p); outs["ret_s"].append(r_s)
            outs["ssm_p"].append(s_p); outs["ssm_s"].append(s_s)
            outs["conv_p"].append(c_p); outs["conv_s"].append(c_s)
        else:
            w_in = jnp.pad(mla_w_in[j], ((0, 0), (0, MLA_IN_PAD - MLA_IN))).astype(BF16)
            qnw = mla_q_norm_w[j].reshape(1, Q_LORA)
            kvnw = mla_kv_norm_w[j].reshape(1, KV_LORA)
            wq = _mla_wq_layout(mla_w_uq[j])
            w_uk2 = mla_w_uk[j].reshape(KV_LORA, MLA_HEADS * QK_NOPE)
            w_uv2 = mla_w_uv[j].reshape(KV_LORA, MLA_HEADS * V_DIM)
            w_kv = jnp.concatenate([w_uk2, w_uv2], axis=1).astype(BF16)
            w_o = mla_w_o[j].astype(BF16)
            cq, ckv, kr, krp = _mla_in(xp, nm, w_in, qnw, kvnw, mla_tabs_p)
            qp = _mla_q(cq, wq, mla_tabs_p)
            kv = _matmul_cast(ckv, w_kv)
            o = _flash_prompt(qp.reshape(bp, sp, -1), kv.reshape(bp, sp, -1), krp.reshape(bp, sp, LANES))
            xp = _matmul_res(o.reshape(mp, MLA_HEADS * V_DIM), w_o, xp)
            outs["ckv_p"].append(ckv.reshape(bp, sp, KV_LORA))
            outs["kr_p"].append(kr.reshape(bp, sp, QK_ROPE))
            cq_s, ckv_s, kr_s, krp_s = _mla_in(xs, nm, w_in, qnw, kvnw, mla_tabs_s)
            qp_s = _mla_q(cq_s, wq, mla_tabs_s)
            qcat = jnp.transpose(_qlat(qp_s, w_uk2.T.astype(BF16)), (1, 0, 2))
            o_lat = _mla_decode(page_table, qcat, ckv_s, krp_s, cache_ckv, cache_krope, j)
            o_s = _olat(o_lat.reshape(bs, MLA_HEADS * KV_LORA), w_uv2.astype(BF16))
            xs = _matmul_res(o_s, w_o, xs)
            outs["ckv_s"].append(ckv_s.reshape(bs, 1, KV_LORA))
            outs["kr_s"].append(kr_s.reshape(bs, 1, QK_ROPE))
        xp = _ffn(xp, n2, w2i, w2o, fw, final_norm=last)
        xs = _ffn(xs, n2, w2i, w2o, fw, final_norm=last)
    if depth == 0:
        raise ValueError("depth must be positive")
    return (xp.reshape(bp, sp, d), xs.reshape(bs, 1, d),
            jnp.stack(outs["ret_p"]), jnp.stack(outs["ret_s"]),
            jnp.stack(outs["ssm_p"]), jnp.stack(outs["ssm_s"]),
            jnp.stack(outs["conv_p"]), jnp.stack(outs["conv_s"]),
            jnp.stack(outs["ckv_p"]), jnp.stack(outs["ckv_s"]),
            jnp.stack(outs["kr_p"]), jnp.stack(outs["kr_s"]))
```

```python
import functools
import math

import jax
import jax.numpy as jnp
from jax import lax
from jax.experimental import pallas as pl
from jax.experimental.pallas import tpu as pltpu

F32 = jnp.float32
BF16 = jnp.bfloat16

D_MODEL = 1024
D_FF = 2816
RMS_EPS = 1e-6
ROPE_THETA = 10000.0
CHUNK = 128
RET_HEADS = 4
RET_DK = 128
RET_DV = 256
SSM_HEADS = 16
SSM_HEAD_DIM = 64
SSM_D_INNER = 1024
SSM_STATE = 64
SSM_GROUPS = 2
CONV_W = 4
CONV_DIM = 1280
HYB_IN = 5392
HYB_IN_PAD = 5632
HYB_MIX = 2048
MLA_HEADS = 16
Q_LORA = 512
KV_LORA = 256
QK_NOPE = 64
QK_ROPE = 32
V_DIM = 64
MLA_IN = 800
MLA_IN_PAD = 896
MLA_SCALE = (QK_NOPE + QK_ROPE) ** -0.5
PAST_LEN = 16384
PAGE_SIZE = 128

OFF_XBC = 0
OFF_DT = 1280
OFF_Q = 1408
OFF_K = 1920
OFF_V = 2560
OFF_G = 3584
OFF_Z = 4608

LANES = 128
SUBLANES = 8
VMEM_LIMIT_BYTES = 56 * 1024 * 1024

NT_DIMS = (((1,), (1,)), ((), ()))


def _cparams(sem):
    return pltpu.CompilerParams(dimension_semantics=sem, vmem_limit_bytes=VMEM_LIMIT_BYTES)


def _silu(x):
    return x / (1.0 + jnp.exp(-x))


def _rms(x, w):
    return x * lax.rsqrt(jnp.mean(x * x, axis=-1, keepdims=True) + RMS_EPS) * w


def _bf16_round(x):
    return x.astype(BF16).astype(F32)


def _pick_tile(m, pref):
    t = min(m, pref)
    while m % t:
        t //= 2
    return t


def _ffn_kernel(x_ref, nw_ref, wg_ref, wu_ref, wo_ref, fw_ref, o_ref, h_ref, acc_ref, *, final_norm):
    j = pl.program_id(1)

    @pl.when(j == 0)
    def _():
        h_ref[...] = _rms(x_ref[...], nw_ref[...]).astype(BF16)
        acc_ref[...] = jnp.zeros_like(acc_ref)

    h = h_ref[...]
    g = jnp.dot(h, wg_ref[...], preferred_element_type=F32)
    u = jnp.dot(h, wu_ref[...], preferred_element_type=F32)
    a = (_silu(g) * u).astype(BF16)
    acc_ref[...] += jnp.dot(a, wo_ref[...], preferred_element_type=F32)

    @pl.when(j == pl.num_programs(1) - 1)
    def _():
        y = x_ref[...] + 0.5 * acc_ref[...]
        if final_norm:
            y = _rms(y, fw_ref[...])
        o_ref[...] = y


def _ffn(x, nw, w_in, w_out, fw, *, final_norm=False, tf=1408):
    m, d = x.shape
    tm = _pick_tile(m, 512)
    nj = D_FF // tf
    return pl.pallas_call(
        functools.partial(_ffn_kernel, final_norm=final_norm),
        grid=(m // tm, nj),
        in_specs=[
            pl.BlockSpec((tm, d), lambda i, j: (i, 0)),
            pl.BlockSpec((1, d), lambda i, j: (0, 0)),
            pl.BlockSpec((d, tf), lambda i, j: (0, j)),
            pl.BlockSpec((d, tf), lambda i, j: (0, j + nj)),
            pl.BlockSpec((tf, d), lambda i, j: (j, 0)),
            pl.BlockSpec((1, d), lambda i, j: (0, 0)),
        ],
        out_specs=pl.BlockSpec((tm, d), lambda i, j: (i, 0)),
        out_shape=jax.ShapeDtypeStruct((m, d), F32),
        scratch_shapes=[pltpu.VMEM((tm, d), BF16), pltpu.VMEM((tm, d), F32)],
        compiler_params=_cparams(("parallel", "arbitrary")),
        name="ffn",
    )(x, nw, w_in, w_in, w_out, fw)


def _norm_matmul_kernel(x_ref, nw_ref, w_ref, o_ref, h_ref):
    @pl.when(pl.program_id(1) == 0)
    def _():
        h_ref[...] = _rms(x_ref[...], nw_ref[...]).astype(BF16)

    o_ref[...] = jnp.dot(h_ref[...], w_ref[...], preferred_element_type=F32)


def _norm_matmul(x, nw, w, *, tn):
    m, d = x.shape
    n = w.shape[1]
    tm = _pick_tile(m, 512)
    return pl.pallas_call(
        _norm_matmul_kernel,
        grid=(m // tm, n // tn),
        in_specs=[
            pl.BlockSpec((tm, d), lambda i, j: (i, 0)),
            pl.BlockSpec((1, d), lambda i, j: (0, 0)),
            pl.BlockSpec((d, tn), lambda i, j: (0, j)),
        ],
        out_specs=pl.BlockSpec((tm, tn), lambda i, j: (i, j)),
        out_shape=jax.ShapeDtypeStruct((m, n), F32),
        scratch_shapes=[pltpu.VMEM((tm, d), BF16)],
        compiler_params=_cparams(("parallel", "arbitrary")),
        name="norm_matmul",
    )(x, nw, w)


def _matmul_res_kernel(*refs, nparts):
    a_refs, w_refs, r_ref, o_ref = refs[:nparts], refs[nparts:2 * nparts], refs[2 * nparts], refs[2 * nparts + 1]
    acc = r_ref[...]
    for a_ref, w_ref in zip(a_refs, w_refs):
        acc = acc + jnp.dot(a_ref[...].astype(BF16), w_ref[...], preferred_element_type=F32)
    o_ref[...] = acc


def _matmul_res(parts, w, res):
    m = res.shape[0]
    n = w.shape[1]
    k = parts[0].shape[1]
    assert all(a.shape == (m, k) for a in parts) and w.shape[0] == k * len(parts)
    tm = _pick_tile(m, 512)
    nparts = len(parts)
    return pl.pallas_call(
        functools.partial(_matmul_res_kernel, nparts=nparts),
        grid=(m // tm,),
        in_specs=([pl.BlockSpec((tm, k), lambda i: (i, 0)) for _ in parts]
                  + [pl.BlockSpec((k, n), functools.partial(lambda i, j: (j, 0), j=j)) for j in range(nparts)]
                  + [pl.BlockSpec((tm, n), lambda i: (i, 0))]),
        out_specs=pl.BlockSpec((tm, n), lambda i: (i, 0)),
        out_shape=jax.ShapeDtypeStruct((m, n), F32),
        compiler_params=_cparams(("parallel",)),
        name="matmul_res",
    )(*parts, *([w] * nparts), res)


def _rope_full(x, cosf, sinf):
    return x * cosf + pltpu.roll(x, RET_DK // 2, 1) * sinf


def _ret_kernel(q_ref, k_ref, v_ref, g_ref, cos_ref, sin_ref, lg_ref, o_ref, so_ref, s_ref, *, cps):
    c = pl.program_id(2)

    @pl.when(c == 0)
    def _():
        s_ref[...] = jnp.zeros_like(s_ref)

    lg = lg_ref[0]
    ii = lax.broadcasted_iota(jnp.int32, (CHUNK, CHUNK), 0)
    jj = lax.broadcasted_iota(jnp.int32, (CHUNK, CHUNK), 1)
    seg = jnp.where(ii >= jj, jnp.exp((ii - jj).astype(F32) * lg), 0.0)
    ri = lax.broadcasted_iota(jnp.int32, (CHUNK, RET_DK), 0).astype(F32)
    qdec = jnp.exp((ri + 1.0) * lg)
    kdec = jnp.exp((CHUNK - 1.0 - ri) * lg)
    cdec = jnp.exp(CHUNK * lg)[:, 0:1]

    s = s_ref[...]
    for t in range(cps):
        rows = pl.ds(t * CHUNK, CHUNK)
        cosf = cos_ref[rows, :]
        sinf = sin_ref[rows, :]
        q = _rope_full(q_ref[0, rows, :], cosf, sinf)
        k = _rope_full(k_ref[0, rows, :], cosf, sinf) * (RET_DK ** -0.5)
        vb = v_ref[0, rows, :].astype(BF16)
        sc = lax.dot_general(q.astype(BF16), k.astype(BF16), NT_DIMS, preferred_element_type=F32) * seg
        y = jnp.dot(sc.astype(BF16), vb, preferred_element_type=F32)
        y = y + jnp.dot((q * qdec).astype(BF16), s.astype(BF16), preferred_element_type=F32)
        kend_t = jnp.transpose(k * kdec).astype(BF16)
        s = cdec * s + jnp.dot(kend_t, vb, preferred_element_type=F32)
        y = y * lax.rsqrt(jnp.mean(y * y, axis=-1, keepdims=True) + RMS_EPS)
        o_ref[0, rows, :] = (y * _silu(g_ref[0, rows, :])).astype(BF16)
    s_ref[...] = s

    @pl.when(c == pl.num_programs(2) - 1)
    def _():
        so_ref[0, 0] = s


def _retention_prompt(proj, cosf, sinf, lg_rows, *, cps=4):
    b, l, _ = proj.shape
    rows = cps * CHUNK
    qb = OFF_Q // RET_DK
    kb = OFF_K // RET_DK
    vb = OFF_V // RET_DV
    gb = OFF_G // RET_DV
    return pl.pallas_call(
        functools.partial(_ret_kernel, cps=cps),
        grid=(b, RET_HEADS, l // rows),
        in_specs=[
            pl.BlockSpec((1, rows, RET_DK), lambda i, h, c: (i, c, qb + h)),
            pl.BlockSpec((1, rows, RET_DK), lambda i, h, c: (i, c, kb + h)),
            pl.BlockSpec((1, rows, RET_DV), lambda i, h, c: (i, c, vb + h)),
            pl.BlockSpec((1, rows, RET_DV), lambda i, h, c: (i, c, gb + h)),
            pl.BlockSpec((rows, RET_DK), lambda i, h, c: (c, 0)),
            pl.BlockSpec((rows, RET_DK), lambda i, h, c: (c, 0)),
            pl.BlockSpec((1, 1, LANES), lambda i, h, c: (h, 0, 0)),
        ],
        out_specs=[
            pl.BlockSpec((1, rows, RET_DV), lambda i, h, c: (i, c, h)),
            pl.BlockSpec((1, 1, RET_DK, RET_DV), lambda i, h, c: (i, h, 0, 0)),
        ],
        out_shape=[
            jax.ShapeDtypeStruct((b, l, RET_HEADS * RET_DV), BF16),
            jax.ShapeDtypeStruct((b, RET_HEADS, RET_DK, RET_DV), F32),
        ],
        scratch_shapes=[pltpu.VMEM((RET_DK, RET_DV), F32)],
        compiler_params=_cparams(("parallel", "parallel", "arbitrary")),
        name="retention_prompt",
    )(proj, proj, proj, proj, cosf, sinf, lg_rows)


def _split3(x):
    hi = x.astype(BF16)
    r1 = x - hi.astype(F32)
    mid = r1.astype(BF16)
    lo = (r1 - mid.astype(F32)).astype(BF16)
    return hi, mid, lo


def _cumsum_rows(tril_bf, x):
    hi, mid, lo = _split3(x)
    out = jnp.dot(tril_bf, lo, preferred_element_type=F32)
    out = out + jnp.dot(tril_bf, mid, preferred_element_type=F32)
    return out + jnp.dot(tril_bf, hi, preferred_element_type=F32)


def _shift_rows(cur, tail, s):
    r = pltpu.roll(cur, s, 0)
    pt = pltpu.roll(tail, s, 0)
    row = lax.broadcasted_iota(jnp.int32, (SUBLANES, cur.shape[1]), 0)
    top = jnp.where(row < s, pt, r[0:SUBLANES])
    return jnp.concatenate([top, r[SUBLANES:]], axis=0)


def _softplus(x):
    return jnp.maximum(x, 0.0) + jnp.log1p(jnp.exp(-jnp.abs(x)))


def _ssd_kernel(xbc_ref, z0_ref, z1_ref, dt_ref, cw_ref, cb_ref, dtb_ref, a_ref, dsk_ref, nw_ref,
                o_ref, so_ref, co_ref, s_ref, tail_ref):
    c = pl.program_id(1)
    z_refs = (z0_ref, z1_ref)

    @pl.when(c == 0)
    def _():
        s_ref[...] = jnp.zeros_like(s_ref)
        tail_ref[...] = jnp.zeros_like(tail_ref)

    raw = xbc_ref[0]
    tail = tail_ref[...]
    acc = raw * cw_ref[CONV_W - 1:CONV_W, :] + cb_ref[...]
    for s in range(1, CONV_W):
        acc = acc + _shift_rows(raw, tail, s) * cw_ref[CONV_W - 1 - s:CONV_W - s, :]
    xbc = _silu(acc)
    tail_ref[...] = raw[CHUNK - SUBLANES:, :]

    @pl.when(c == pl.num_programs(1) - 1)
    def _():
        co_ref[0] = raw[CHUNK - (CONV_W - 1):, :]

    ii = lax.broadcasted_iota(jnp.int32, (CHUNK, CHUNK), 0)
    jj = lax.broadcasted_iota(jnp.int32, (CHUNK, CHUNK), 1)
    causal = ii >= jj
    tril_bf = jnp.where(causal, 1.0, 0.0).astype(BF16)

    dt = _softplus(dt_ref[0] + dtb_ref[...])
    la = dt * a_ref[...]
    cum = _cumsum_rows(tril_bf, la)
    cum_t = jnp.transpose(cum)
    cum_last = cum[CHUNK - 1:CHUNK, :]

    bc_off = SSM_D_INNER
    b_all = xbc[:, bc_off:bc_off + LANES]
    c_all = xbc[:, bc_off + LANES:bc_off + 2 * LANES]
    b_all_t = jnp.transpose(b_all)
    rep = SSM_HEADS // SSM_GROUPS
    gw = rep * SSM_HEAD_DIM
    for g in range(SSM_GROUPS):
        bg = b_all[:, g * SSM_STATE:(g + 1) * SSM_STATE]
        cg = c_all[:, g * SSM_STATE:(g + 1) * SSM_STATE]
        bg_t = b_all_t[g * SSM_STATE:(g + 1) * SSM_STATE, :]
        cb = lax.dot_general(cg.astype(BF16), bg.astype(BF16), NT_DIMS, preferred_element_type=F32)
        ys = []
        for hh in range(rep):
            h = g * rep + hh
            cc = cum[:, h:h + 1]
            cr = cum_t[h:h + 1, :]
            cl = cum_last[:, h:h + 1]
            seg = jnp.exp(jnp.where(causal, cc - cr, -jnp.inf))
            xh = xbc[:, h * SSM_HEAD_DIM:(h + 1) * SSM_HEAD_DIM]
            xdt = (xh * dt[:, h:h + 1]).astype(BF16)
            sh = s_ref[h]
            y = jnp.dot((cb * seg).astype(BF16), xdt, preferred_element_type=F32)
            y = y + jnp.dot((cg * jnp.exp(cc)).astype(BF16), sh.astype(BF16), preferred_element_type=F32)
            y = y + dsk_ref[:, h:h + 1] * xh
            kend_t = (bg_t * jnp.exp(cl - cr)).astype(BF16)
            s_ref[h] = jnp.exp(cl) * sh + jnp.dot(kend_t, xdt, preferred_element_type=F32)
            ys.append(y)
        yg = jnp.concatenate(ys, axis=1) * _silu(z_refs[g][0])
        yg = yg * lax.rsqrt(jnp.mean(yg * yg, axis=-1, keepdims=True) + RMS_EPS)
        o_ref[0, :, g * gw:(g + 1) * gw] = (yg * nw_ref[:, g * gw:(g + 1) * gw]).astype(BF16)

    @pl.when(c == pl.num_programs(1) - 1)
    def _():
        so_ref[0] = s_ref[...]


def _ssd_prompt(proj, conv_w, conv_b, dtb_row, a_row, dsk_row, norm_w):
    b, l, _ = proj.shape
    gw = SSM_D_INNER // SSM_GROUPS
    full = lambda i, c: (0, 0)
    return pl.pallas_call(
        _ssd_kernel,
        grid=(b, l // CHUNK),
        in_specs=[
            pl.BlockSpec((1, CHUNK, CONV_DIM), lambda i, c: (i, c, OFF_XBC // CONV_DIM)),
            pl.BlockSpec((1, CHUNK, gw), lambda i, c: (i, c, OFF_Z // gw)),
            pl.BlockSpec((1, CHUNK, gw), lambda i, c: (i, c, OFF_Z // gw + 1)),
            pl.BlockSpec((1, CHUNK, LANES), lambda i, c: (i, c, OFF_DT // LANES)),
            pl.BlockSpec((CONV_W, CONV_DIM), full),
            pl.BlockSpec((1, CONV_DIM), full),
            pl.BlockSpec((1, LANES), full),
            pl.BlockSpec((1, LANES), full),
            pl.BlockSpec((1, LANES), full),
            pl.BlockSpec((1, SSM_D_INNER), full),
        ],
        out_specs=[
            pl.BlockSpec((1, CHUNK, SSM_D_INNER), lambda i, c: (i, c, 0)),
            pl.BlockSpec((1, SSM_HEADS, SSM_STATE, SSM_HEAD_DIM), lambda i, c: (i, 0, 0, 0)),
            pl.BlockSpec((1, CONV_W - 1, CONV_DIM), lambda i, c: (i, 0, 0)),
        ],
        out_shape=[
            jax.ShapeDtypeStruct((b, l, SSM_D_INNER), BF16),
            jax.ShapeDtypeStruct((b, SSM_HEADS, SSM_STATE, SSM_HEAD_DIM), F32),
            jax.ShapeDtypeStruct((b, CONV_W - 1, CONV_DIM), F32),
        ],
        scratch_shapes=[
            pltpu.VMEM((SSM_HEADS, SSM_STATE, SSM_HEAD_DIM), F32),
            pltpu.VMEM((SUBLANES, CONV_DIM), F32),
        ],
        compiler_params=_cparams(("parallel", "arbitrary")),
        name="ssd_prompt",
    )(proj, proj, proj, proj, conv_w, conv_b, dtb_row, a_row, dsk_row, norm_w)


def _bcast_rows(x, n):
    return jnp.broadcast_to(x, (n, x.shape[1]))


def _column_matrix(row):
    return jnp.transpose(_bcast_rows(row, LANES))


def _hyb_decode_kernel(row_ref, sr_ref, ss_ref, cs_ref, cos_ref, sin_ref, lg_ref, cw_ref, cb_ref,
                       dtb_ref, a_ref, dsk_ref, nw_ref, o_ref, sro_ref, sso_ref, co_ref):
    row = row_ref[0]
    cosf = cos_ref[...]
    sinf = sin_ref[...]
    for h in range(RET_HEADS):
        q = _rope_full(_bcast_rows(row[:, OFF_Q + h * RET_DK:OFF_Q + (h + 1) * RET_DK], SUBLANES), cosf, sinf)
        k = _rope_full(_bcast_rows(row[:, OFF_K + h * RET_DK:OFF_K + (h + 1) * RET_DK], SUBLANES), cosf, sinf)
        k = k * (RET_DK ** -0.5)
        v = _bf16_round(row[:, OFF_V + h * RET_DV:OFF_V + (h + 1) * RET_DV])
        g = row[:, OFF_G + h * RET_DV:OFF_G + (h + 1) * RET_DV]
        gamma = jnp.exp(lg_ref[h:h + 1, :])
        qb = _bf16_round(q)
        kb = _bf16_round(k)
        kcol = _column_matrix(kb[0:1])
        s0 = sr_ref[0, h]
        sro_ref[0, h] = gamma[:, 0:1] * s0 + jnp.concatenate([kcol, kcol], axis=1) * v
        qcol = _column_matrix(_bf16_round(q * gamma)[0:1])
        y = jnp.sum(jnp.concatenate([qcol, qcol], axis=1) * _bf16_round(s0), axis=0, keepdims=True)
        score = jnp.sum(qb * kb, axis=-1, keepdims=True)[0:1]
        y = y + _bf16_round(score) * v
        y = y * lax.rsqrt(jnp.mean(y * y, axis=-1, keepdims=True) + RMS_EPS)
        o_ref[0, :, h * RET_DV:(h + 1) * RET_DV] = y * _silu(g)
    cs = cs_ref[0]
    raw = row[:, OFF_XBC:OFF_XBC + CONV_DIM]
    acc = raw * cw_ref[CONV_W - 1:CONV_W, :] + cb_ref[...]
    for w in range(CONV_W - 1):
        acc = acc + cs[w:w + 1, :] * cw_ref[w:w + 1, :]
    xbc = _silu(acc)
    co_ref[0, 0:CONV_W - 2, :] = cs[1:CONV_W - 1, :]
    co_ref[0, CONV_W - 2:CONV_W - 1, :] = raw
    dt = _softplus(row[:, OFF_DT:OFF_DT + LANES] + dtb_ref[...])
    la = dt * a_ref[...]
    dec = jnp.exp(la)
    b_all = _bf16_round(xbc[:, SSM_D_INNER:SSM_D_INNER + LANES])
    c_all = xbc[:, SSM_D_INNER + LANES:SSM_D_INNER + 2 * LANES]
    bcol = _column_matrix(b_all)
    ccol = _column_matrix(_bf16_round(c_all))
    rep = SSM_HEADS // SSM_GROUPS
    ys = []
    for h in range(SSM_HEADS):
        g = h // rep
        s0 = ss_ref[0, h]
        xh = xbc[:, h * SSM_HEAD_DIM:(h + 1) * SSM_HEAD_DIM]
        xdt = _bf16_round(xh * dt[:, h:h + 1])
        dech = dec[:, h:h + 1]
        bg = b_all[:, g * SSM_STATE:(g + 1) * SSM_STATE]
        cg = c_all[:, g * SSM_STATE:(g + 1) * SSM_STATE]
        sso_ref[0, h] = dech * s0 + bcol[g * SSM_STATE:(g + 1) * SSM_STATE, 0:SSM_HEAD_DIM] * xdt
        y = dech * jnp.sum(ccol[g * SSM_STATE:(g + 1) * SSM_STATE, 0:SSM_HEAD_DIM] * _bf16_round(s0),
                           axis=0, keepdims=True)
        score = jnp.sum(_bf16_round(cg) * bg, axis=-1, keepdims=True)
        ys.append(y + _bf16_round(score) * xdt + dsk_ref[:, h:h + 1] * xh)
    gw = rep * SSM_HEAD_DIM
    for g in range(SSM_GROUPS):
        yg = jnp.concatenate(ys[g * rep:(g + 1) * rep], axis=1)
        yg = yg * _silu(row[:, OFF_Z + g * gw:OFF_Z + (g + 1) * gw])
        yg = yg * lax.rsqrt(jnp.mean(yg * yg, axis=-1, keepdims=True) + RMS_EPS)
        lo = RET_HEADS * RET_DV + g * gw
        o_ref[0, :, lo:lo + gw] = yg * nw_ref[:, g * gw:(g + 1) * gw]


def _hybrid_decode(proj_s, state_ret, state_ssm, state_conv, cos_row, sin_row, lg_rows, conv_w, conv_b,
                   dtb_row, a_row, dsk_row, norm_w):
    nb = proj_s.shape[0]
    full = lambda i: (0, 0)
    return pl.pallas_call(
        _hyb_decode_kernel,
        grid=(nb,),
        in_specs=[
            pl.BlockSpec((1, 1, HYB_IN_PAD), lambda i: (i, 0, 0)),
            pl.BlockSpec((1, RET_HEADS, RET_DK, RET_DV), lambda i: (i, 0, 0, 0)),
            pl.BlockSpec((1, SSM_HEADS, SSM_STATE, SSM_HEAD_DIM), lambda i: (i, 0, 0, 0)),
            pl.BlockSpec((1, CONV_W - 1, CONV_DIM), lambda i: (i, 0, 0)),
            pl.BlockSpec((1, RET_DK), full),
            pl.BlockSpec((1, RET_DK), full),
            pl.BlockSpec((RET_HEADS, LANES), full),
            pl.BlockSpec((CONV_W, CONV_DIM), full),
            pl.BlockSpec((1, CONV_DIM), full),
            pl.BlockSpec((1, LANES), full),
            pl.BlockSpec((1, LANES), full),
            pl.BlockSpec((1, LANES), full),
            pl.BlockSpec((1, SSM_D_INNER), full),
        ],
        out_specs=[
            pl.BlockSpec((1, 1, HYB_MIX), lambda i: (i, 0, 0)),
            pl.BlockSpec((1, RET_HEADS, RET_DK, RET_DV), lambda i: (i, 0, 0, 0)),
            pl.BlockSpec((1, SSM_HEADS, SSM_STATE, SSM_HEAD_DIM), lambda i: (i, 0, 0, 0)),
            pl.BlockSpec((1, CONV_W - 1, CONV_DIM), lambda i: (i, 0, 0)),
        ],
        out_shape=[
            jax.ShapeDtypeStruct((nb, 1, HYB_MIX), F32),
            jax.ShapeDtypeStruct(state_ret.shape, F32),
            jax.ShapeDtypeStruct(state_ssm.shape, F32),
            jax.ShapeDtypeStruct(state_conv.shape, F32),
        ],
        compiler_params=_cparams(("parallel",)),
        name="hybrid_decode",
    )(proj_s.reshape(nb, 1, HYB_IN_PAD), state_ret, state_ssm, state_conv, cos_row, sin_row, lg_rows,
      conv_w, conv_b, dtb_row, a_row, dsk_row, norm_w)


def _rope_group(x, c, s1, s2):
    half = QK_ROPE // 2
    return x * c + pltpu.roll(x, LANES - half, 1) * s1 + pltpu.roll(x, half, 1) * s2


def _mla_in_kernel(x_ref, nw_ref, w_ref, qnw_ref, kvnw_ref, c_ref, s1_ref, s2_ref,
                   cq_ref, ckv_ref, kr_ref, krp_ref):
    h = _rms(x_ref[...], nw_ref[...]).astype(BF16)
    p = jnp.dot(h, w_ref[...], preferred_element_type=F32)
    cq_ref[...] = _rms(p[:, :Q_LORA], qnw_ref[...]).astype(BF16)
    ckv_ref[...] = _rms(p[:, Q_LORA:Q_LORA + KV_LORA], kvnw_ref[...])
    kr = _rope_group(p[:, Q_LORA + KV_LORA:], c_ref[...], s1_ref[...], s2_ref[...])
    kr_ref[...] = kr[:, :QK_ROPE]
    krp_ref[...] = kr.astype(BF16)


def _mla_in(x, nw, w, qnw, kvnw, tabs):
    m, d = x.shape
    tm = _pick_tile(m, 512)
    nt = tabs[0].shape[0] // tm
    full = lambda i: (0, 0)
    tab = pl.BlockSpec((tm, LANES), lambda i: (i % nt, 0))
    return pl.pallas_call(
        _mla_in_kernel,
        grid=(m // tm,),
        in_specs=[
            pl.BlockSpec((tm, d), lambda i: (i, 0)),
            pl.BlockSpec((1, d), full),
            pl.BlockSpec((d, MLA_IN_PAD), full),
            pl.BlockSpec((1, Q_LORA), full),
            pl.BlockSpec((1, KV_LORA), full),
            tab, tab, tab,
        ],
        out_specs=[
            pl.BlockSpec((tm, Q_LORA), lambda i: (i, 0)),
            pl.BlockSpec((tm, KV_LORA), lambda i: (i, 0)),
            pl.BlockSpec((tm, QK_ROPE), lambda i: (i, 0)),
            pl.BlockSpec((tm, LANES), lambda i: (i, 0)),
        ],
        out_shape=[
            jax.ShapeDtypeStruct((m, Q_LORA), BF16),
            jax.ShapeDtypeStruct((m, KV_LORA), F32),
            jax.ShapeDtypeStruct((m, QK_ROPE), F32),
            jax.ShapeDtypeStruct((m, LANES), BF16),
        ],
        compiler_params=_cparams(("parallel",)),
        name="mla_in",
    )(x, nw, w, qnw, kvnw, *tabs)


Q_HEAD_PAD = 2 * LANES
Q_TN = 2 * Q_HEAD_PAD


def _mla_q_kernel(cq_ref, w_ref, c_ref, s1_ref, s2_ref, o_ref):
    p = jnp.dot(cq_ref[...], w_ref[...], preferred_element_type=F32)
    for gi in range(Q_TN // LANES):
        x = p[:, gi * LANES:(gi + 1) * LANES]
        if gi % 2 == 1:
            x = _rope_group(x, c_ref[...], s1_ref[...], s2_ref[...])
        o_ref[:, gi * LANES:(gi + 1) * LANES] = (x * MLA_SCALE).astype(BF16)


def _mla_q(cq, wq, tabs):
    m, k = cq.shape
    n = wq.shape[1]
    tm = _pick_tile(m, 512)
    nt = tabs[0].shape[0] // tm
    tab = pl.BlockSpec((tm, LANES), lambda i, j: (i % nt, 0))
    return pl.pallas_call(
        _mla_q_kernel,
        grid=(m // tm, n // Q_TN),
        in_specs=[
            pl.BlockSpec((tm, k), lambda i, j: (i, 0)),
            pl.BlockSpec((k, Q_TN), lambda i, j: (0, j)),
            tab, tab, tab,
        ],
        out_specs=pl.BlockSpec((tm, Q_TN), lambda i, j: (i, j)),
        out_shape=jax.ShapeDtypeStruct((m, n), BF16),
        compiler_params=_cparams(("parallel", "arbitrary")),
        name="mla_q",
    )(cq, wq, *tabs)


def _mla_kv_kernel(c_ref, wk_ref, wvt_ref, k_ref, vt_ref):
    cb = c_ref[...].astype(BF16)
    k_ref[...] = jnp.dot(cb, wk_ref[...], preferred_element_type=F32).astype(BF16)
    vt_ref[0] = lax.dot_general(wvt_ref[...], cb, NT_DIMS, preferred_element_type=F32).astype(BF16)


def _mla_kv(ckv, w_uk, w_uv_t, b, l):
    m, k = ckv.shape
    n = w_uk.shape[1]
    tm = _pick_tile(l, 512)
    nt = l // tm
    return pl.pallas_call(
        _mla_kv_kernel,
        grid=(m // tm,),
        in_specs=[
            pl.BlockSpec((tm, k), lambda i: (i, 0)),
            pl.BlockSpec((k, n), lambda i: (0, 0)),
            pl.BlockSpec((n, k), lambda i: (0, 0)),
        ],
        out_specs=[
            pl.BlockSpec((tm, n), lambda i: (i, 0)),
            pl.BlockSpec((1, n, tm), lambda i: (i // nt, 0, i % nt)),
        ],
        out_shape=[
            jax.ShapeDtypeStruct((m, n), BF16),
            jax.ShapeDtypeStruct((b, n, l), BF16),
        ],
        compiler_params=_cparams(("parallel",)),
        name="mla_kv",
    )(ckv, w_uk, w_uv_t)


def _flash_kernel(qi_ref, ki_ref, q_ref, kn_ref, kr_ref, vt_ref, o_ref, m_ref, l_ref, acc_ref, *, t):
    step = pl.program_id(2)
    qi = qi_ref[step]
    ki = ki_ref[step]

    @pl.when(ki == 0)
    def _():
        m_ref[...] = jnp.full_like(m_ref, -jnp.inf)
        l_ref[...] = jnp.zeros_like(l_ref)
        acc_ref[...] = jnp.zeros_like(acc_ref)

    def update(masked):
        kcat = jnp.concatenate([kn_ref[0], kr_ref[0]], axis=1)
        for hh in range(2):
            rows = pl.ds(hh * V_DIM, V_DIM)
            q = q_ref[0, :, hh * Q_HEAD_PAD:(hh + 1) * Q_HEAD_PAD]
            st = lax.dot_general(kcat, q, NT_DIMS, preferred_element_type=F32)
            if masked:
                key = lax.broadcasted_iota(jnp.int32, (t, t), 0)
                qry = lax.broadcasted_iota(jnp.int32, (t, t), 1)
                st = jnp.where(key <= qry, st, -jnp.inf)
            m_prev = m_ref[hh]
            m_new = jnp.maximum(m_prev, jnp.max(st, axis=0, keepdims=True))
            alpha = jnp.exp(m_prev - m_new)
            p = jnp.exp(st - m_new)
            l_ref[hh] = alpha * l_ref[hh] + jnp.sum(p, axis=0, keepdims=True)
            m_ref[hh] = m_new
            pv = jnp.dot(vt_ref[0, rows, :], p.astype(BF16), preferred_element_type=F32)
            acc_ref[rows, :] = alpha * acc_ref[rows, :] + pv

    @pl.when(ki < qi)
    def _():
        update(False)

    @pl.when(ki == qi)
    def _():
        update(True)
        inv = jnp.concatenate([jnp.broadcast_to(1.0 / l_ref[hh], (V_DIM, t)) for hh in range(2)], axis=0)
        o_ref[0] = jnp.transpose(acc_ref[...] * inv).astype(BF16)


def _flash_prompt(qp, kn, krp, vt, *, t=512):
    b, l, _ = qp.shape
    n = l // t
    qi_tab = jnp.asarray([qi for qi in range(n) for _ in range(qi + 1)], jnp.int32)
    ki_tab = jnp.asarray([ki for qi in range(n) for ki in range(qi + 1)], jnp.int32)
    npairs = MLA_HEADS // 2
    grid_spec = pltpu.PrefetchScalarGridSpec(
        num_scalar_prefetch=2,
        grid=(b, npairs, int(qi_tab.shape[0])),
        in_specs=[
            pl.BlockSpec((1, t, Q_TN), lambda i, p, s, qt, kt: (i, qt[s], p)),
            pl.BlockSpec((1, t, LANES), lambda i, p, s, qt, kt: (i, kt[s], p)),
            pl.BlockSpec((1, t, LANES), lambda i, p, s, qt, kt: (i, kt[s], 0)),
            pl.BlockSpec((1, 2 * V_DIM, t), lambda i, p, s, qt, kt: (i, p, kt[s])),
        ],
        out_specs=pl.BlockSpec((1, t, 2 * V_DIM), lambda i, p, s, qt, kt: (i, qt[s], p)),
        scratch_shapes=[
            pltpu.VMEM((2, 1, t), F32),
            pltpu.VMEM((2, 1, t), F32),
            pltpu.VMEM((2 * V_DIM, t), F32),
        ],
    )
    return pl.pallas_call(
        functools.partial(_flash_kernel, t=t),
        grid_spec=grid_spec,
        out_shape=jax.ShapeDtypeStruct((b, l, MLA_HEADS * V_DIM), BF16),
        compiler_params=_cparams(("parallel", "parallel", "arbitrary")),
        name="mla_flash",
    )(qi_tab, ki_tab, qp, kn, krp, vt)


Q_CAT = KV_LORA + LANES


def _qlat_kernel(q_ref, w_ref, o_ref):
    for hh in range(2):
        qh = q_ref[:, hh * Q_HEAD_PAD:(hh + 1) * Q_HEAD_PAD]
        lat = jnp.dot(qh[:, :LANES], w_ref[...], preferred_element_type=F32)
        o_ref[hh] = jnp.concatenate([lat.astype(BF16), qh[:, LANES:]], axis=1)


def _qlat(qp_s, w_uk_t):
    nb = qp_s.shape[0]
    npairs = MLA_HEADS // 2
    return pl.pallas_call(
        _qlat_kernel,
        grid=(npairs,),
        in_specs=[
            pl.BlockSpec((nb, Q_TN), lambda p: (0, p)),
            pl.BlockSpec((LANES, KV_LORA), lambda p: (p, 0)),
        ],
        out_specs=pl.BlockSpec((2, nb, Q_CAT), lambda p: (p, 0, 0)),
        out_shape=jax.ShapeDtypeStruct((MLA_HEADS, nb, Q_CAT), BF16),
        compiler_params=_cparams(("parallel",)),
        name="mla_qlat",
    )(qp_s, w_uk_t)


def _olat_kernel(o_ref, w_ref, out_ref):
    nb = o_ref.shape[0]
    first = lax.broadcasted_iota(jnp.int32, (nb, LANES), 1) < V_DIM
    r0 = jnp.dot(o_ref[:, :KV_LORA], w_ref[...], preferred_element_type=F32)
    r1 = jnp.dot(o_ref[:, KV_LORA:], w_ref[...], preferred_element_type=F32)
    out_ref[...] = jnp.where(first, r0, r1).astype(BF16)


def _olat(o_lat, w_uv):
    nb = o_lat.shape[0]
    npairs = MLA_HEADS // 2
    return pl.pallas_call(
        _olat_kernel,
        grid=(npairs,),
        in_specs=[
            pl.BlockSpec((nb, 2 * KV_LORA), lambda p: (0, p)),
            pl.BlockSpec((KV_LORA, LANES), lambda p: (0, p)),
        ],
        out_specs=pl.BlockSpec((nb, LANES), lambda p: (0, p)),
        out_shape=jax.ShapeDtypeStruct((nb, MLA_HEADS * V_DIM), BF16),
        compiler_params=_cparams(("parallel",)),
        name="mla_olat",
    )(o_lat, w_uv)


PAGES_PER_CHUNK = 32


def _mla_decode_kernel(pt_ref, q_ref, cn_ref, kn_ref, ckv_hbm, krt_hbm, o_ref, ckbuf, krbuf, sem,
                       *, layer, nch):
    ppc = PAGES_PER_CHUNK
    b = pl.program_id(0)
    nb = pl.num_programs(0)

    def copies(bb, c, slot):
        out = []
        for i in range(ppc):
            pg = pt_ref[bb, c * ppc + i]
            out.append(pltpu.make_async_copy(ckv_hbm.at[layer, pg], ckbuf.at[slot, i], sem.at[slot]))
            out.append(pltpu.make_async_copy(krt_hbm.at[layer, pg], krbuf.at[slot, i], sem.at[slot]))
        return out

    def start(bb, c, slot):
        for cp in copies(bb, c, slot):
            cp.start()

    @pl.when(b == 0)
    def _():
        start(0, 0, 0)

    q = q_ref[0]
    q_lat = q[:, :KV_LORA]
    q_rope = q[:, KV_LORA:KV_LORA + QK_ROPE]
    rows = ppc * PAGE_SIZE
    m_prev = jnp.full((MLA_HEADS, 1), -jnp.inf, F32)
    l_prev = jnp.zeros((MLA_HEADS, 1), F32)
    acc = jnp.zeros((MLA_HEADS, KV_LORA), F32)
    for c in range(nch):
        slot = c % 2
        if c + 1 < nch:
            start(b, c + 1, 1 - slot)
        else:
            @pl.when(b + 1 < nb)
            def _():
                start(b + 1, 0, 1 - slot)
        for cp in copies(b, c, slot):
            cp.wait()
        ck = ckbuf[slot].reshape(rows, KV_LORA).astype(BF16)
        krt = jnp.concatenate([krbuf[slot, i] for i in range(ppc)], axis=1).astype(BF16)
        s = lax.dot_general(q_lat, ck, NT_DIMS, preferred_element_type=F32)
        s = s + jnp.dot(q_rope, krt, preferred_element_type=F32)
        m_new = jnp.maximum(m_prev, jnp.max(s, axis=-1, keepdims=True))
        alpha = jnp.exp(m_prev - m_new)
        p = jnp.exp(s - m_new)
        l_prev = alpha * l_prev + jnp.sum(p, axis=-1, keepdims=True)
        acc = alpha * acc + jnp.dot(p.astype(BF16), ck, preferred_element_type=F32)
        m_prev = m_new

    cn = cn_ref[0]
    knew = jnp.concatenate([cn.astype(BF16), kn_ref[0]], axis=1).astype(F32)
    s_new = jnp.sum(q.astype(F32) * knew, axis=-1, keepdims=True)
    m_new = jnp.maximum(m_prev, s_new)
    alpha = jnp.exp(m_prev - m_new)
    p = jnp.exp(s_new - m_new)
    l_new = alpha * l_prev + p
    acc = alpha * acc + _bf16_round(p) * _bf16_round(cn)
    o_ref[0] = (acc / l_new).astype(BF16)


def _mla_decode(page_table, qcat, ckv_new, krp_new, cache_ckv, cache_krope_t, layer):
    nb, npages = page_table.shape
    nch = npages // PAGES_PER_CHUNK
    assert npages % PAGES_PER_CHUNK == 0 and nch % 2 == 0
    grid_spec = pltpu.PrefetchScalarGridSpec(
        num_scalar_prefetch=1,
        grid=(nb,),
        in_specs=[
            pl.BlockSpec((1, MLA_HEADS, Q_CAT), lambda i, pt: (i, 0, 0)),
            pl.BlockSpec((1, 1, KV_LORA), lambda i, pt: (i, 0, 0)),
            pl.BlockSpec((1, 1, LANES), lambda i, pt: (i, 0, 0)),
            pl.BlockSpec(memory_space=pl.ANY),
            pl.BlockSpec(memory_space=pl.ANY),
        ],
        out_specs=pl.BlockSpec((1, MLA_HEADS, KV_LORA), lambda i, pt: (i, 0, 0)),
        scratch_shapes=[
            pltpu.VMEM((2, PAGES_PER_CHUNK, PAGE_SIZE, KV_LORA), F32),
            pltpu.VMEM((2, PAGES_PER_CHUNK, QK_ROPE, PAGE_SIZE), F32),
            pltpu.SemaphoreType.DMA((2,)),
        ],
    )
    return pl.pallas_call(
        functools.partial(_mla_decode_kernel, layer=layer, nch=nch),
        grid_spec=grid_spec,
        out_shape=jax.ShapeDtypeStruct((nb, MLA_HEADS, KV_LORA), BF16),
        compiler_params=_cparams(("arbitrary",)),
        name="mla_decode",
    )(page_table, qcat, ckv_new.reshape(nb, 1, KV_LORA), krp_new.reshape(nb, 1, LANES),
      cache_ckv, cache_krope_t)


def _rope_angles(pos, half):
    inv = ROPE_THETA ** (-jnp.arange(half, dtype=F32) / half)
    ang = pos.astype(F32)[:, None] * inv[None, :]
    return jnp.cos(ang), jnp.sin(ang)


def _ret_tables(pos):
    cos, sin = _rope_angles(pos, RET_DK // 2)
    return jnp.concatenate([cos, cos], axis=1), jnp.concatenate([-sin, sin], axis=1)


def _mla_tables(pos, rows):
    half = QK_ROPE // 2
    cos, sin = _rope_angles(pos, half)
    n = pos.shape[0]
    c = jnp.concatenate([cos, cos, jnp.ones((n, LANES - QK_ROPE), F32)], axis=1)
    s1 = jnp.concatenate([-sin, jnp.zeros((n, LANES - half), F32)], axis=1)
    s2 = jnp.concatenate([jnp.zeros((n, half), F32), sin, jnp.zeros((n, LANES - QK_ROPE), F32)], axis=1)
    return tuple(jnp.broadcast_to(t, (rows, LANES)) if n == 1 else t for t in (c, s1, s2))


def _pad_lanes(v):
    return jnp.pad(v.astype(F32), (0, LANES - v.shape[0])).reshape(1, LANES)


def _hyb_w_in_layout(w):
    d = w.shape[0]
    qk = 2 * RET_HEADS * RET_DK
    vg = 2 * RET_HEADS * RET_DV
    q_k = w[:, :qk]
    v_g = w[:, qk:qk + vg]
    z = w[:, qk + vg:qk + vg + SSM_D_INNER]
    xbc = w[:, qk + vg + SSM_D_INNER:qk + vg + SSM_D_INNER + CONV_DIM]
    dt = w[:, qk + vg + SSM_D_INNER + CONV_DIM:]
    pieces = [xbc, dt, jnp.zeros((d, OFF_Q - OFF_DT - SSM_HEADS), w.dtype), q_k,
              jnp.zeros((d, OFF_V - OFF_K - RET_HEADS * RET_DK), w.dtype), v_g, z]
    out = jnp.concatenate(pieces, axis=1)
    assert out.shape[1] == HYB_IN_PAD
    return out.astype(BF16)


def _mla_wq_layout(w_uq):
    k = w_uq.shape[0]
    w3 = w_uq.reshape(k, MLA_HEADS, QK_NOPE + QK_ROPE)
    nope, ropew = w3[:, :, :QK_NOPE], w3[:, :, QK_NOPE:]
    z = jnp.zeros_like(nope)
    even = (jnp.arange(MLA_HEADS) % 2 == 0)[None, :, None]
    first = jnp.where(even, jnp.concatenate([nope, z], -1), jnp.concatenate([z, nope], -1))
    second = jnp.concatenate([ropew, jnp.zeros((k, MLA_HEADS, LANES - QK_ROPE), w_uq.dtype)], -1)
    return jnp.concatenate([first, second], -1).reshape(k, MLA_HEADS * Q_HEAD_PAD).astype(BF16)


def kernel(x_prompt, x_sample, state_ret, state_ssm, state_conv, cache_ckv, cache_krope, page_table,
           norm_ffn1, ffn1_w_in, ffn1_w_out, norm_mix, norm_ffn2, ffn2_w_in, ffn2_w_out,
           hyb_w_in, hyb_w_out, hyb_conv_w, hyb_conv_b, hyb_dt_bias, hyb_a_log, hyb_d_skip,
           hyb_norm_w, mla_w_in, mla_q_norm_w, mla_kv_norm_w, mla_w_uq, mla_w_uk, mla_w_uv,
           mla_w_o, final_norm_w):
    bp, sp, d = x_prompt.shape
    bs, ss, _ = x_sample.shape
    assert ss == 1 and sp % CHUNK == 0
    depth = norm_ffn1.shape[0]
    mp = bp * sp
    xp = x_prompt.reshape(mp, d)
    xs = x_sample.reshape(bs, d)
    pos_p = jnp.arange(sp)
    pos_s = PAST_LEN + jnp.arange(1)
    fw = final_norm_w.reshape(1, d)

    ret_cos_p, ret_sin_p = _ret_tables(pos_p)
    ret_cos_s, ret_sin_s = _ret_tables(pos_s)
    mla_tabs_p = _mla_tables(pos_p, sp)
    mla_tabs_s = _mla_tables(pos_s, bs)
    log_gamma = jnp.log1p(-jnp.exp2(-5.0 - jnp.arange(RET_HEADS, dtype=F32)))
    lg_rows = jnp.broadcast_to(log_gamma[:, None], (RET_HEADS, LANES))

    outs = {k: [] for k in ("ret_p", "ret_s", "ssm_p", "ssm_s", "conv_p", "conv_s",
                            "ckv_p", "ckv_s", "kr_p", "kr_s")}
    for layer in range(depth):
        j = layer // 2
        last = layer == depth - 1
        w1i, w1o = ffn1_w_in[layer].astype(BF16), ffn1_w_out[layer].astype(BF16)
        w2i, w2o = ffn2_w_in[layer].astype(BF16), ffn2_w_out[layer].astype(BF16)
        n1 = norm_ffn1[layer].reshape(1, d)
        nm = norm_mix[layer].reshape(1, d)
        n2 = norm_ffn2[layer].reshape(1, d)
        xp = _ffn(xp, n1, w1i, w1o, fw)
        xs = _ffn(xs, n1, w1i, w1o, fw)
        if layer % 2 == 0:
            w_in = _hyb_w_in_layout(hyb_w_in[j])
            w_out = hyb_w_out[j].astype(BF16)
            conv_w = hyb_conv_w[j].astype(F32)
            conv_b = hyb_conv_b[j].reshape(1, CONV_DIM).astype(F32)
            dtb_row = _pad_lanes(hyb_dt_bias[j])
            a_row = _pad_lanes(-jnp.exp(hyb_a_log[j].astype(F32)))
            dsk_row = _pad_lanes(hyb_d_skip[j])
            gnw = hyb_norm_w[j].reshape(1, SSM_D_INNER).astype(F32)
            proj = _norm_matmul(xp, nm, w_in, tn=1408).reshape(bp, sp, HYB_IN_PAD)
            o_ret, r_p = _retention_prompt(proj, ret_cos_p, ret_sin_p, lg_rows.reshape(RET_HEADS, 1, LANES))
            o_ssd, s_p, c_p = _ssd_prompt(proj, conv_w, conv_b, dtb_row, a_row, dsk_row, gnw)
            xp = _matmul_res([o_ret.reshape(mp, -1), o_ssd.reshape(mp, -1)], w_out, xp)
            proj_s = _norm_matmul(xs, nm, w_in, tn=1408)
            mixed_s, r_s, s_s, c_s = _hybrid_decode(
                proj_s, state_ret[j], state_ssm[j], state_conv[j], ret_cos_s, ret_sin_s, lg_rows,
                conv_w, conv_b, dtb_row, a_row, dsk_row, gnw)
            xs = _matmul_res([mixed_s.reshape(bs, HYB_MIX)], w_out, xs)
            outs["ret_p"].append(r_p); outs["ret_s"].append(r_s)
            outs["ssm_p"].append(s_p); outs["ssm_s"].append(s_s)
            outs["conv_p"].append(c_p); outs["conv_s"].append(c_s)
        else:
            w_in = jnp.pad(mla_w_in[j], ((0, 0), (0, MLA_IN_PAD - MLA_IN))).astype(BF16)
            qnw = mla_q_norm_w[j].reshape(1, Q_LORA)
            kvnw = mla_kv_norm_w[j].reshape(1, KV_LORA)
            wq = _mla_wq_layout(mla_w_uq[j])
            w_uk2 = mla_w_uk[j].reshape(KV_LORA, MLA_HEADS * QK_NOPE)
            w_uv2 = mla_w_uv[j].reshape(KV_LORA, MLA_HEADS * V_DIM)
            krope_t = jnp.swapaxes(cache_krope, 2, 3)
            w_o = mla_w_o[j].astype(BF16)
            cq, ckv, kr, krp = _mla_in(xp, nm, w_in, qnw, kvnw, mla_tabs_p)
            qp = _mla_q(cq, wq, mla_tabs_p)
            kn, vt = _mla_kv(ckv, w_uk2.astype(BF16), w_uv2.T.astype(BF16), bp, sp)
            o = _flash_prompt(qp.reshape(bp, sp, -1), kn.reshape(bp, sp, -1), krp.reshape(bp, sp, LANES), vt)
            xp = _matmul_res([o.reshape(mp, MLA_HEADS * V_DIM)], w_o, xp)
            outs["ckv_p"].append(ckv.reshape(bp, sp, KV_LORA))
            outs["kr_p"].append(kr.reshape(bp, sp, QK_ROPE))
            cq_s, ckv_s, kr_s, krp_s = _mla_in(xs, nm, w_in, qnw, kvnw, mla_tabs_s)
            qp_s = _mla_q(cq_s, wq, mla_tabs_s)
            qcat = jnp.transpose(_qlat(qp_s, w_uk2.T.astype(BF16)), (1, 0, 2))
            o_lat = _mla_decode(page_table, qcat, ckv_s, krp_s, cache_ckv, krope_t, j)
            o_s = _olat(o_lat.reshape(bs, MLA_HEADS * KV_LORA), w_uv2.astype(BF16))
            xs = _matmul_res([o_s], w_o, xs)
            outs["ckv_s"].append(ckv_s.reshape(bs, 1, KV_LORA))
            outs["kr_s"].append(kr_s.reshape(bs, 1, QK_ROPE))
        xp = _ffn(xp, n2, w2i, w2o, fw, final_norm=last)
        xs = _ffn(xs, n2, w2i, w2o, fw, final_norm=last)
    if depth == 0:
        raise ValueError("depth must be positive")
    return (xp.reshape(bp, sp, d), xs.reshape(bs, 1, d),
            jnp.stack(outs["ret_p"]), jnp.stack(outs["ret_s"]),
            jnp.stack(outs["ssm_p"]), jnp.stack(outs["ssm_s"]),
            jnp.stack(outs["conv_p"]), jnp.stack(outs["conv_s"]),
            jnp.stack(outs["ckv_p"]), jnp.stack(outs["ckv_s"]),
            jnp.stack(outs["kr_p"]), jnp.stack(outs["kr_s"]))
```

```python
import functools
import math

import jax
import jax.numpy as jnp
from jax import lax
from jax.experimental import pallas as pl
from jax.experimental.pallas import tpu as pltpu

F32 = jnp.float32
BF16 = jnp.bfloat16

D_MODEL = 1024
D_FF = 2816
RMS_EPS = 1e-6
ROPE_THETA = 10000.0
CHUNK = 128
RET_HEADS = 4
RET_DK = 128
RET_DV = 256
SSM_HEADS = 16
SSM_HEAD_DIM = 64
SSM_D_INNER = 1024
SSM_STATE = 64
SSM_GROUPS = 2
CONV_W = 4
CONV_DIM = 1280
HYB_IN = 5392
HYB_IN_PAD = 5632
HYB_MIX = 2048
MLA_HEADS = 16
Q_LORA = 512
KV_LORA = 256
QK_NOPE = 64
QK_ROPE = 32
V_DIM = 64
MLA_IN = 800
MLA_IN_PAD = 896
MLA_SCALE = (QK_NOPE + QK_ROPE) ** -0.5
Q_SCALE = MLA_SCALE * math.log2(math.e)
PAST_LEN = 16384
PAGE_SIZE = 128

OFF_XBC = 0
OFF_DT = 1280
OFF_Q = 1408
OFF_K = 1920
OFF_V = 2560
OFF_G = 3584
OFF_Z = 4608

LANES = 128
SUBLANES = 8
VMEM_LIMIT_BYTES = 56 * 1024 * 1024

NT_DIMS = (((1,), (1,)), ((), ()))


def _cparams(sem):
    return pltpu.CompilerParams(dimension_semantics=sem, vmem_limit_bytes=VMEM_LIMIT_BYTES)


def _silu(x):
    return x / (1.0 + jnp.exp(-x))


def _rms(x, w):
    return x * lax.rsqrt(jnp.mean(x * x, axis=-1, keepdims=True) + RMS_EPS) * w


def _bf16_round(x):
    return x.astype(BF16).astype(F32)


def _pick_tile(m, pref):
    t = min(m, pref)
    while m % t:
        t //= 2
    return t


def _ffn_kernel(x_ref, nw_ref, wg_ref, wu_ref, wo_ref, fw_ref, o_ref, h_ref, acc_ref, *, final_norm):
    j = pl.program_id(1)

    @pl.when(j == 0)
    def _():
        h_ref[...] = _rms(x_ref[...], nw_ref[...]).astype(BF16)
        acc_ref[...] = jnp.zeros_like(acc_ref)

    h = h_ref[...]
    g = jnp.dot(h, wg_ref[...], preferred_element_type=F32)
    u = jnp.dot(h, wu_ref[...], preferred_element_type=F32)
    a = (_silu(g) * u).astype(BF16)
    acc_ref[...] += jnp.dot(a, wo_ref[...], preferred_element_type=F32)

    @pl.when(j == pl.num_programs(1) - 1)
    def _():
        y = x_ref[...] + 0.5 * acc_ref[...]
        if final_norm:
            y = _rms(y, fw_ref[...])
        o_ref[...] = y


def _ffn(x, nw, w_in, w_out, fw, *, final_norm=False, tf=1408):
    m, d = x.shape
    tm = _pick_tile(m, 512)
    nj = D_FF // tf
    return pl.pallas_call(
        functools.partial(_ffn_kernel, final_norm=final_norm),
        grid=(m // tm, nj),
        in_specs=[
            pl.BlockSpec((tm, d), lambda i, j: (i, 0)),
            pl.BlockSpec((1, d), lambda i, j: (0, 0)),
            pl.BlockSpec((d, tf), lambda i, j: (0, j)),
            pl.BlockSpec((d, tf), lambda i, j: (0, j + nj)),
            pl.BlockSpec((tf, d), lambda i, j: (j, 0)),
            pl.BlockSpec((1, d), lambda i, j: (0, 0)),
        ],
        out_specs=pl.BlockSpec((tm, d), lambda i, j: (i, 0)),
        out_shape=jax.ShapeDtypeStruct((m, d), F32),
        scratch_shapes=[pltpu.VMEM((tm, d), BF16), pltpu.VMEM((tm, d), F32)],
        compiler_params=_cparams(("parallel", "arbitrary")),
        name="ffn",
    )(x, nw, w_in, w_in, w_out, fw)


def _norm_matmul_kernel(x_ref, nw_ref, w_ref, o_ref, h_ref):
    @pl.when(pl.program_id(1) == 0)
    def _():
        h_ref[...] = _rms(x_ref[...], nw_ref[...]).astype(BF16)

    o_ref[...] = jnp.dot(h_ref[...], w_ref[...], preferred_element_type=F32)


def _norm_matmul(x, nw, w, *, tn):
    m, d = x.shape
    n = w.shape[1]
    tm = _pick_tile(m, 512)
    return pl.pallas_call(
        _norm_matmul_kernel,
        grid=(m // tm, n // tn),
        in_specs=[
            pl.BlockSpec((tm, d), lambda i, j: (i, 0)),
            pl.BlockSpec((1, d), lambda i, j: (0, 0)),
            pl.BlockSpec((d, tn), lambda i, j: (0, j)),
        ],
        out_specs=pl.BlockSpec((tm, tn), lambda i, j: (i, j)),
        out_shape=jax.ShapeDtypeStruct((m, n), F32),
        scratch_shapes=[pltpu.VMEM((tm, d), BF16)],
        compiler_params=_cparams(("parallel", "arbitrary")),
        name="norm_matmul",
    )(x, nw, w)


def _matmul_res_kernel(*refs, nparts):
    a_refs, w_refs, r_ref, o_ref = refs[:nparts], refs[nparts:2 * nparts], refs[2 * nparts], refs[2 * nparts + 1]
    acc = r_ref[...]
    for a_ref, w_ref in zip(a_refs, w_refs):
        acc = acc + jnp.dot(a_ref[...].astype(BF16), w_ref[...], preferred_element_type=F32)
    o_ref[...] = acc


def _matmul_res(parts, w, res):
    m = res.shape[0]
    n = w.shape[1]
    k = parts[0].shape[1]
    assert all(a.shape == (m, k) for a in parts) and w.shape[0] == k * len(parts)
    tm = _pick_tile(m, 512)
    nparts = len(parts)
    return pl.pallas_call(
        functools.partial(_matmul_res_kernel, nparts=nparts),
        grid=(m // tm,),
        in_specs=([pl.BlockSpec((tm, k), lambda i: (i, 0)) for _ in parts]
                  + [pl.BlockSpec((k, n), functools.partial(lambda i, j: (j, 0), j=j)) for j in range(nparts)]
                  + [pl.BlockSpec((tm, n), lambda i: (i, 0))]),
        out_specs=pl.BlockSpec((tm, n), lambda i: (i, 0)),
        out_shape=jax.ShapeDtypeStruct((m, n), F32),
        compiler_params=_cparams(("parallel",)),
        name="matmul_res",
    )(*parts, *([w] * nparts), res)


def _rope_full(x, cosf, sinf):
    return x * cosf + pltpu.roll(x, RET_DK // 2, 1) * sinf


def _ret_kernel(q_ref, k_ref, v_ref, g_ref, cos_ref, sin_ref, lg_ref, o_ref, so_ref, s_ref, *, cps):
    c = pl.program_id(2)

    @pl.when(c == 0)
    def _():
        s_ref[...] = jnp.zeros_like(s_ref)

    lg = lg_ref[0]
    ii = lax.broadcasted_iota(jnp.int32, (CHUNK, CHUNK), 0)
    jj = lax.broadcasted_iota(jnp.int32, (CHUNK, CHUNK), 1)
    seg = jnp.where(ii >= jj, jnp.exp((ii - jj).astype(F32) * lg), 0.0)
    ri = lax.broadcasted_iota(jnp.int32, (CHUNK, RET_DK), 0).astype(F32)
    qdec = jnp.exp((ri + 1.0) * lg)
    kdec = jnp.exp((CHUNK - 1.0 - ri) * lg)
    cdec = jnp.exp(CHUNK * lg)[:, 0:1]

    s = s_ref[...]
    for t in range(cps):
        rows = pl.ds(t * CHUNK, CHUNK)
        cosf = cos_ref[rows, :]
        sinf = sin_ref[rows, :]
        q = _rope_full(q_ref[0, rows, :], cosf, sinf)
        k = _rope_full(k_ref[0, rows, :], cosf, sinf) * (RET_DK ** -0.5)
        vb = v_ref[0, rows, :].astype(BF16)
        sc = lax.dot_general(q.astype(BF16), k.astype(BF16), NT_DIMS, preferred_element_type=F32) * seg
        y = jnp.dot(sc.astype(BF16), vb, preferred_element_type=F32)
        y = y + jnp.dot((q * qdec).astype(BF16), s.astype(BF16), preferred_element_type=F32)
        kend_t = jnp.transpose(k * kdec).astype(BF16)
        s = cdec * s + jnp.dot(kend_t, vb, preferred_element_type=F32)
        y = y * lax.rsqrt(jnp.mean(y * y, axis=-1, keepdims=True) + RMS_EPS)
        o_ref[0, rows, :] = (y * _silu(g_ref[0, rows, :])).astype(BF16)
    s_ref[...] = s

    @pl.when(c == pl.num_programs(2) - 1)
    def _():
        so_ref[0, 0] = s


def _retention_prompt(proj, cosf, sinf, lg_rows, *, cps=4):
    b, l, _ = proj.shape
    rows = cps * CHUNK
    qb = OFF_Q // RET_DK
    kb = OFF_K // RET_DK
    vb = OFF_V // RET_DV
    gb = OFF_G // RET_DV
    return pl.pallas_call(
        functools.partial(_ret_kernel, cps=cps),
        grid=(b, RET_HEADS, l // rows),
        in_specs=[
            pl.BlockSpec((1, rows, RET_DK), lambda i, h, c: (i, c, qb + h)),
            pl.BlockSpec((1, rows, RET_DK), lambda i, h, c: (i, c, kb + h)),
            pl.BlockSpec((1, rows, RET_DV), lambda i, h, c: (i, c, vb + h)),
            pl.BlockSpec((1, rows, RET_DV), lambda i, h, c: (i, c, gb + h)),
            pl.BlockSpec((rows, RET_DK), lambda i, h, c: (c, 0)),
            pl.BlockSpec((rows, RET_DK), lambda i, h, c: (c, 0)),
            pl.BlockSpec((1, 1, LANES), lambda i, h, c: (h, 0, 0)),
        ],
        out_specs=[
            pl.BlockSpec((1, rows, RET_DV), lambda i, h, c: (i, c, h)),
            pl.BlockSpec((1, 1, RET_DK, RET_DV), lambda i, h, c: (i, h, 0, 0)),
        ],
        out_shape=[
            jax.ShapeDtypeStruct((b, l, RET_HEADS * RET_DV), BF16),
            jax.ShapeDtypeStruct((b, RET_HEADS, RET_DK, RET_DV), F32),
        ],
        scratch_shapes=[pltpu.VMEM((RET_DK, RET_DV), F32)],
        compiler_params=_cparams(("parallel", "parallel", "arbitrary")),
        name="retention_prompt",
    )(proj, proj, proj, proj, cosf, sinf, lg_rows)


def _split3(x):
    hi = x.astype(BF16)
    r1 = x - hi.astype(F32)
    mid = r1.astype(BF16)
    lo = (r1 - mid.astype(F32)).astype(BF16)
    return hi, mid, lo


def _cumsum_rows(tril_bf, x):
    hi, mid, lo = _split3(x)
    out = jnp.dot(tril_bf, lo, preferred_element_type=F32)
    out = out + jnp.dot(tril_bf, mid, preferred_element_type=F32)
    return out + jnp.dot(tril_bf, hi, preferred_element_type=F32)


def _shift_rows(cur, tail, s):
    r = pltpu.roll(cur, s, 0)
    pt = pltpu.roll(tail, s, 0)
    row = lax.broadcasted_iota(jnp.int32, (SUBLANES, cur.shape[1]), 0)
    top = jnp.where(row < s, pt, r[0:SUBLANES])
    return jnp.concatenate([top, r[SUBLANES:]], axis=0)


def _softplus(x):
    return jnp.maximum(x, 0.0) + jnp.log1p(jnp.exp(-jnp.abs(x)))


def _ssd_kernel(xbc_ref, z0_ref, z1_ref, dt_ref, cw_ref, cb_ref, dtb_ref, a_ref, dsk_ref, nw_ref,
                o_ref, so_ref, co_ref, s_ref, tail_ref):
    c = pl.program_id(1)
    z_refs = (z0_ref, z1_ref)

    @pl.when(c == 0)
    def _():
        s_ref[...] = jnp.zeros_like(s_ref)
        tail_ref[...] = jnp.zeros_like(tail_ref)

    raw = xbc_ref[0]
    tail = tail_ref[...]
    acc = raw * cw_ref[CONV_W - 1:CONV_W, :] + cb_ref[...]
    for s in range(1, CONV_W):
        acc = acc + _shift_rows(raw, tail, s) * cw_ref[CONV_W - 1 - s:CONV_W - s, :]
    xbc = _silu(acc)
    tail_ref[...] = raw[CHUNK - SUBLANES:, :]

    @pl.when(c == pl.num_programs(1) - 1)
    def _():
        co_ref[0] = raw[CHUNK - (CONV_W - 1):, :]

    ii = lax.broadcasted_iota(jnp.int32, (CHUNK, CHUNK), 0)
    jj = lax.broadcasted_iota(jnp.int32, (CHUNK, CHUNK), 1)
    causal = ii >= jj
    tril_bf = jnp.where(causal, 1.0, 0.0).astype(BF16)

    dt = _softplus(dt_ref[0] + dtb_ref[...])
    la = dt * a_ref[...]
    cum = _cumsum_rows(tril_bf, la)
    cum_t = jnp.transpose(cum)
    cum_last = cum[CHUNK - 1:CHUNK, :]

    bc_off = SSM_D_INNER
    b_all = xbc[:, bc_off:bc_off + LANES]
    c_all = xbc[:, bc_off + LANES:bc_off + 2 * LANES]
    b_all_t = jnp.transpose(b_all)
    rep = SSM_HEADS // SSM_GROUPS
    gw = rep * SSM_HEAD_DIM
    for g in range(SSM_GROUPS):
        bg = b_all[:, g * SSM_STATE:(g + 1) * SSM_STATE]
        cg = c_all[:, g * SSM_STATE:(g + 1) * SSM_STATE]
        bg_t = b_all_t[g * SSM_STATE:(g + 1) * SSM_STATE, :]
        cb = lax.dot_general(cg.astype(BF16), bg.astype(BF16), NT_DIMS, preferred_element_type=F32)
        ys = []
        for hh in range(rep):
            h = g * rep + hh
            cc = cum[:, h:h + 1]
            cr = cum_t[h:h + 1, :]
            cl = cum_last[:, h:h + 1]
            seg = jnp.exp(jnp.where(causal, cc - cr, -jnp.inf))
            xh = xbc[:, h * SSM_HEAD_DIM:(h + 1) * SSM_HEAD_DIM]
            xdt = (xh * dt[:, h:h + 1]).astype(BF16)
            sh = s_ref[h]
            y = jnp.dot((cb * seg).astype(BF16), xdt, preferred_element_type=F32)
            y = y + jnp.dot((cg * jnp.exp(cc)).astype(BF16), sh.astype(BF16), preferred_element_type=F32)
            y = y + dsk_ref[:, h:h + 1] * xh
            kend_t = (bg_t * jnp.exp(cl - cr)).astype(BF16)
            s_ref[h] = jnp.exp(cl) * sh + jnp.dot(kend_t, xdt, preferred_element_type=F32)
            ys.append(y)
        yg = jnp.concatenate(ys, axis=1) * _silu(z_refs[g][0])
        yg = yg * lax.rsqrt(jnp.mean(yg * yg, axis=-1, keepdims=True) + RMS_EPS)
        o_ref[0, :, g * gw:(g + 1) * gw] = (yg * nw_ref[:, g * gw:(g + 1) * gw]).astype(BF16)

    @pl.when(c == pl.num_programs(1) - 1)
    def _():
        so_ref[0] = s_ref[...]


def _ssd_prompt(proj, conv_w, conv_b, dtb_row, a_row, dsk_row, norm_w):
    b, l, _ = proj.shape
    gw = SSM_D_INNER // SSM_GROUPS
    full = lambda i, c: (0, 0)
    return pl.pallas_call(
        _ssd_kernel,
        grid=(b, l // CHUNK),
        in_specs=[
            pl.BlockSpec((1, CHUNK, CONV_DIM), lambda i, c: (i, c, OFF_XBC // CONV_DIM)),
            pl.BlockSpec((1, CHUNK, gw), lambda i, c: (i, c, OFF_Z // gw)),
            pl.BlockSpec((1, CHUNK, gw), lambda i, c: (i, c, OFF_Z // gw + 1)),
            pl.BlockSpec((1, CHUNK, LANES), lambda i, c: (i, c, OFF_DT // LANES)),
            pl.BlockSpec((CONV_W, CONV_DIM), full),
            pl.BlockSpec((1, CONV_DIM), full),
            pl.BlockSpec((1, LANES), full),
            pl.BlockSpec((1, LANES), full),
            pl.BlockSpec((1, LANES), full),
            pl.BlockSpec((1, SSM_D_INNER), full),
        ],
        out_specs=[
            pl.BlockSpec((1, CHUNK, SSM_D_INNER), lambda i, c: (i, c, 0)),
            pl.BlockSpec((1, SSM_HEADS, SSM_STATE, SSM_HEAD_DIM), lambda i, c: (i, 0, 0, 0)),
            pl.BlockSpec((1, CONV_W - 1, CONV_DIM), lambda i, c: (i, 0, 0)),
        ],
        out_shape=[
            jax.ShapeDtypeStruct((b, l, SSM_D_INNER), BF16),
            jax.ShapeDtypeStruct((b, SSM_HEADS, SSM_STATE, SSM_HEAD_DIM), F32),
            jax.ShapeDtypeStruct((b, CONV_W - 1, CONV_DIM), F32),
        ],
        scratch_shapes=[
            pltpu.VMEM((SSM_HEADS, SSM_STATE, SSM_HEAD_DIM), F32),
            pltpu.VMEM((SUBLANES, CONV_DIM), F32),
        ],
        compiler_params=_cparams(("parallel", "arbitrary")),
        name="ssd_prompt",
    )(proj, proj, proj, proj, conv_w, conv_b, dtb_row, a_row, dsk_row, norm_w)


def _bcast_rows(x, n):
    return jnp.broadcast_to(x, (n, x.shape[1]))


def _column_matrix(row):
    return jnp.transpose(_bcast_rows(row, LANES))


def _hyb_decode_kernel(row_ref, sr_ref, ss_ref, cs_ref, cos_ref, sin_ref, lg_ref, cw_ref, cb_ref,
                       dtb_ref, a_ref, dsk_ref, nw_ref, o_ref, sro_ref, sso_ref, co_ref):
    row = row_ref[0]
    cosf = cos_ref[...]
    sinf = sin_ref[...]
    for h in range(RET_HEADS):
        q = _rope_full(_bcast_rows(row[:, OFF_Q + h * RET_DK:OFF_Q + (h + 1) * RET_DK], SUBLANES), cosf, sinf)
        k = _rope_full(_bcast_rows(row[:, OFF_K + h * RET_DK:OFF_K + (h + 1) * RET_DK], SUBLANES), cosf, sinf)
        k = k * (RET_DK ** -0.5)
        v = _bf16_round(row[:, OFF_V + h * RET_DV:OFF_V + (h + 1) * RET_DV])
        g = row[:, OFF_G + h * RET_DV:OFF_G + (h + 1) * RET_DV]
        gamma = jnp.exp(lg_ref[h:h + 1, :])
        qb = _bf16_round(q)
        kb = _bf16_round(k)
        kcol = _column_matrix(kb[0:1])
        s0 = sr_ref[0, h]
        sro_ref[0, h] = gamma[:, 0:1] * s0 + jnp.concatenate([kcol, kcol], axis=1) * v
        qcol = _column_matrix(_bf16_round(q * gamma)[0:1])
        y = jnp.sum(jnp.concatenate([qcol, qcol], axis=1) * _bf16_round(s0), axis=0, keepdims=True)
        score = jnp.sum(qb * kb, axis=-1, keepdims=True)[0:1]
        y = y + _bf16_round(score) * v
        y = y * lax.rsqrt(jnp.mean(y * y, axis=-1, keepdims=True) + RMS_EPS)
        o_ref[0, :, h * RET_DV:(h + 1) * RET_DV] = y * _silu(g)
    cs = cs_ref[0]
    raw = row[:, OFF_XBC:OFF_XBC + CONV_DIM]
    acc = raw * cw_ref[CONV_W - 1:CONV_W, :] + cb_ref[...]
    for w in range(CONV_W - 1):
        acc = acc + cs[w:w + 1, :] * cw_ref[w:w + 1, :]
    xbc = _silu(acc)
    co_ref[0, 0:CONV_W - 2, :] = cs[1:CONV_W - 1, :]
    co_ref[0, CONV_W - 2:CONV_W - 1, :] = raw
    dt = _softplus(row[:, OFF_DT:OFF_DT + LANES] + dtb_ref[...])
    la = dt * a_ref[...]
    dec = jnp.exp(la)
    b_all = _bf16_round(xbc[:, SSM_D_INNER:SSM_D_INNER + LANES])
    c_all = xbc[:, SSM_D_INNER + LANES:SSM_D_INNER + 2 * LANES]
    bcol = _column_matrix(b_all)
    ccol = _column_matrix(_bf16_round(c_all))
    rep = SSM_HEADS // SSM_GROUPS
    ys = []
    for h in range(SSM_HEADS):
        g = h // rep
        s0 = ss_ref[0, h]
        xh = xbc[:, h * SSM_HEAD_DIM:(h + 1) * SSM_HEAD_DIM]
        xdt = _bf16_round(xh * dt[:, h:h + 1])
        dech = dec[:, h:h + 1]
        bg = b_all[:, g * SSM_STATE:(g + 1) * SSM_STATE]
        cg = c_all[:, g * SSM_STATE:(g + 1) * SSM_STATE]
        sso_ref[0, h] = dech * s0 + bcol[g * SSM_STATE:(g + 1) * SSM_STATE, 0:SSM_HEAD_DIM] * xdt
        y = dech * jnp.sum(ccol[g * SSM_STATE:(g + 1) * SSM_STATE, 0:SSM_HEAD_DIM] * _bf16_round(s0),
                           axis=0, keepdims=True)
        score = jnp.sum(_bf16_round(cg) * bg, axis=-1, keepdims=True)
        ys.append(y + _bf16_round(score) * xdt + dsk_ref[:, h:h + 1] * xh)
    gw = rep * SSM_HEAD_DIM
    for g in range(SSM_GROUPS):
        yg = jnp.concatenate(ys[g * rep:(g + 1) * rep], axis=1)
        yg = yg * _silu(row[:, OFF_Z + g * gw:OFF_Z + (g + 1) * gw])
        yg = yg * lax.rsqrt(jnp.mean(yg * yg, axis=-1, keepdims=True) + RMS_EPS)
        lo = RET_HEADS * RET_DV + g * gw
        o_ref[0, :, lo:lo + gw] = yg * nw_ref[:, g * gw:(g + 1) * gw]


def _hybrid_decode(proj_s, state_ret, state_ssm, state_conv, cos_row, sin_row, lg_rows, conv_w, conv_b,
                   dtb_row, a_row, dsk_row, norm_w):
    nb = proj_s.shape[0]
    full = lambda i: (0, 0)
    return pl.pallas_call(
        _hyb_decode_kernel,
        grid=(nb,),
        in_specs=[
            pl.BlockSpec((1, 1, HYB_IN_PAD), lambda i: (i, 0, 0)),
            pl.BlockSpec((1, RET_HEADS, RET_DK, RET_DV), lambda i: (i, 0, 0, 0)),
            pl.BlockSpec((1, SSM_HEADS, SSM_STATE, SSM_HEAD_DIM), lambda i: (i, 0, 0, 0)),
            pl.BlockSpec((1, CONV_W - 1, CONV_DIM), lambda i: (i, 0, 0)),
            pl.BlockSpec((1, RET_DK), full),
            pl.BlockSpec((1, RET_DK), full),
            pl.BlockSpec((RET_HEADS, LANES), full),
            pl.BlockSpec((CONV_W, CONV_DIM), full),
            pl.BlockSpec((1, CONV_DIM), full),
            pl.BlockSpec((1, LANES), full),
            pl.BlockSpec((1, LANES), full),
            pl.BlockSpec((1, LANES), full),
            pl.BlockSpec((1, SSM_D_INNER), full),
        ],
        out_specs=[
            pl.BlockSpec((1, 1, HYB_MIX), lambda i: (i, 0, 0)),
            pl.BlockSpec((1, RET_HEADS, RET_DK, RET_DV), lambda i: (i, 0, 0, 0)),
            pl.BlockSpec((1, SSM_HEADS, SSM_STATE, SSM_HEAD_DIM), lambda i: (i, 0, 0, 0)),
            pl.BlockSpec((1, CONV_W - 1, CONV_DIM), lambda i: (i, 0, 0)),
        ],
        out_shape=[
            jax.ShapeDtypeStruct((nb, 1, HYB_MIX), F32),
            jax.ShapeDtypeStruct(state_ret.shape, F32),
            jax.ShapeDtypeStruct(state_ssm.shape, F32),
            jax.ShapeDtypeStruct(state_conv.shape, F32),
        ],
        compiler_params=_cparams(("parallel",)),
        name="hybrid_decode",
    )(proj_s.reshape(nb, 1, HYB_IN_PAD), state_ret, state_ssm, state_conv, cos_row, sin_row, lg_rows,
      conv_w, conv_b, dtb_row, a_row, dsk_row, norm_w)


def _rope_group(x, c, s1, s2):
    half = QK_ROPE // 2
    return x * c + pltpu.roll(x, LANES - half, 1) * s1 + pltpu.roll(x, half, 1) * s2


def _mla_in_kernel(x_ref, nw_ref, w_ref, qnw_ref, kvnw_ref, c_ref, s1_ref, s2_ref,
                   cq_ref, ckv_ref, kr_ref, krp_ref):
    h = _rms(x_ref[...], nw_ref[...]).astype(BF16)
    p = jnp.dot(h, w_ref[...], preferred_element_type=F32)
    cq_ref[...] = _rms(p[:, :Q_LORA], qnw_ref[...]).astype(BF16)
    ckv_ref[...] = _rms(p[:, Q_LORA:Q_LORA + KV_LORA], kvnw_ref[...])
    kr = _rope_group(p[:, Q_LORA + KV_LORA:], c_ref[...], s1_ref[...], s2_ref[...])
    kr_ref[...] = kr[:, :QK_ROPE]
    krp_ref[...] = kr.astype(BF16)


def _mla_in(x, nw, w, qnw, kvnw, tabs):
    m, d = x.shape
    tm = _pick_tile(m, 512)
    nt = tabs[0].shape[0] // tm
    full = lambda i: (0, 0)
    tab = pl.BlockSpec((tm, LANES), lambda i: (i % nt, 0))
    return pl.pallas_call(
        _mla_in_kernel,
        grid=(m // tm,),
        in_specs=[
            pl.BlockSpec((tm, d), lambda i: (i, 0)),
            pl.BlockSpec((1, d), full),
            pl.BlockSpec((d, MLA_IN_PAD), full),
            pl.BlockSpec((1, Q_LORA), full),
            pl.BlockSpec((1, KV_LORA), full),
            tab, tab, tab,
        ],
        out_specs=[
            pl.BlockSpec((tm, Q_LORA), lambda i: (i, 0)),
            pl.BlockSpec((tm, KV_LORA), lambda i: (i, 0)),
            pl.BlockSpec((tm, QK_ROPE), lambda i: (i, 0)),
            pl.BlockSpec((tm, LANES), lambda i: (i, 0)),
        ],
        out_shape=[
            jax.ShapeDtypeStruct((m, Q_LORA), BF16),
            jax.ShapeDtypeStruct((m, KV_LORA), F32),
            jax.ShapeDtypeStruct((m, QK_ROPE), F32),
            jax.ShapeDtypeStruct((m, LANES), BF16),
        ],
        compiler_params=_cparams(("parallel",)),
        name="mla_in",
    )(x, nw, w, qnw, kvnw, *tabs)


Q_HEAD_PAD = 2 * LANES
Q_TN = 2 * Q_HEAD_PAD


def _mla_q_kernel(cq_ref, w_ref, c_ref, s1_ref, s2_ref, o_ref):
    p = jnp.dot(cq_ref[...], w_ref[...], preferred_element_type=F32)
    for gi in range(Q_TN // LANES):
        x = p[:, gi * LANES:(gi + 1) * LANES]
        if gi % 2 == 1:
            x = _rope_group(x, c_ref[...], s1_ref[...], s2_ref[...])
        o_ref[:, gi * LANES:(gi + 1) * LANES] = (x * Q_SCALE).astype(BF16)


def _mla_q(cq, wq, tabs):
    m, k = cq.shape
    n = wq.shape[1]
    tm = _pick_tile(m, 512)
    nt = tabs[0].shape[0] // tm
    tab = pl.BlockSpec((tm, LANES), lambda i, j: (i % nt, 0))
    return pl.pallas_call(
        _mla_q_kernel,
        grid=(m // tm, n // Q_TN),
        in_specs=[
            pl.BlockSpec((tm, k), lambda i, j: (i, 0)),
            pl.BlockSpec((k, Q_TN), lambda i, j: (0, j)),
            tab, tab, tab,
        ],
        out_specs=pl.BlockSpec((tm, Q_TN), lambda i, j: (i, j)),
        out_shape=jax.ShapeDtypeStruct((m, n), BF16),
        compiler_params=_cparams(("parallel", "arbitrary")),
        name="mla_q",
    )(cq, wq, *tabs)


QT_PAIR = 2 * QK_NOPE + 2 * QK_ROPE


def _mla_qt_kernel(cq_ref, wt_ref, cos_ref, sin_ref, o_ref):
    p = lax.dot_general(wt_ref[...], cq_ref[...], NT_DIMS, preferred_element_type=F32)
    cos = cos_ref[...]
    sin = sin_ref[...]
    half = QK_ROPE // 2
    for pr in range(MLA_HEADS // 2):
        base = pr * QT_PAIR
        o_ref[0, base:base + 2 * QK_NOPE, :] = (p[base:base + 2 * QK_NOPE] * Q_SCALE).astype(BF16)
        for hh in range(2):
            r0 = base + 2 * QK_NOPE + hh * QK_ROPE
            x1 = p[r0:r0 + half]
            x2 = p[r0 + half:r0 + QK_ROPE]
            o_ref[0, r0:r0 + half, :] = ((x1 * cos - x2 * sin) * Q_SCALE).astype(BF16)
            o_ref[0, r0 + half:r0 + QK_ROPE, :] = ((x1 * sin + x2 * cos) * Q_SCALE).astype(BF16)


def _mla_qt(cq, wq_t, cos_t, sin_t, b, l):
    m, k = cq.shape
    n = wq_t.shape[0]
    tm = _pick_tile(l, 512)
    nt = l // tm
    half = QK_ROPE // 2
    return pl.pallas_call(
        _mla_qt_kernel,
        grid=(m // tm,),
        in_specs=[
            pl.BlockSpec((tm, k), lambda i: (i, 0)),
            pl.BlockSpec((n, k), lambda i: (0, 0)),
            pl.BlockSpec((half, tm), lambda i: (0, i % nt)),
            pl.BlockSpec((half, tm), lambda i: (0, i % nt)),
        ],
        out_specs=pl.BlockSpec((1, n, tm), lambda i: (i // nt, 0, i % nt)),
        out_shape=jax.ShapeDtypeStruct((b, n, l), BF16),
        compiler_params=_cparams(("parallel",)),
        name="mla_qt",
    )(cq, wq_t, cos_t, sin_t)


def _mla_kv_kernel(c_ref, wk_ref, wvt_ref, k_ref, vt_ref):
    cb = c_ref[...].astype(BF16)
    k_ref[...] = jnp.dot(cb, wk_ref[...], preferred_element_type=F32).astype(BF16)
    vt_ref[0] = lax.dot_general(wvt_ref[...], cb, NT_DIMS, preferred_element_type=F32).astype(BF16)


def _mla_kv(ckv, w_uk, w_uv_t, b, l):
    m, k = ckv.shape
    n = w_uk.shape[1]
    tm = _pick_tile(l, 512)
    nt = l // tm
    return pl.pallas_call(
        _mla_kv_kernel,
        grid=(m // tm,),
        in_specs=[
            pl.BlockSpec((tm, k), lambda i: (i, 0)),
            pl.BlockSpec((k, n), lambda i: (0, 0)),
            pl.BlockSpec((n, k), lambda i: (0, 0)),
        ],
        out_specs=[
            pl.BlockSpec((tm, n), lambda i: (i, 0)),
            pl.BlockSpec((1, n, tm), lambda i: (i // nt, 0, i % nt)),
        ],
        out_shape=[
            jax.ShapeDtypeStruct((m, n), BF16),
            jax.ShapeDtypeStruct((b, n, l), BF16),
        ],
        compiler_params=_cparams(("parallel",)),
        name="mla_kv",
    )(ckv, w_uk, w_uv_t)


def _flash_kernel(qi_ref, ki_ref, q_ref, kn_ref, kr_ref, vt_ref, o_ref, m_ref, l_ref, acc_ref, *, t):
    step = pl.program_id(2)
    qi = qi_ref[step]
    ki = ki_ref[step]

    @pl.when(ki == 0)
    def _():
        m_ref[...] = jnp.full_like(m_ref, -jnp.inf)
        l_ref[...] = jnp.zeros_like(l_ref)
        acc_ref[...] = jnp.zeros_like(acc_ref)

    def update(masked):
        kcat = jnp.concatenate([kn_ref[0], kr_ref[0]], axis=1)
        qp = q_ref[0]
        zn = jnp.zeros((QK_NOPE, t), BF16)
        zt = jnp.zeros((LANES - QK_ROPE, t), BF16)
        rope0 = 2 * QK_NOPE
        qts = [jnp.concatenate([qp[0:QK_NOPE], zn, qp[rope0:rope0 + QK_ROPE], zt], axis=0),
               jnp.concatenate([zn, qp[QK_NOPE:rope0], qp[rope0 + QK_ROPE:], zt], axis=0)]
        sts = [jnp.dot(kcat, qts[hh], preferred_element_type=F32) for hh in range(2)]
        for hh in range(2):
            rows = pl.ds(hh * V_DIM, V_DIM)
            st = sts[hh]
            if masked:
                key = lax.broadcasted_iota(jnp.int32, (t, t), 0)
                qry = lax.broadcasted_iota(jnp.int32, (t, t), 1)
                st = jnp.where(key <= qry, st, -jnp.inf)
            m_prev = m_ref[hh]
            m_new = jnp.maximum(m_prev, jnp.max(st, axis=0, keepdims=True))
            alpha = jnp.exp2(m_prev - m_new)
            p = jnp.exp2(st - m_new)
            l_ref[hh] = alpha * l_ref[hh] + jnp.sum(p, axis=0, keepdims=True)
            m_ref[hh] = m_new
            pv = jnp.dot(vt_ref[0, rows, :], p.astype(BF16), preferred_element_type=F32)
            acc_ref[rows, :] = alpha * acc_ref[rows, :] + pv

    @pl.when(ki < qi)
    def _():
        update(False)

    @pl.when(ki == qi)
    def _():
        update(True)
        inv = jnp.concatenate([jnp.broadcast_to(1.0 / l_ref[hh], (V_DIM, t)) for hh in range(2)], axis=0)
        o_ref[0] = jnp.transpose(acc_ref[...] * inv).astype(BF16)


def _flash_prompt(qt, kn, krp, vt, *, t=512):
    b, l, _ = kn.shape
    n = l // t
    qi_tab = jnp.asarray([qi for qi in range(n) for _ in range(qi + 1)], jnp.int32)
    ki_tab = jnp.asarray([ki for qi in range(n) for ki in range(qi + 1)], jnp.int32)
    npairs = MLA_HEADS // 2
    grid_spec = pltpu.PrefetchScalarGridSpec(
        num_scalar_prefetch=2,
        grid=(b, npairs, int(qi_tab.shape[0])),
        in_specs=[
            pl.BlockSpec((1, QT_PAIR, t), lambda i, p, s, qt, kt: (i, p, qt[s])),
            pl.BlockSpec((1, t, LANES), lambda i, p, s, qt, kt: (i, kt[s], p)),
            pl.BlockSpec((1, t, LANES), lambda i, p, s, qt, kt: (i, kt[s], 0)),
            pl.BlockSpec((1, 2 * V_DIM, t), lambda i, p, s, qt, kt: (i, p, kt[s])),
        ],
        out_specs=pl.BlockSpec((1, t, 2 * V_DIM), lambda i, p, s, qt, kt: (i, qt[s], p)),
        scratch_shapes=[
            pltpu.VMEM((2, 1, t), F32),
            pltpu.VMEM((2, 1, t), F32),
            pltpu.VMEM((2 * V_DIM, t), F32),
        ],
    )
    return pl.pallas_call(
        functools.partial(_flash_kernel, t=t),
        grid_spec=grid_spec,
        out_shape=jax.ShapeDtypeStruct((b, l, MLA_HEADS * V_DIM), BF16),
        compiler_params=_cparams(("parallel", "parallel", "arbitrary")),
        name="mla_flash",
    )(qi_tab, ki_tab, qt, kn, krp, vt)


Q_CAT = KV_LORA + LANES


def _qlat_kernel(q_ref, w_ref, o_ref):
    for hh in range(2):
        qh = q_ref[:, hh * Q_HEAD_PAD:(hh + 1) * Q_HEAD_PAD]
        lat = jnp.dot(qh[:, :LANES], w_ref[...], preferred_element_type=F32)
        o_ref[hh] = jnp.concatenate([lat.astype(BF16), qh[:, LANES:]], axis=1)


def _qlat(qp_s, w_uk_t):
    nb = qp_s.shape[0]
    npairs = MLA_HEADS // 2
    return pl.pallas_call(
        _qlat_kernel,
        grid=(npairs,),
        in_specs=[
            pl.BlockSpec((nb, Q_TN), lambda p: (0, p)),
            pl.BlockSpec((LANES, KV_LORA), lambda p: (p, 0)),
        ],
        out_specs=pl.BlockSpec((2, nb, Q_CAT), lambda p: (p, 0, 0)),
        out_shape=jax.ShapeDtypeStruct((MLA_HEADS, nb, Q_CAT), BF16),
        compiler_params=_cparams(("parallel",)),
        name="mla_qlat",
    )(qp_s, w_uk_t)


def _olat_kernel(o_ref, w_ref, out_ref):
    nb = o_ref.shape[0]
    first = lax.broadcasted_iota(jnp.int32, (nb, LANES), 1) < V_DIM
    r0 = jnp.dot(o_ref[:, :KV_LORA], w_ref[...], preferred_element_type=F32)
    r1 = jnp.dot(o_ref[:, KV_LORA:], w_ref[...], preferred_element_type=F32)
    out_ref[...] = jnp.where(first, r0, r1).astype(BF16)


def _olat(o_lat, w_uv):
    nb = o_lat.shape[0]
    npairs = MLA_HEADS // 2
    return pl.pallas_call(
        _olat_kernel,
        grid=(npairs,),
        in_specs=[
            pl.BlockSpec((nb, 2 * KV_LORA), lambda p: (0, p)),
            pl.BlockSpec((KV_LORA, LANES), lambda p: (0, p)),
        ],
        out_specs=pl.BlockSpec((nb, LANES), lambda p: (0, p)),
        out_shape=jax.ShapeDtypeStruct((nb, MLA_HEADS * V_DIM), BF16),
        compiler_params=_cparams(("parallel",)),
        name="mla_olat",
    )(o_lat, w_uv)


PAGES_PER_CHUNK = 32


def _mla_decode_kernel(pt_ref, q_ref, cn_ref, kn_ref, ckv_hbm, krt_hbm, o_ref, ckbuf, krbuf, sem,
                       *, layer, nch):
    ppc = PAGES_PER_CHUNK
    b = pl.program_id(0)
    nb = pl.num_programs(0)

    def copies(bb, c, slot):
        out = []
        for i in range(ppc):
            pg = pt_ref[bb, c * ppc + i]
            out.append(pltpu.make_async_copy(ckv_hbm.at[layer, pg], ckbuf.at[slot, i], sem.at[slot]))
            out.append(pltpu.make_async_copy(krt_hbm.at[layer, pg], krbuf.at[slot, i], sem.at[slot]))
        return out

    def start(bb, c, slot):
        for cp in copies(bb, c, slot):
            cp.start()

    @pl.when(b == 0)
    def _():
        start(0, 0, 0)

    q = q_ref[0]
    q_lat = q[:, :KV_LORA]
    q_rope = q[:, KV_LORA:KV_LORA + QK_ROPE]
    rows = ppc * PAGE_SIZE
    m_prev = jnp.full((MLA_HEADS, 1), -jnp.inf, F32)
    l_prev = jnp.zeros((MLA_HEADS, 1), F32)
    acc = jnp.zeros((MLA_HEADS, KV_LORA), F32)
    for c in range(nch):
        slot = c % 2
        if c + 1 < nch:
            start(b, c + 1, 1 - slot)
        else:
            @pl.when(b + 1 < nb)
            def _():
                start(b + 1, 0, 1 - slot)
        for cp in copies(b, c, slot):
            cp.wait()
        ck = ckbuf[slot].reshape(rows, KV_LORA).astype(BF16)
        krt = jnp.concatenate([krbuf[slot, i] for i in range(ppc)], axis=1).astype(BF16)
        s = lax.dot_general(q_lat, ck, NT_DIMS, preferred_element_type=F32)
        s = s + jnp.dot(q_rope, krt, preferred_element_type=F32)
        m_new = jnp.maximum(m_prev, jnp.max(s, axis=-1, keepdims=True))
        alpha = jnp.exp2(m_prev - m_new)
        p = jnp.exp2(s - m_new)
        l_prev = alpha * l_prev + jnp.sum(p, axis=-1, keepdims=True)
        acc = alpha * acc + jnp.dot(p.astype(BF16), ck, preferred_element_type=F32)
        m_prev = m_new

    cn = cn_ref[0]
    knew = jnp.concatenate([cn.astype(BF16), kn_ref[0]], axis=1).astype(F32)
    s_new = jnp.sum(q.astype(F32) * knew, axis=-1, keepdims=True)
    m_new = jnp.maximum(m_prev, s_new)
    alpha = jnp.exp2(m_prev - m_new)
    p = jnp.exp2(s_new - m_new)
    l_new = alpha * l_prev + p
    acc = alpha * acc + _bf16_round(p) * _bf16_round(cn)
    o_ref[0] = (acc / l_new).astype(BF16)


def _mla_decode(page_table, qcat, ckv_new, krp_new, cache_ckv, cache_krope_t, layer):
    nb, npages = page_table.shape
    nch = npages // PAGES_PER_CHUNK
    assert npages % PAGES_PER_CHUNK == 0 and nch % 2 == 0
    grid_spec = pltpu.PrefetchScalarGridSpec(
        num_scalar_prefetch=1,
        grid=(nb,),
        in_specs=[
            pl.BlockSpec((1, MLA_HEADS, Q_CAT), lambda i, pt: (i, 0, 0)),
            pl.BlockSpec((1, 1, KV_LORA), lambda i, pt: (i, 0, 0)),
            pl.BlockSpec((1, 1, LANES), lambda i, pt: (i, 0, 0)),
            pl.BlockSpec(memory_space=pl.ANY),
            pl.BlockSpec(memory_space=pl.ANY),
        ],
        out_specs=pl.BlockSpec((1, MLA_HEADS, KV_LORA), lambda i, pt: (i, 0, 0)),
        scratch_shapes=[
            pltpu.VMEM((2, PAGES_PER_CHUNK, PAGE_SIZE, KV_LORA), F32),
            pltpu.VMEM((2, PAGES_PER_CHUNK, QK_ROPE, PAGE_SIZE), F32),
            pltpu.SemaphoreType.DMA((2,)),
        ],
    )
    return pl.pallas_call(
        functools.partial(_mla_decode_kernel, layer=layer, nch=nch),
        grid_spec=grid_spec,
        out_shape=jax.ShapeDtypeStruct((nb, MLA_HEADS, KV_LORA), BF16),
        compiler_params=_cparams(("arbitrary",)),
        name="mla_decode",
    )(page_table, qcat, ckv_new.reshape(nb, 1, KV_LORA), krp_new.reshape(nb, 1, LANES),
      cache_ckv, cache_krope_t)


def _rope_angles(pos, half):
    inv = ROPE_THETA ** (-jnp.arange(half, dtype=F32) / half)
    ang = pos.astype(F32)[:, None] * inv[None, :]
    return jnp.cos(ang), jnp.sin(ang)


def _ret_tables(pos):
    cos, sin = _rope_angles(pos, RET_DK // 2)
    return jnp.concatenate([cos, cos], axis=1), jnp.concatenate([-sin, sin], axis=1)


def _mla_tables(pos, rows):
    half = QK_ROPE // 2
    cos, sin = _rope_angles(pos, half)
    n = pos.shape[0]
    c = jnp.concatenate([cos, cos, jnp.ones((n, LANES - QK_ROPE), F32)], axis=1)
    s1 = jnp.concatenate([-sin, jnp.zeros((n, LANES - half), F32)], axis=1)
    s2 = jnp.concatenate([jnp.zeros((n, half), F32), sin, jnp.zeros((n, LANES - QK_ROPE), F32)], axis=1)
    return tuple(jnp.broadcast_to(t, (rows, LANES)) if n == 1 else t for t in (c, s1, s2))


def _pad_lanes(v):
    return jnp.pad(v.astype(F32), (0, LANES - v.shape[0])).reshape(1, LANES)


def _hyb_w_in_layout(w):
    d = w.shape[0]
    qk = 2 * RET_HEADS * RET_DK
    vg = 2 * RET_HEADS * RET_DV
    q_k = w[:, :qk]
    v_g = w[:, qk:qk + vg]
    z = w[:, qk + vg:qk + vg + SSM_D_INNER]
    xbc = w[:, qk + vg + SSM_D_INNER:qk + vg + SSM_D_INNER + CONV_DIM]
    dt = w[:, qk + vg + SSM_D_INNER + CONV_DIM:]
    pieces = [xbc, dt, jnp.zeros((d, OFF_Q - OFF_DT - SSM_HEADS), w.dtype), q_k,
              jnp.zeros((d, OFF_V - OFF_K - RET_HEADS * RET_DK), w.dtype), v_g, z]
    out = jnp.concatenate(pieces, axis=1)
    assert out.shape[1] == HYB_IN_PAD
    return out.astype(BF16)


def _mla_wqt_layout(w_uq):
    k = w_uq.shape[0]
    w4 = w_uq.reshape(k, MLA_HEADS // 2, 2, QK_NOPE + QK_ROPE)
    nope = w4[..., :QK_NOPE].reshape(k, MLA_HEADS // 2, 2 * QK_NOPE)
    ropew = w4[..., QK_NOPE:].reshape(k, MLA_HEADS // 2, 2 * QK_ROPE)
    return jnp.concatenate([nope, ropew], -1).reshape(k, MLA_HEADS // 2 * QT_PAIR).T.astype(BF16)


def _mla_wq_layout(w_uq):
    k = w_uq.shape[0]
    w3 = w_uq.reshape(k, MLA_HEADS, QK_NOPE + QK_ROPE)
    nope, ropew = w3[:, :, :QK_NOPE], w3[:, :, QK_NOPE:]
    z = jnp.zeros_like(nope)
    even = (jnp.arange(MLA_HEADS) % 2 == 0)[None, :, None]
    first = jnp.where(even, jnp.concatenate([nope, z], -1), jnp.concatenate([z, nope], -1))
    second = jnp.concatenate([ropew, jnp.zeros((k, MLA_HEADS, LANES - QK_ROPE), w_uq.dtype)], -1)
    return jnp.concatenate([first, second], -1).reshape(k, MLA_HEADS * Q_HEAD_PAD).astype(BF16)


def kernel(x_prompt, x_sample, state_ret, state_ssm, state_conv, cache_ckv, cache_krope, page_table,
           norm_ffn1, ffn1_w_in, ffn1_w_out, norm_mix, norm_ffn2, ffn2_w_in, ffn2_w_out,
           hyb_w_in, hyb_w_out, hyb_conv_w, hyb_conv_b, hyb_dt_bias, hyb_a_log, hyb_d_skip,
           hyb_norm_w, mla_w_in, mla_q_norm_w, mla_kv_norm_w, mla_w_uq, mla_w_uk, mla_w_uv,
           mla_w_o, final_norm_w):
    bp, sp, d = x_prompt.shape
    bs, ss, _ = x_sample.shape
    assert ss == 1 and sp % CHUNK == 0
    depth = norm_ffn1.shape[0]
    mp = bp * sp
    xp = x_prompt.reshape(mp, d)
    xs = x_sample.reshape(bs, d)
    pos_p = jnp.arange(sp)
    pos_s = PAST_LEN + jnp.arange(1)
    fw = final_norm_w.reshape(1, d)

    ret_cos_p, ret_sin_p = _ret_tables(pos_p)
    ret_cos_s, ret_sin_s = _ret_tables(pos_s)
    mla_tabs_p = _mla_tables(pos_p, sp)
    mla_tabs_s = _mla_tables(pos_s, bs)
    mla_cos_t, mla_sin_t = (t.T for t in _rope_angles(pos_p, QK_ROPE // 2))
    log_gamma = jnp.log1p(-jnp.exp2(-5.0 - jnp.arange(RET_HEADS, dtype=F32)))
    lg_rows = jnp.broadcast_to(log_gamma[:, None], (RET_HEADS, LANES))

    outs = {k: [] for k in ("ret_p", "ret_s", "ssm_p", "ssm_s", "conv_p", "conv_s",
                            "ckv_p", "ckv_s", "kr_p", "kr_s")}
    for layer in range(depth):
        j = layer // 2
        last = layer == depth - 1
        w1i, w1o = ffn1_w_in[layer].astype(BF16), ffn1_w_out[layer].astype(BF16)
        w2i, w2o = ffn2_w_in[layer].astype(BF16), ffn2_w_out[layer].astype(BF16)
        n1 = norm_ffn1[layer].reshape(1, d)
        nm = norm_mix[layer].reshape(1, d)
        n2 = norm_ffn2[layer].reshape(1, d)
        xp = _ffn(xp, n1, w1i, w1o, fw)
        xs = _ffn(xs, n1, w1i, w1o, fw)
        if layer % 2 == 0:
            w_in = _hyb_w_in_layout(hyb_w_in[j])
            w_out = hyb_w_out[j].astype(BF16)
            conv_w = hyb_conv_w[j].astype(F32)
            conv_b = hyb_conv_b[j].reshape(1, CONV_DIM).astype(F32)
            dtb_row = _pad_lanes(hyb_dt_bias[j])
            a_row = _pad_lanes(-jnp.exp(hyb_a_log[j].astype(F32)))
            dsk_row = _pad_lanes(hyb_d_skip[j])
            gnw = hyb_norm_w[j].reshape(1, SSM_D_INNER).astype(F32)
            proj = _norm_matmul(xp, nm, w_in, tn=1408).reshape(bp, sp, HYB_IN_PAD)
            o_ret, r_p = _retention_prompt(proj, ret_cos_p, ret_sin_p, lg_rows.reshape(RET_HEADS, 1, LANES))
            o_ssd, s_p, c_p = _ssd_prompt(proj, conv_w, conv_b, dtb_row, a_row, dsk_row, gnw)
            xp = _matmul_res([o_ret.reshape(mp, -1), o_ssd.reshape(mp, -1)], w_out, xp)
            proj_s = _norm_matmul(xs, nm, w_in, tn=1408)
            mixed_s, r_s, s_s, c_s = _hybrid_decode(
                proj_s, state_ret[j], state_ssm[j], state_conv[j], ret_cos_s, ret_sin_s, lg_rows,
                conv_w, conv_b, dtb_row, a_row, dsk_row, gnw)
            xs = _matmul_res([mixed_s.reshape(bs, HYB_MIX)], w_out, xs)
            outs["ret_p"].append(r_p); outs["ret_s"].append(r_s)
            outs["ssm_p"].append(s_p); outs["ssm_s"].append(s_s)
            outs["conv_p"].append(c_p); outs["conv_s"].append(c_s)
        else:
            w_in = jnp.pad(mla_w_in[j], ((0, 0), (0, MLA_IN_PAD - MLA_IN))).astype(BF16)
            qnw = mla_q_norm_w[j].reshape(1, Q_LORA)
            kvnw = mla_kv_norm_w[j].reshape(1, KV_LORA)
            wq = _mla_wq_layout(mla_w_uq[j])
            w_uk2 = mla_w_uk[j].reshape(KV_LORA, MLA_HEADS * QK_NOPE)
            w_uv2 = mla_w_uv[j].reshape(KV_LORA, MLA_HEADS * V_DIM)
            krope_t = jnp.swapaxes(cache_krope, 2, 3)
            w_o = mla_w_o[j].astype(BF16)
            cq, ckv, kr, krp = _mla_in(xp, nm, w_in, qnw, kvnw, mla_tabs_p)
            qt = _mla_qt(cq, _mla_wqt_layout(mla_w_uq[j]), mla_cos_t, mla_sin_t, bp, sp)
            kn, vt = _mla_kv(ckv, w_uk2.astype(BF16), w_uv2.T.astype(BF16), bp, sp)
            o = _flash_prompt(qt, kn.reshape(bp, sp, -1), krp.reshape(bp, sp, LANES), vt)
            xp = _matmul_res([o.reshape(mp, MLA_HEADS * V_DIM)], w_o, xp)
            outs["ckv_p"].append(ckv.reshape(bp, sp, KV_LORA))
            outs["kr_p"].append(kr.reshape(bp, sp, QK_ROPE))
            cq_s, ckv_s, kr_s, krp_s = _mla_in(xs, nm, w_in, qnw, kvnw, mla_tabs_s)
            qp_s = _mla_q(cq_s, wq, mla_tabs_s)
            qcat = jnp.transpose(_qlat(qp_s, w_uk2.T.astype(BF16)), (1, 0, 2))
            o_lat = _mla_decode(page_table, qcat, ckv_s, krp_s, cache_ckv, krope_t, j)
            o_s = _olat(o_lat.reshape(bs, MLA_HEADS * KV_LORA), w_uv2.astype(BF16))
            xs = _matmul_res([o_s], w_o, xs)
            outs["ckv_s"].append(ckv_s.reshape(bs, 1, KV_LORA))
            outs["kr_s"].append(kr_s.reshape(bs, 1, QK_ROPE))
        xp = _ffn(xp, n2, w2i, w2o, fw, final_norm=last)
        xs = _ffn(xs, n2, w2i, w2o, fw, final_norm=last)
    if depth == 0:
        raise ValueError("depth must be positive")
    return (xp.reshape(bp, sp, d), xs.reshape(bs, 1, d),
            jnp.stack(outs["ret_p"]), jnp.stack(outs["ret_s"]),
            jnp.stack(outs["ssm_p"]), jnp.stack(outs["ssm_s"]),
            jnp.stack(outs["conv_p"]), jnp.stack(outs["conv_s"]),
            jnp.stack(outs["ckv_p"]), jnp.stack(outs["ckv_s"]),
            jnp.stack(outs["kr_p"]), jnp.stack(outs["kr_s"]))
```

```python
import functools
import math

import jax
import jax.numpy as jnp
from jax import lax
from jax.experimental import pallas as pl
from jax.experimental.pallas import tpu as pltpu

F32 = jnp.float32
BF16 = jnp.bfloat16

D_MODEL = 1024
D_FF = 2816
RMS_EPS = 1e-6
ROPE_THETA = 10000.0
CHUNK = 128
RET_HEADS = 4
RET_DK = 128
RET_DV = 256
SSM_HEADS = 16
SSM_HEAD_DIM = 64
SSM_D_INNER = 1024
SSM_STATE = 64
SSM_GROUPS = 2
CONV_W = 4
CONV_DIM = 1280
HYB_IN = 5392
HYB_IN_PAD = 5632
HYB_MIX = 2048
MLA_HEADS = 16
Q_LORA = 512
KV_LORA = 256
QK_NOPE = 64
QK_ROPE = 32
V_DIM = 64
MLA_IN = 800
MLA_IN_PAD = 896
MLA_SCALE = (QK_NOPE + QK_ROPE) ** -0.5
Q_SCALE = MLA_SCALE * math.log2(math.e)
PAST_LEN = 16384
PAGE_SIZE = 128

OFF_XBC = 0
OFF_DT = 1280
OFF_Q = 1408
OFF_K = 1920
OFF_V = 2560
OFF_G = 3584
OFF_Z = 4608

LANES = 128
SUBLANES = 8
VMEM_LIMIT_BYTES = 56 * 1024 * 1024

NT_DIMS = (((1,), (1,)), ((), ()))


def _cparams(sem):
    return pltpu.CompilerParams(dimension_semantics=sem, vmem_limit_bytes=VMEM_LIMIT_BYTES)


def _silu(x):
    return x * (0.5 * jnp.tanh(0.5 * x) + 0.5)


def _rms(x, w):
    return x * lax.rsqrt(jnp.mean(x * x, axis=-1, keepdims=True) + RMS_EPS) * w


def _bf16_round(x):
    return x.astype(BF16).astype(F32)


def _pick_tile(m, pref):
    t = min(m, pref)
    while m % t:
        t //= 2
    return t


def _ffn_kernel(x_ref, nw_ref, wg_ref, wu_ref, wo_ref, fw_ref, o_ref, h_ref, acc_ref, *, final_norm):
    j = pl.program_id(1)

    @pl.when(j == 0)
    def _():
        h_ref[...] = _rms(x_ref[...], nw_ref[...]).astype(BF16)
        acc_ref[...] = jnp.zeros_like(acc_ref)

    h = h_ref[...]
    g = jnp.dot(h, wg_ref[...], preferred_element_type=F32)
    u = jnp.dot(h, wu_ref[...], preferred_element_type=F32)
    a = (_silu(g) * u).astype(BF16)
    acc_ref[...] += jnp.dot(a, wo_ref[...], preferred_element_type=F32)

    @pl.when(j == pl.num_programs(1) - 1)
    def _():
        y = x_ref[...] + 0.5 * acc_ref[...]
        if final_norm:
            y = _rms(y, fw_ref[...])
        o_ref[...] = y


def _ffn(x, nw, w_in, w_out, fw, *, final_norm=False, tf=1408):
    m, d = x.shape
    tm = _pick_tile(m, 512)
    nj = D_FF // tf
    return pl.pallas_call(
        functools.partial(_ffn_kernel, final_norm=final_norm),
        grid=(m // tm, nj),
        in_specs=[
            pl.BlockSpec((tm, d), lambda i, j: (i, 0)),
            pl.BlockSpec((1, d), lambda i, j: (0, 0)),
            pl.BlockSpec((d, tf), lambda i, j: (0, j)),
            pl.BlockSpec((d, tf), lambda i, j: (0, j + nj)),
            pl.BlockSpec((tf, d), lambda i, j: (j, 0)),
            pl.BlockSpec((1, d), lambda i, j: (0, 0)),
        ],
        out_specs=pl.BlockSpec((tm, d), lambda i, j: (i, 0)),
        out_shape=jax.ShapeDtypeStruct((m, d), F32),
        scratch_shapes=[pltpu.VMEM((tm, d), BF16), pltpu.VMEM((tm, d), F32)],
        compiler_params=_cparams(("parallel", "arbitrary")),
        name="ffn",
    )(x, nw, w_in, w_in, w_out, fw)


def _norm_matmul_kernel(x_ref, nw_ref, w_ref, o_ref, h_ref):
    @pl.when(pl.program_id(1) == 0)
    def _():
        h_ref[...] = _rms(x_ref[...], nw_ref[...]).astype(BF16)

    o_ref[...] = jnp.dot(h_ref[...], w_ref[...], preferred_element_type=F32)


def _norm_matmul(x, nw, w, *, tn):
    m, d = x.shape
    n = w.shape[1]
    tm = _pick_tile(m, 1024)
    return pl.pallas_call(
        _norm_matmul_kernel,
        grid=(m // tm, n // tn),
        in_specs=[
            pl.BlockSpec((tm, d), lambda i, j: (i, 0)),
            pl.BlockSpec((1, d), lambda i, j: (0, 0)),
            pl.BlockSpec((d, tn), lambda i, j: (0, j)),
        ],
        out_specs=pl.BlockSpec((tm, tn), lambda i, j: (i, j)),
        out_shape=jax.ShapeDtypeStruct((m, n), F32),
        scratch_shapes=[pltpu.VMEM((tm, d), BF16)],
        compiler_params=_cparams(("parallel", "arbitrary")),
        name="norm_matmul",
    )(x, nw, w)


def _matmul_res_kernel(*refs, nparts):
    a_refs, w_refs, r_ref, o_ref = refs[:nparts], refs[nparts:2 * nparts], refs[2 * nparts], refs[2 * nparts + 1]
    acc = r_ref[...]
    for a_ref, w_ref in zip(a_refs, w_refs):
        acc = acc + jnp.dot(a_ref[...].astype(BF16), w_ref[...], preferred_element_type=F32)
    o_ref[...] = acc


def _matmul_res(parts, w, res):
    m = res.shape[0]
    n = w.shape[1]
    k = parts[0].shape[1]
    assert all(a.shape == (m, k) for a in parts) and w.shape[0] == k * len(parts)
    tm = _pick_tile(m, 512)
    nparts = len(parts)
    return pl.pallas_call(
        functools.partial(_matmul_res_kernel, nparts=nparts),
        grid=(m // tm,),
        in_specs=([pl.BlockSpec((tm, k), lambda i: (i, 0)) for _ in parts]
                  + [pl.BlockSpec((k, n), functools.partial(lambda i, j: (j, 0), j=j)) for j in range(nparts)]
                  + [pl.BlockSpec((tm, n), lambda i: (i, 0))]),
        out_specs=pl.BlockSpec((tm, n), lambda i: (i, 0)),
        out_shape=jax.ShapeDtypeStruct((m, n), F32),
        compiler_params=_cparams(("parallel",)),
        name="matmul_res",
    )(*parts, *([w] * nparts), res)


def _rope_full(x, cosf, sinf):
    return x * cosf + pltpu.roll(x, RET_DK // 2, 1) * sinf


def _ret_kernel(q_ref, k_ref, v_ref, g_ref, cos_ref, sin_ref, lg_ref, o_ref, so_ref, s_ref, *, cps):
    c = pl.program_id(2)

    @pl.when(c == 0)
    def _():
        s_ref[...] = jnp.zeros_like(s_ref)

    lg = lg_ref[0]
    ii = lax.broadcasted_iota(jnp.int32, (CHUNK, CHUNK), 0)
    jj = lax.broadcasted_iota(jnp.int32, (CHUNK, CHUNK), 1)
    seg = jnp.where(ii >= jj, jnp.exp((ii - jj).astype(F32) * lg), 0.0)
    ri = lax.broadcasted_iota(jnp.int32, (CHUNK, RET_DK), 0).astype(F32)
    qdec = jnp.exp((ri + 1.0) * lg)
    kdec = jnp.exp((CHUNK - 1.0 - ri) * lg)
    cdec = jnp.exp(CHUNK * lg)[:, 0:1]

    s = s_ref[...]
    for t in range(cps):
        rows = pl.ds(t * CHUNK, CHUNK)
        cosf = cos_ref[rows, :]
        sinf = sin_ref[rows, :]
        q = _rope_full(q_ref[0, rows, :], cosf, sinf)
        k = _rope_full(k_ref[0, rows, :], cosf, sinf) * (RET_DK ** -0.5)
        vb = v_ref[0, rows, :].astype(BF16)
        sc = lax.dot_general(q.astype(BF16), k.astype(BF16), NT_DIMS, preferred_element_type=F32) * seg
        y = jnp.dot(sc.astype(BF16), vb, preferred_element_type=F32)
        y = y + jnp.dot((q * qdec).astype(BF16), s.astype(BF16), preferred_element_type=F32)
        kend_t = jnp.transpose(k * kdec).astype(BF16)
        s = cdec * s + jnp.dot(kend_t, vb, preferred_element_type=F32)
        y = y * lax.rsqrt(jnp.mean(y * y, axis=-1, keepdims=True) + RMS_EPS)
        o_ref[0, rows, :] = (y * _silu(g_ref[0, rows, :])).astype(BF16)
    s_ref[...] = s

    @pl.when(c == pl.num_programs(2) - 1)
    def _():
        so_ref[0, 0] = s


def _retention_prompt(proj, cosf, sinf, lg_rows, *, cps=8):
    b, l, _ = proj.shape
    while l % (cps * CHUNK):
        cps //= 2
    rows = cps * CHUNK
    qb = OFF_Q // RET_DK
    kb = OFF_K // RET_DK
    vb = OFF_V // RET_DV
    gb = OFF_G // RET_DV
    return pl.pallas_call(
        functools.partial(_ret_kernel, cps=cps),
        grid=(b, RET_HEADS, l // rows),
        in_specs=[
            pl.BlockSpec((1, rows, RET_DK), lambda i, h, c: (i, c, qb + h)),
            pl.BlockSpec((1, rows, RET_DK), lambda i, h, c: (i, c, kb + h)),
            pl.BlockSpec((1, rows, RET_DV), lambda i, h, c: (i, c, vb + h)),
            pl.BlockSpec((1, rows, RET_DV), lambda i, h, c: (i, c, gb + h)),
            pl.BlockSpec((rows, RET_DK), lambda i, h, c: (c, 0)),
            pl.BlockSpec((rows, RET_DK), lambda i, h, c: (c, 0)),
            pl.BlockSpec((1, 1, LANES), lambda i, h, c: (h, 0, 0)),
        ],
        out_specs=[
            pl.BlockSpec((1, rows, RET_DV), lambda i, h, c: (i, c, h)),
            pl.BlockSpec((1, 1, RET_DK, RET_DV), lambda i, h, c: (i, h, 0, 0)),
        ],
        out_shape=[
            jax.ShapeDtypeStruct((b, l, RET_HEADS * RET_DV), BF16),
            jax.ShapeDtypeStruct((b, RET_HEADS, RET_DK, RET_DV), F32),
        ],
        scratch_shapes=[pltpu.VMEM((RET_DK, RET_DV), F32)],
        compiler_params=_cparams(("parallel", "parallel", "arbitrary")),
        name="retention_prompt",
    )(proj, proj, proj, proj, cosf, sinf, lg_rows)


def _split3(x):
    hi = x.astype(BF16)
    r1 = x - hi.astype(F32)
    mid = r1.astype(BF16)
    lo = (r1 - mid.astype(F32)).astype(BF16)
    return hi, mid, lo


def _cumsum_rows(tril_bf, x):
    hi, mid, lo = _split3(x)
    out = jnp.dot(tril_bf, lo, preferred_element_type=F32)
    out = out + jnp.dot(tril_bf, mid, preferred_element_type=F32)
    return out + jnp.dot(tril_bf, hi, preferred_element_type=F32)


def _shift_rows(cur, tail, s):
    r = pltpu.roll(cur, s, 0)
    pt = pltpu.roll(tail, s, 0)
    row = lax.broadcasted_iota(jnp.int32, (SUBLANES, cur.shape[1]), 0)
    top = jnp.where(row < s, pt, r[0:SUBLANES])
    return jnp.concatenate([top, r[SUBLANES:]], axis=0)


def _softplus(x):
    return jnp.maximum(x, 0.0) + jnp.log(1.0 + jnp.exp(-jnp.abs(x)))


def _expand(x, e_bf, parts=3):
    hi, mid, lo = _split3(x)
    out = jnp.dot(mid, e_bf, preferred_element_type=F32)
    if parts == 3:
        out = jnp.dot(lo, e_bf, preferred_element_type=F32) + out
    return out + jnp.dot(hi, e_bf, preferred_element_type=F32)


def _ssd_kernel(xbc_ref, z0_ref, z1_ref, dt_ref, cw_ref, cb_ref, dtb_ref, a_ref, dskx_ref, nw_ref,
                e64_ref, e128_ref, o_ref, so_ref, co_ref, s_ref, tail_ref):
    c = pl.program_id(1)
    z_refs = (z0_ref, z1_ref)

    @pl.when(c == 0)
    def _():
        s_ref[...] = jnp.zeros_like(s_ref)
        tail_ref[...] = jnp.zeros_like(tail_ref)

    raw = xbc_ref[0]
    tail = tail_ref[...]
    acc = raw * cw_ref[CONV_W - 1:CONV_W, :] + cb_ref[...]
    for s in range(1, CONV_W):
        acc = acc + _shift_rows(raw, tail, s) * cw_ref[CONV_W - 1 - s:CONV_W - s, :]
    xbc = _silu(acc)
    tail_ref[...] = raw[CHUNK - SUBLANES:, :]

    @pl.when(c == pl.num_programs(1) - 1)
    def _():
        co_ref[0] = raw[CHUNK - (CONV_W - 1):, :]

    ii = lax.broadcasted_iota(jnp.int32, (CHUNK, CHUNK), 0)
    jj = lax.broadcasted_iota(jnp.int32, (CHUNK, CHUNK), 1)
    causal = ii >= jj
    tril_bf = jnp.where(causal, 1.0, 0.0).astype(BF16)

    dt = _softplus(dt_ref[0] + dtb_ref[...])
    la = dt * a_ref[...]
    cum = _cumsum_rows(tril_bf, la)
    cum_t = jnp.transpose(cum)

    dt_x = _expand(dt, e64_ref[...], parts=2)
    cum_x = _expand(cum, e64_ref[...])
    cc_all = _expand(cum, e128_ref[...])
    x_all = xbc[:, :SSM_D_INNER]
    xdt_all = x_all * dt_x
    ecum_x = jnp.exp(cum_x)
    dec_x = ecum_x[CHUNK - 1:CHUNK, :]

    b_all = xbc[:, SSM_D_INNER:SSM_D_INNER + LANES]
    c_all = xbc[:, SSM_D_INNER + LANES:SSM_D_INNER + 2 * LANES]
    b_all_t = jnp.transpose(b_all)
    left = lax.broadcasted_iota(jnp.int32, (CHUNK, LANES), 1) < SSM_HEAD_DIM
    top = lax.broadcasted_iota(jnp.int32, (CHUNK, LANES), 0) < SSM_STATE
    blockdiag = left == top
    pairs_per_group = SSM_HEADS // SSM_GROUPS // 2
    gw = SSM_D_INNER // SSM_GROUPS
    for g in range(SSM_GROUPS):
        bg = b_all[:, g * SSM_STATE:(g + 1) * SSM_STATE]
        cg = c_all[:, g * SSM_STATE:(g + 1) * SSM_STATE]
        bg_t = b_all_t[g * SSM_STATE:(g + 1) * SSM_STATE, :]
        cb = lax.dot_general(cg.astype(BF16), bg.astype(BF16), NT_DIMS, preferred_element_type=F32)
        cg2 = jnp.concatenate([cg, cg], axis=1)
        ys = []
        for pp in range(pairs_per_group):
            pr = g * pairs_per_group + pp
            lanes = slice(pr * LANES, (pr + 1) * LANES)
            lhs, kend = [], []
            for h in (2 * pr, 2 * pr + 1):
                cr = cum_t[h:h + 1, :]
                seg = jnp.exp(jnp.where(causal, cc_all[:, h * LANES:(h + 1) * LANES] - cr, -jnp.inf))
                lhs.append((cb * seg).astype(BF16))
                kend.append(bg_t * jnp.exp(cr[:, CHUNK - 1:CHUNK] - cr))
            xdt_p = xdt_all[:, lanes]
            rhs = jnp.concatenate([jnp.where(left, xdt_p, 0.0), jnp.where(left, 0.0, xdt_p)], axis=0)
            y = jnp.dot(jnp.concatenate(lhs, axis=1), rhs.astype(BF16), preferred_element_type=F32)
            sp = s_ref[pr]
            y = y + jnp.dot((cg2 * ecum_x[:, lanes]).astype(BF16), sp.astype(BF16), preferred_element_type=F32)
            ys.append(y + dskx_ref[:, lanes] * x_all[:, lanes])
            upd = jnp.dot(jnp.concatenate(kend, axis=0).astype(BF16), xdt_p.astype(BF16),
                          preferred_element_type=F32)
            s_ref[pr] = sp * dec_x[:, lanes] + jnp.where(blockdiag, upd, 0.0)
        yg = jnp.concatenate(ys, axis=1) * _silu(z_refs[g][0])
        yg = yg * lax.rsqrt(jnp.mean(yg * yg, axis=-1, keepdims=True) + RMS_EPS)
        o_ref[0, :, g * gw:(g + 1) * gw] = (yg * nw_ref[:, g * gw:(g + 1) * gw]).astype(BF16)

    @pl.when(c == pl.num_programs(1) - 1)
    def _():
        for pr in range(SSM_HEADS // 2):
            sp = s_ref[pr]
            so_ref[0, 2 * pr] = sp[:SSM_STATE, :SSM_HEAD_DIM]
            so_ref[0, 2 * pr + 1] = sp[SSM_STATE:, SSM_HEAD_DIM:]


def _head_expansion(width):
    col = jnp.arange(SSM_HEADS * width) // width
    return (jnp.arange(LANES)[:, None] == col[None, :]).astype(BF16)


def _ssd_prompt(proj, conv_w, conv_b, dtb_row, a_row, dsk_x, norm_w):
    b, l, _ = proj.shape
    gw = SSM_D_INNER // SSM_GROUPS
    full = lambda i, c: (0, 0)
    return pl.pallas_call(
        _ssd_kernel,
        grid=(b, l // CHUNK),
        in_specs=[
            pl.BlockSpec((1, CHUNK, CONV_DIM), lambda i, c: (i, c, OFF_XBC // CONV_DIM)),
            pl.BlockSpec((1, CHUNK, gw), lambda i, c: (i, c, OFF_Z // gw)),
            pl.BlockSpec((1, CHUNK, gw), lambda i, c: (i, c, OFF_Z // gw + 1)),
            pl.BlockSpec((1, CHUNK, LANES), lambda i, c: (i, c, OFF_DT // LANES)),
            pl.BlockSpec((CONV_W, CONV_DIM), full),
            pl.BlockSpec((1, CONV_DIM), full),
            pl.BlockSpec((1, LANES), full),
            pl.BlockSpec((1, LANES), full),
            pl.BlockSpec((1, SSM_D_INNER), full),
            pl.BlockSpec((1, SSM_D_INNER), full),
            pl.BlockSpec((LANES, SSM_HEADS * SSM_HEAD_DIM), full),
            pl.BlockSpec((LANES, SSM_HEADS * LANES), full),
        ],
        out_specs=[
            pl.BlockSpec((1, CHUNK, SSM_D_INNER), lambda i, c: (i, c, 0)),
            pl.BlockSpec((1, SSM_HEADS, SSM_STATE, SSM_HEAD_DIM), lambda i, c: (i, 0, 0, 0)),
            pl.BlockSpec((1, CONV_W - 1, CONV_DIM), lambda i, c: (i, 0, 0)),
        ],
        out_shape=[
            jax.ShapeDtypeStruct((b, l, SSM_D_INNER), BF16),
            jax.ShapeDtypeStruct((b, SSM_HEADS, SSM_STATE, SSM_HEAD_DIM), F32),
            jax.ShapeDtypeStruct((b, CONV_W - 1, CONV_DIM), F32),
        ],
        scratch_shapes=[
            pltpu.VMEM((SSM_HEADS // 2, 2 * SSM_STATE, 2 * SSM_HEAD_DIM), F32),
            pltpu.VMEM((SUBLANES, CONV_DIM), F32),
        ],
        compiler_params=_cparams(("parallel", "arbitrary")),
        name="ssd_prompt",
    )(proj, proj, proj, proj, conv_w, conv_b, dtb_row, a_row, dsk_x, norm_w,
      _head_expansion(SSM_HEAD_DIM), _head_expansion(LANES))


def _bcast_rows(x, n):
    return jnp.broadcast_to(x, (n, x.shape[1]))


def _column_matrix(row):
    return jnp.transpose(_bcast_rows(row, LANES))


def _hyb_decode_kernel(row_ref, sr_ref, ss_ref, cs_ref, cos_ref, sin_ref, lg_ref, cw_ref, cb_ref,
                       dtb_ref, a_ref, dsk_ref, nw_ref, o_ref, sro_ref, sso_ref, co_ref):
    row = row_ref[0]
    cosf = cos_ref[...]
    sinf = sin_ref[...]
    for h in range(RET_HEADS):
        q = _rope_full(_bcast_rows(row[:, OFF_Q + h * RET_DK:OFF_Q + (h + 1) * RET_DK], SUBLANES), cosf, sinf)
        k = _rope_full(_bcast_rows(row[:, OFF_K + h * RET_DK:OFF_K + (h + 1) * RET_DK], SUBLANES), cosf, sinf)
        k = k * (RET_DK ** -0.5)
        v = _bf16_round(row[:, OFF_V + h * RET_DV:OFF_V + (h + 1) * RET_DV])
        g = row[:, OFF_G + h * RET_DV:OFF_G + (h + 1) * RET_DV]
        gamma = jnp.exp(lg_ref[h:h + 1, :])
        qb = _bf16_round(q)
        kb = _bf16_round(k)
        kcol = _column_matrix(kb[0:1])
        s0 = sr_ref[0, h]
        sro_ref[0, h] = gamma[:, 0:1] * s0 + jnp.concatenate([kcol, kcol], axis=1) * v
        qcol = _column_matrix(_bf16_round(q * gamma)[0:1])
        y = jnp.sum(jnp.concatenate([qcol, qcol], axis=1) * _bf16_round(s0), axis=0, keepdims=True)
        score = jnp.sum(qb * kb, axis=-1, keepdims=True)[0:1]
        y = y + _bf16_round(score) * v
        y = y * lax.rsqrt(jnp.mean(y * y, axis=-1, keepdims=True) + RMS_EPS)
        o_ref[0, :, h * RET_DV:(h + 1) * RET_DV] = y * _silu(g)
    cs = cs_ref[0]
    raw = row[:, OFF_XBC:OFF_XBC + CONV_DIM]
    acc = raw * cw_ref[CONV_W - 1:CONV_W, :] + cb_ref[...]
    for w in range(CONV_W - 1):
        acc = acc + cs[w:w + 1, :] * cw_ref[w:w + 1, :]
    xbc = _silu(acc)
    co_ref[0, 0:CONV_W - 2, :] = cs[1:CONV_W - 1, :]
    co_ref[0, CONV_W - 2:CONV_W - 1, :] = raw
    dt = _softplus(row[:, OFF_DT:OFF_DT + LANES] + dtb_ref[...])
    la = dt * a_ref[...]
    dec = jnp.exp(la)
    b_all = _bf16_round(xbc[:, SSM_D_INNER:SSM_D_INNER + LANES])
    c_all = xbc[:, SSM_D_INNER + LANES:SSM_D_INNER + 2 * LANES]
    bcol = _column_matrix(b_all)
    ccol = _column_matrix(_bf16_round(c_all))
    rep = SSM_HEADS // SSM_GROUPS
    ys = []
    for h in range(SSM_HEADS):
        g = h // rep
        s0 = ss_ref[0, h]
        xh = xbc[:, h * SSM_HEAD_DIM:(h + 1) * SSM_HEAD_DIM]
        xdt = _bf16_round(xh * dt[:, h:h + 1])
        dech = dec[:, h:h + 1]
        bg = b_all[:, g * SSM_STATE:(g + 1) * SSM_STATE]
        cg = c_all[:, g * SSM_STATE:(g + 1) * SSM_STATE]
        sso_ref[0, h] = dech * s0 + bcol[g * SSM_STATE:(g + 1) * SSM_STATE, 0:SSM_HEAD_DIM] * xdt
        y = dech * jnp.sum(ccol[g * SSM_STATE:(g + 1) * SSM_STATE, 0:SSM_HEAD_DIM] * _bf16_round(s0),
                           axis=0, keepdims=True)
        score = jnp.sum(_bf16_round(cg) * bg, axis=-1, keepdims=True)
        ys.append(y + _bf16_round(score) * xdt + dsk_ref[:, h:h + 1] * xh)
    gw = rep * SSM_HEAD_DIM
    for g in range(SSM_GROUPS):
        yg = jnp.concatenate(ys[g * rep:(g + 1) * rep], axis=1)
        yg = yg * _silu(row[:, OFF_Z + g * gw:OFF_Z + (g + 1) * gw])
        yg = yg * lax.rsqrt(jnp.mean(yg * yg, axis=-1, keepdims=True) + RMS_EPS)
        lo = RET_HEADS * RET_DV + g * gw
        o_ref[0, :, lo:lo + gw] = yg * nw_ref[:, g * gw:(g + 1) * gw]


def _hybrid_decode(proj_s, state_ret, state_ssm, state_conv, cos_row, sin_row, lg_rows, conv_w, conv_b,
                   dtb_row, a_row, dsk_row, norm_w):
    nb = proj_s.shape[0]
    full = lambda i: (0, 0)
    return pl.pallas_call(
        _hyb_decode_kernel,
        grid=(nb,),
        in_specs=[
            pl.BlockSpec((1, 1, HYB_IN_PAD), lambda i: (i, 0, 0)),
            pl.BlockSpec((1, RET_HEADS, RET_DK, RET_DV), lambda i: (i, 0, 0, 0)),
            pl.BlockSpec((1, SSM_HEADS, SSM_STATE, SSM_HEAD_DIM), lambda i: (i, 0, 0, 0)),
            pl.BlockSpec((1, CONV_W - 1, CONV_DIM), lambda i: (i, 0, 0)),
            pl.BlockSpec((1, RET_DK), full),
            pl.BlockSpec((1, RET_DK), full),
            pl.BlockSpec((RET_HEADS, LANES), full),
            pl.BlockSpec((CONV_W, CONV_DIM), full),
            pl.BlockSpec((1, CONV_DIM), full),
            pl.BlockSpec((1, LANES), full),
            pl.BlockSpec((1, LANES), full),
            pl.BlockSpec((1, LANES), full),
            pl.BlockSpec((1, SSM_D_INNER), full),
        ],
        out_specs=[
            pl.BlockSpec((1, 1, HYB_MIX), lambda i: (i, 0, 0)),
            pl.BlockSpec((1, RET_HEADS, RET_DK, RET_DV), lambda i: (i, 0, 0, 0)),
            pl.BlockSpec((1, SSM_HEADS, SSM_STATE, SSM_HEAD_DIM), lambda i: (i, 0, 0, 0)),
            pl.BlockSpec((1, CONV_W - 1, CONV_DIM), lambda i: (i, 0, 0)),
        ],
        out_shape=[
            jax.ShapeDtypeStruct((nb, 1, HYB_MIX), F32),
            jax.ShapeDtypeStruct(state_ret.shape, F32),
            jax.ShapeDtypeStruct(state_ssm.shape, F32),
            jax.ShapeDtypeStruct(state_conv.shape, F32),
        ],
        compiler_params=_cparams(("parallel",)),
        name="hybrid_decode",
    )(proj_s.reshape(nb, 1, HYB_IN_PAD), state_ret, state_ssm, state_conv, cos_row, sin_row, lg_rows,
      conv_w, conv_b, dtb_row, a_row, dsk_row, norm_w)


def _rope_group(x, c, s1, s2):
    half = QK_ROPE // 2
    return x * c + pltpu.roll(x, LANES - half, 1) * s1 + pltpu.roll(x, half, 1) * s2


def _mla_in_kernel(x_ref, nw_ref, w_ref, qnw_ref, kvnw_ref, c_ref, s1_ref, s2_ref,
                   cq_ref, ckv_ref, kr_ref, krp_ref):
    h = _rms(x_ref[...], nw_ref[...]).astype(BF16)
    p = jnp.dot(h, w_ref[...], preferred_element_type=F32)
    cq_ref[...] = _rms(p[:, :Q_LORA], qnw_ref[...]).astype(BF16)
    ckv_ref[...] = _rms(p[:, Q_LORA:Q_LORA + KV_LORA], kvnw_ref[...])
    kr = _rope_group(p[:, Q_LORA + KV_LORA:], c_ref[...], s1_ref[...], s2_ref[...])
    kr_ref[...] = kr[:, :QK_ROPE]
    krp_ref[...] = kr.astype(BF16)


def _mla_in(x, nw, w, qnw, kvnw, tabs):
    m, d = x.shape
    tm = _pick_tile(m, 512)
    nt = tabs[0].shape[0] // tm
    full = lambda i: (0, 0)
    tab = pl.BlockSpec((tm, LANES), lambda i: (i % nt, 0))
    return pl.pallas_call(
        _mla_in_kernel,
        grid=(m // tm,),
        in_specs=[
            pl.BlockSpec((tm, d), lambda i: (i, 0)),
            pl.BlockSpec((1, d), full),
            pl.BlockSpec((d, MLA_IN_PAD), full),
            pl.BlockSpec((1, Q_LORA), full),
            pl.BlockSpec((1, KV_LORA), full),
            tab, tab, tab,
        ],
        out_specs=[
            pl.BlockSpec((tm, Q_LORA), lambda i: (i, 0)),
            pl.BlockSpec((tm, KV_LORA), lambda i: (i, 0)),
            pl.BlockSpec((tm, QK_ROPE), lambda i: (i, 0)),
            pl.BlockSpec((tm, LANES), lambda i: (i, 0)),
        ],
        out_shape=[
            jax.ShapeDtypeStruct((m, Q_LORA), BF16),
            jax.ShapeDtypeStruct((m, KV_LORA), F32),
            jax.ShapeDtypeStruct((m, QK_ROPE), F32),
            jax.ShapeDtypeStruct((m, LANES), BF16),
        ],
        compiler_params=_cparams(("parallel",)),
        name="mla_in",
    )(x, nw, w, qnw, kvnw, *tabs)


Q_HEAD_PAD = 2 * LANES
Q_TN = 2 * Q_HEAD_PAD


def _mla_q_kernel(cq_ref, w_ref, c_ref, s1_ref, s2_ref, o_ref):
    p = jnp.dot(cq_ref[...], w_ref[...], preferred_element_type=F32)
    for gi in range(Q_TN // LANES):
        x = p[:, gi * LANES:(gi + 1) * LANES]
        if gi % 2 == 1:
            x = _rope_group(x, c_ref[...], s1_ref[...], s2_ref[...])
        o_ref[:, gi * LANES:(gi + 1) * LANES] = (x * Q_SCALE).astype(BF16)


def _mla_q(cq, wq, tabs):
    m, k = cq.shape
    n = wq.shape[1]
    tm = _pick_tile(m, 512)
    nt = tabs[0].shape[0] // tm
    tab = pl.BlockSpec((tm, LANES), lambda i, j: (i % nt, 0))
    return pl.pallas_call(
        _mla_q_kernel,
        grid=(m // tm, n // Q_TN),
        in_specs=[
            pl.BlockSpec((tm, k), lambda i, j: (i, 0)),
            pl.BlockSpec((k, Q_TN), lambda i, j: (0, j)),
            tab, tab, tab,
        ],
        out_specs=pl.BlockSpec((tm, Q_TN), lambda i, j: (i, j)),
        out_shape=jax.ShapeDtypeStruct((m, n), BF16),
        compiler_params=_cparams(("parallel", "arbitrary")),
        name="mla_q",
    )(cq, wq, *tabs)


QT_PAIR = 2 * QK_NOPE + 2 * QK_ROPE


def _mla_qt_kernel(cq_ref, wt_ref, cos_ref, sin_ref, o_ref):
    p = lax.dot_general(wt_ref[...], cq_ref[...], NT_DIMS, preferred_element_type=F32)
    cos = cos_ref[...]
    sin = sin_ref[...]
    half = QK_ROPE // 2
    for pr in range(MLA_HEADS // 2):
        base = pr * QT_PAIR
        o_ref[0, base:base + 2 * QK_NOPE, :] = (p[base:base + 2 * QK_NOPE] * Q_SCALE).astype(BF16)
        for hh in range(2):
            r0 = base + 2 * QK_NOPE + hh * QK_ROPE
            x1 = p[r0:r0 + half]
            x2 = p[r0 + half:r0 + QK_ROPE]
            o_ref[0, r0:r0 + half, :] = ((x1 * cos - x2 * sin) * Q_SCALE).astype(BF16)
            o_ref[0, r0 + half:r0 + QK_ROPE, :] = ((x1 * sin + x2 * cos) * Q_SCALE).astype(BF16)


def _mla_qt(cq, wq_t, cos_t, sin_t, b, l):
    m, k = cq.shape
    n = wq_t.shape[0]
    tm = _pick_tile(l, 512)
    nt = l // tm
    half = QK_ROPE // 2
    return pl.pallas_call(
        _mla_qt_kernel,
        grid=(m // tm,),
        in_specs=[
            pl.BlockSpec((tm, k), lambda i: (i, 0)),
            pl.BlockSpec((n, k), lambda i: (0, 0)),
            pl.BlockSpec((half, tm), lambda i: (0, i % nt)),
            pl.BlockSpec((half, tm), lambda i: (0, i % nt)),
        ],
        out_specs=pl.BlockSpec((1, n, tm), lambda i: (i // nt, 0, i % nt)),
        out_shape=jax.ShapeDtypeStruct((b, n, l), BF16),
        compiler_params=_cparams(("parallel",)),
        name="mla_qt",
    )(cq, wq_t, cos_t, sin_t)


def _mla_kv_kernel(c_ref, wk_ref, wvt_ref, k_ref, vt_ref):
    cb = c_ref[...].astype(BF16)
    k_ref[...] = jnp.dot(cb, wk_ref[...], preferred_element_type=F32).astype(BF16)
    vt_ref[0] = lax.dot_general(wvt_ref[...], cb, NT_DIMS, preferred_element_type=F32).astype(BF16)


def _mla_kv(ckv, w_uk, w_uv_t, b, l):
    m, k = ckv.shape
    n = w_uk.shape[1]
    tm = _pick_tile(l, 512)
    nt = l // tm
    return pl.pallas_call(
        _mla_kv_kernel,
        grid=(m // tm,),
        in_specs=[
            pl.BlockSpec((tm, k), lambda i: (i, 0)),
            pl.BlockSpec((k, n), lambda i: (0, 0)),
            pl.BlockSpec((n, k), lambda i: (0, 0)),
        ],
        out_specs=[
            pl.BlockSpec((tm, n), lambda i: (i, 0)),
            pl.BlockSpec((1, n, tm), lambda i: (i // nt, 0, i % nt)),
        ],
        out_shape=[
            jax.ShapeDtypeStruct((m, n), BF16),
            jax.ShapeDtypeStruct((b, n, l), BF16),
        ],
        compiler_params=_cparams(("parallel",)),
        name="mla_kv",
    )(ckv, w_uk, w_uv_t)


def _flash_kernel(qi_ref, ki_ref, q_ref, kn_ref, kr_ref, vt_ref, o_ref, m_ref, l_ref, acc_ref, *, t):
    step = pl.program_id(2)
    qi = qi_ref[step]
    ki = ki_ref[step]

    @pl.when(ki == 0)
    def _():
        m_ref[...] = jnp.full_like(m_ref, -jnp.inf)
        l_ref[...] = jnp.zeros_like(l_ref)
        acc_ref[...] = jnp.zeros_like(acc_ref)

    def update(masked):
        kcat = jnp.concatenate([kn_ref[0], kr_ref[0]], axis=1)
        qp = q_ref[0]
        zn = jnp.zeros((QK_NOPE, t), BF16)
        zt = jnp.zeros((LANES - QK_ROPE, t), BF16)
        rope0 = 2 * QK_NOPE
        qts = [jnp.concatenate([qp[0:QK_NOPE], zn, qp[rope0:rope0 + QK_ROPE], zt], axis=0),
               jnp.concatenate([zn, qp[QK_NOPE:rope0], qp[rope0 + QK_ROPE:], zt], axis=0)]
        sts = [jnp.dot(kcat, qts[hh], preferred_element_type=F32) for hh in range(2)]
        for hh in range(2):
            rows = pl.ds(hh * V_DIM, V_DIM)
            st = sts[hh]
            if masked:
                key = lax.broadcasted_iota(jnp.int32, (t, t), 0)
                qry = lax.broadcasted_iota(jnp.int32, (t, t), 1)
                st = jnp.where(key <= qry, st, -jnp.inf)
            m_prev = m_ref[hh]
            m_new = jnp.maximum(m_prev, jnp.max(st, axis=0, keepdims=True))
            alpha = jnp.exp2(m_prev - m_new)
            p = jnp.exp2(st - m_new)
            l_ref[hh] = alpha * l_ref[hh] + jnp.sum(p, axis=0, keepdims=True)
            m_ref[hh] = m_new
            pv = jnp.dot(vt_ref[0, rows, :], p.astype(BF16), preferred_element_type=F32)
            acc_ref[rows, :] = alpha * acc_ref[rows, :] + pv

    @pl.when(ki < qi)
    def _():
        update(False)

    @pl.when(ki == qi)
    def _():
        update(True)
        inv = jnp.concatenate([jnp.broadcast_to(1.0 / l_ref[hh], (V_DIM, t)) for hh in range(2)], axis=0)
        o_ref[0] = jnp.transpose(acc_ref[...] * inv).astype(BF16)


def _flash_prompt(qt, kn, krp, vt, *, t=512):
    b, l, _ = kn.shape
    n = l // t
    qi_tab = jnp.asarray([qi for qi in range(n) for _ in range(qi + 1)], jnp.int32)
    ki_tab = jnp.asarray([ki for qi in range(n) for ki in range(qi + 1)], jnp.int32)
    npairs = MLA_HEADS // 2
    grid_spec = pltpu.PrefetchScalarGridSpec(
        num_scalar_prefetch=2,
        grid=(b, npairs, int(qi_tab.shape[0])),
        in_specs=[
            pl.BlockSpec((1, QT_PAIR, t), lambda i, p, s, qt, kt: (i, p, qt[s])),
            pl.BlockSpec((1, t, LANES), lambda i, p, s, qt, kt: (i, kt[s], p)),
            pl.BlockSpec((1, t, LANES), lambda i, p, s, qt, kt: (i, kt[s], 0)),
            pl.BlockSpec((1, 2 * V_DIM, t), lambda i, p, s, qt, kt: (i, p, kt[s])),
        ],
        out_specs=pl.BlockSpec((1, t, 2 * V_DIM), lambda i, p, s, qt, kt: (i, qt[s], p)),
        scratch_shapes=[
            pltpu.VMEM((2, 1, t), F32),
            pltpu.VMEM((2, 1, t), F32),
            pltpu.VMEM((2 * V_DIM, t), F32),
        ],
    )
    return pl.pallas_call(
        functools.partial(_flash_kernel, t=t),
        grid_spec=grid_spec,
        out_shape=jax.ShapeDtypeStruct((b, l, MLA_HEADS * V_DIM), BF16),
        compiler_params=_cparams(("parallel", "parallel", "arbitrary")),
        name="mla_flash",
    )(qi_tab, ki_tab, qt, kn, krp, vt)


Q_CAT = KV_LORA + LANES


def _qlat_kernel(q_ref, w_ref, o_ref):
    for hh in range(2):
        qh = q_ref[:, hh * Q_HEAD_PAD:(hh + 1) * Q_HEAD_PAD]
        lat = jnp.dot(qh[:, :LANES], w_ref[...], preferred_element_type=F32)
        o_ref[hh] = jnp.concatenate([lat.astype(BF16), qh[:, LANES:]], axis=1)


def _qlat(qp_s, w_uk_t):
    nb = qp_s.shape[0]
    npairs = MLA_HEADS // 2
    return pl.pallas_call(
        _qlat_kernel,
        grid=(npairs,),
        in_specs=[
            pl.BlockSpec((nb, Q_TN), lambda p: (0, p)),
            pl.BlockSpec((LANES, KV_LORA), lambda p: (p, 0)),
        ],
        out_specs=pl.BlockSpec((2, nb, Q_CAT), lambda p: (p, 0, 0)),
        out_shape=jax.ShapeDtypeStruct((MLA_HEADS, nb, Q_CAT), BF16),
        compiler_params=_cparams(("parallel",)),
        name="mla_qlat",
    )(qp_s, w_uk_t)


def _olat_kernel(o_ref, w_ref, out_ref):
    nb = o_ref.shape[0]
    first = lax.broadcasted_iota(jnp.int32, (nb, LANES), 1) < V_DIM
    r0 = jnp.dot(o_ref[:, :KV_LORA], w_ref[...], preferred_element_type=F32)
    r1 = jnp.dot(o_ref[:, KV_LORA:], w_ref[...], preferred_element_type=F32)
    out_ref[...] = jnp.where(first, r0, r1).astype(BF16)


def _olat(o_lat, w_uv):
    nb = o_lat.shape[0]
    npairs = MLA_HEADS // 2
    return pl.pallas_call(
        _olat_kernel,
        grid=(npairs,),
        in_specs=[
            pl.BlockSpec((nb, 2 * KV_LORA), lambda p: (0, p)),
            pl.BlockSpec((KV_LORA, LANES), lambda p: (0, p)),
        ],
        out_specs=pl.BlockSpec((nb, LANES), lambda p: (0, p)),
        out_shape=jax.ShapeDtypeStruct((nb, MLA_HEADS * V_DIM), BF16),
        compiler_params=_cparams(("parallel",)),
        name="mla_olat",
    )(o_lat, w_uv)


PAGES_PER_CHUNK = 32


def _mla_decode_kernel(pt_ref, q_ref, cn_ref, kn_ref, ckv_hbm, krt_hbm, o_ref, ckbuf, krbuf, sem,
                       *, layer, nch):
    ppc = PAGES_PER_CHUNK
    b = pl.program_id(0)
    nb = pl.num_programs(0)

    def copies(bb, c, slot):
        out = []
        for i in range(ppc):
            pg = pt_ref[bb, c * ppc + i]
            out.append(pltpu.make_async_copy(ckv_hbm.at[layer, pg], ckbuf.at[slot, i], sem.at[slot]))
            out.append(pltpu.make_async_copy(krt_hbm.at[layer, pg], krbuf.at[slot, i], sem.at[slot]))
        return out

    def start(bb, c, slot):
        for cp in copies(bb, c, slot):
            cp.start()

    @pl.when(b == 0)
    def _():
        start(0, 0, 0)

    q = q_ref[0]
    q_lat = q[:, :KV_LORA]
    q_rope = q[:, KV_LORA:KV_LORA + QK_ROPE]
    rows = ppc * PAGE_SIZE
    m_prev = jnp.full((MLA_HEADS, 1), -jnp.inf, F32)
    l_prev = jnp.zeros((MLA_HEADS, 1), F32)
    acc = jnp.zeros((MLA_HEADS, KV_LORA), F32)
    for c in range(nch):
        slot = c % 2
        if c + 1 < nch:
            start(b, c + 1, 1 - slot)
        else:
            @pl.when(b + 1 < nb)
            def _():
                start(b + 1, 0, 1 - slot)
        for cp in copies(b, c, slot):
            cp.wait()
        ck = ckbuf[slot].reshape(rows, KV_LORA).astype(BF16)
        krt = jnp.concatenate([krbuf[slot, i] for i in range(ppc)], axis=1).astype(BF16)
        s = lax.dot_general(q_lat, ck, NT_DIMS, preferred_element_type=F32)
        s = s + jnp.dot(q_rope, krt, preferred_element_type=F32)
        m_new = jnp.maximum(m_prev, jnp.max(s, axis=-1, keepdims=True))
        alpha = jnp.exp2(m_prev - m_new)
        p = jnp.exp2(s - m_new)
        l_prev = alpha * l_prev + jnp.sum(p, axis=-1, keepdims=True)
        acc = alpha * acc + jnp.dot(p.astype(BF16), ck, preferred_element_type=F32)
        m_prev = m_new

    cn = cn_ref[0]
    knew = jnp.concatenate([cn.astype(BF16), kn_ref[0]], axis=1).astype(F32)
    s_new = jnp.sum(q.astype(F32) * knew, axis=-1, keepdims=True)
    m_new = jnp.maximum(m_prev, s_new)
    alpha = jnp.exp2(m_prev - m_new)
    p = jnp.exp2(s_new - m_new)
    l_new = alpha * l_prev + p
    acc = alpha * acc + _bf16_round(p) * _bf16_round(cn)
    o_ref[0] = (acc / l_new).astype(BF16)


def _mla_decode(page_table, qcat, ckv_new, krp_new, cache_ckv, cache_krope_t, layer):
    nb, npages = page_table.shape
    nch = npages // PAGES_PER_CHUNK
    assert npages % PAGES_PER_CHUNK == 0 and nch % 2 == 0
    grid_spec = pltpu.PrefetchScalarGridSpec(
        num_scalar_prefetch=1,
        grid=(nb,),
        in_specs=[
            pl.BlockSpec((1, MLA_HEADS, Q_CAT), lambda i, pt: (i, 0, 0)),
            pl.BlockSpec((1, 1, KV_LORA), lambda i, pt: (i, 0, 0)),
            pl.BlockSpec((1, 1, LANES), lambda i, pt: (i, 0, 0)),
            pl.BlockSpec(memory_space=pl.ANY),
            pl.BlockSpec(memory_space=pl.ANY),
        ],
        out_specs=pl.BlockSpec((1, MLA_HEADS, KV_LORA), lambda i, pt: (i, 0, 0)),
        scratch_shapes=[
            pltpu.VMEM((2, PAGES_PER_CHUNK, PAGE_SIZE, KV_LORA), F32),
            pltpu.VMEM((2, PAGES_PER_CHUNK, QK_ROPE, PAGE_SIZE), F32),
            pltpu.SemaphoreType.DMA((2,)),
        ],
    )
    return pl.pallas_call(
        functools.partial(_mla_decode_kernel, layer=layer, nch=nch),
        grid_spec=grid_spec,
        out_shape=jax.ShapeDtypeStruct((nb, MLA_HEADS, KV_LORA), BF16),
        compiler_params=_cparams(("arbitrary",)),
        name="mla_decode",
    )(page_table, qcat, ckv_new.reshape(nb, 1, KV_LORA), krp_new.reshape(nb, 1, LANES),
      cache_ckv, cache_krope_t)


def _rope_angles(pos, half):
    inv = ROPE_THETA ** (-jnp.arange(half, dtype=F32) / half)
    ang = pos.astype(F32)[:, None] * inv[None, :]
    return jnp.cos(ang), jnp.sin(ang)


def _ret_tables(pos):
    cos, sin = _rope_angles(pos, RET_DK // 2)
    return jnp.concatenate([cos, cos], axis=1), jnp.concatenate([-sin, sin], axis=1)


def _mla_tables(pos, rows):
    half = QK_ROPE // 2
    cos, sin = _rope_angles(pos, half)
    n = pos.shape[0]
    c = jnp.concatenate([cos, cos, jnp.ones((n, LANES - QK_ROPE), F32)], axis=1)
    s1 = jnp.concatenate([-sin, jnp.zeros((n, LANES - half), F32)], axis=1)
    s2 = jnp.concatenate([jnp.zeros((n, half), F32), sin, jnp.zeros((n, LANES - QK_ROPE), F32)], axis=1)
    return tuple(jnp.broadcast_to(t, (rows, LANES)) if n == 1 else t for t in (c, s1, s2))


def _pad_lanes(v):
    return jnp.pad(v.astype(F32), (0, LANES - v.shape[0])).reshape(1, LANES)


def _hyb_w_in_layout(w):
    d = w.shape[0]
    qk = 2 * RET_HEADS * RET_DK
    vg = 2 * RET_HEADS * RET_DV
    q_k = w[:, :qk]
    v_g = w[:, qk:qk + vg]
    z = w[:, qk + vg:qk + vg + SSM_D_INNER]
    xbc = w[:, qk + vg + SSM_D_INNER:qk + vg + SSM_D_INNER + CONV_DIM]
    dt = w[:, qk + vg + SSM_D_INNER + CONV_DIM:]
    pieces = [xbc, dt, jnp.zeros((d, OFF_Q - OFF_DT - SSM_HEADS), w.dtype), q_k,
              jnp.zeros((d, OFF_V - OFF_K - RET_HEADS * RET_DK), w.dtype), v_g, z]
    out = jnp.concatenate(pieces, axis=1)
    assert out.shape[1] == HYB_IN_PAD
    return out.astype(BF16)


def _mla_wqt_layout(w_uq):
    k = w_uq.shape[0]
    w4 = w_uq.reshape(k, MLA_HEADS // 2, 2, QK_NOPE + QK_ROPE)
    nope = w4[..., :QK_NOPE].reshape(k, MLA_HEADS // 2, 2 * QK_NOPE)
    ropew = w4[..., QK_NOPE:].reshape(k, MLA_HEADS // 2, 2 * QK_ROPE)
    return jnp.concatenate([nope, ropew], -1).reshape(k, MLA_HEADS // 2 * QT_PAIR).T.astype(BF16)


def _mla_wq_layout(w_uq):
    k = w_uq.shape[0]
    w3 = w_uq.reshape(k, MLA_HEADS, QK_NOPE + QK_ROPE)
    nope, ropew = w3[:, :, :QK_NOPE], w3[:, :, QK_NOPE:]
    z = jnp.zeros_like(nope)
    even = (jnp.arange(MLA_HEADS) % 2 == 0)[None, :, None]
    first = jnp.where(even, jnp.concatenate([nope, z], -1), jnp.concatenate([z, nope], -1))
    second = jnp.concatenate([ropew, jnp.zeros((k, MLA_HEADS, LANES - QK_ROPE), w_uq.dtype)], -1)
    return jnp.concatenate([first, second], -1).reshape(k, MLA_HEADS * Q_HEAD_PAD).astype(BF16)


def kernel(x_prompt, x_sample, state_ret, state_ssm, state_conv, cache_ckv, cache_krope, page_table,
           norm_ffn1, ffn1_w_in, ffn1_w_out, norm_mix, norm_ffn2, ffn2_w_in, ffn2_w_out,
           hyb_w_in, hyb_w_out, hyb_conv_w, hyb_conv_b, hyb_dt_bias, hyb_a_log, hyb_d_skip,
           hyb_norm_w, mla_w_in, mla_q_norm_w, mla_kv_norm_w, mla_w_uq, mla_w_uk, mla_w_uv,
           mla_w_o, final_norm_w):
    bp, sp, d = x_prompt.shape
    bs, ss, _ = x_sample.shape
    assert ss == 1 and sp % CHUNK == 0
    depth = norm_ffn1.shape[0]
    mp = bp * sp
    xp = x_prompt.reshape(mp, d)
    xs = x_sample.reshape(bs, d)
    pos_p = jnp.arange(sp)
    pos_s = PAST_LEN + jnp.arange(1)
    fw = final_norm_w.reshape(1, d)

    ret_cos_p, ret_sin_p = _ret_tables(pos_p)
    ret_cos_s, ret_sin_s = _ret_tables(pos_s)
    mla_tabs_p = _mla_tables(pos_p, sp)
    mla_tabs_s = _mla_tables(pos_s, bs)
    mla_cos_t, mla_sin_t = (t.T for t in _rope_angles(pos_p, QK_ROPE // 2))
    log_gamma = jnp.log1p(-jnp.exp2(-5.0 - jnp.arange(RET_HEADS, dtype=F32)))
    lg_rows = jnp.broadcast_to(log_gamma[:, None], (RET_HEADS, LANES))

    outs = {k: [] for k in ("ret_p", "ret_s", "ssm_p", "ssm_s", "conv_p", "conv_s",
                            "ckv_p", "ckv_s", "kr_p", "kr_s")}
    for layer in range(depth):
        j = layer // 2
        last = layer == depth - 1
        w1i, w1o = ffn1_w_in[layer].astype(BF16), ffn1_w_out[layer].astype(BF16)
        w2i, w2o = ffn2_w_in[layer].astype(BF16), ffn2_w_out[layer].astype(BF16)
        n1 = norm_ffn1[layer].reshape(1, d)
        nm = norm_mix[layer].reshape(1, d)
        n2 = norm_ffn2[layer].reshape(1, d)
        xp = _ffn(xp, n1, w1i, w1o, fw)
        xs = _ffn(xs, n1, w1i, w1o, fw)
        if layer % 2 == 0:
            w_in = _hyb_w_in_layout(hyb_w_in[j])
            w_out = hyb_w_out[j].astype(BF16)
            conv_w = hyb_conv_w[j].astype(F32)
            conv_b = hyb_conv_b[j].reshape(1, CONV_DIM).astype(F32)
            dtb_row = _pad_lanes(hyb_dt_bias[j])
            a_row = _pad_lanes(-jnp.exp(hyb_a_log[j].astype(F32)))
            dsk_row = _pad_lanes(hyb_d_skip[j])
            gnw = hyb_norm_w[j].reshape(1, SSM_D_INNER).astype(F32)
            proj = _norm_matmul(xp, nm, w_in, tn=1408).reshape(bp, sp, HYB_IN_PAD)
            o_ret, r_p = _retention_prompt(proj, ret_cos_p, ret_sin_p, lg_rows.reshape(RET_HEADS, 1, LANES))
            dsk_x = jnp.repeat(hyb_d_skip[j].astype(F32), SSM_HEAD_DIM).reshape(1, SSM_D_INNER)
            o_ssd, s_p, c_p = _ssd_prompt(proj, conv_w, conv_b, dtb_row, a_row, dsk_x, gnw)
            xp = _matmul_res([o_ret.reshape(mp, -1), o_ssd.reshape(mp, -1)], w_out, xp)
            proj_s = _norm_matmul(xs, nm, w_in, tn=1408)
            mixed_s, r_s, s_s, c_s = _hybrid_decode(
                proj_s, state_ret[j], state_ssm[j], state_conv[j], ret_cos_s, ret_sin_s, lg_rows,
                conv_w, conv_b, dtb_row, a_row, dsk_row, gnw)
            xs = _matmul_res([mixed_s.reshape(bs, HYB_MIX)], w_out, xs)
            outs["ret_p"].append(r_p); outs["ret_s"].append(r_s)
            outs["ssm_p"].append(s_p); outs["ssm_s"].append(s_s)
            outs["conv_p"].append(c_p); outs["conv_s"].append(c_s)
        else:
            w_in = jnp.pad(mla_w_in[j], ((0, 0), (0, MLA_IN_PAD - MLA_IN))).astype(BF16)
            qnw = mla_q_norm_w[j].reshape(1, Q_LORA)
            kvnw = mla_kv_norm_w[j].reshape(1, KV_LORA)
            wq = _mla_wq_layout(mla_w_uq[j])
            w_uk2 = mla_w_uk[j].reshape(KV_LORA, MLA_HEADS * QK_NOPE)
            w_uv2 = mla_w_uv[j].reshape(KV_LORA, MLA_HEADS * V_DIM)
            krope_t = jnp.swapaxes(cache_krope, 2, 3)
            w_o = mla_w_o[j].astype(BF16)
            cq, ckv, kr, krp = _mla_in(xp, nm, w_in, qnw, kvnw, mla_tabs_p)
            qt = _mla_qt(cq, _mla_wqt_layout(mla_w_uq[j]), mla_cos_t, mla_sin_t, bp, sp)
            kn, vt = _mla_kv(ckv, w_uk2.astype(BF16), w_uv2.T.astype(BF16), bp, sp)
            o = _flash_prompt(qt, kn.reshape(bp, sp, -1), krp.reshape(bp, sp, LANES), vt)
            xp = _matmul_res([o.reshape(mp, MLA_HEADS * V_DIM)], w_o, xp)
            outs["ckv_p"].append(ckv.reshape(bp, sp, KV_LORA))
            outs["kr_p"].append(kr.reshape(bp, sp, QK_ROPE))
            cq_s, ckv_s, kr_s, krp_s = _mla_in(xs, nm, w_in, qnw, kvnw, mla_tabs_s)
            qp_s = _mla_q(cq_s, wq, mla_tabs_s)
            qcat = jnp.transpose(_qlat(qp_s, w_uk2.T.astype(BF16)), (1, 0, 2))
            o_lat = _mla_decode(page_table, qcat, ckv_s, krp_s, cache_ckv, krope_t, j)
            o_s = _olat(o_lat.reshape(bs, MLA_HEADS * KV_LORA), w_uv2.astype(BF16))
            xs = _matmul_res([o_s], w_o, xs)
            outs["ckv_s"].append(ckv_s.reshape(bs, 1, KV_LORA))
            outs["kr_s"].append(kr_s.reshape(bs, 1, QK_ROPE))
        xp = _ffn(xp, n2, w2i, w2o, fw, final_norm=last)
        xs = _ffn(xs, n2, w2i, w2o, fw, final_norm=last)
    if depth == 0:
        raise ValueError("depth must be positive")
    return (xp.reshape(bp, sp, d), xs.reshape(bs, 1, d),
            jnp.stack(outs["ret_p"]), jnp.stack(outs["ret_s"]),
            jnp.stack(outs["ssm_p"]), jnp.stack(outs["ssm_s"]),
            jnp.stack(outs["conv_p"]), jnp.stack(outs["conv_s"]),
            jnp.stack(outs["ckv_p"]), jnp.stack(outs["ckv_s"]),
            jnp.stack(outs["kr_p"]), jnp.stack(outs["kr_s"]))
```

```python
import functools
import math

import jax
import jax.numpy as jnp
from jax import lax
from jax.experimental import pallas as pl
from jax.experimental.pallas import tpu as pltpu

F32 = jnp.float32
BF16 = jnp.bfloat16

D_MODEL = 1024
D_FF = 2816
RMS_EPS = 1e-6
ROPE_THETA = 10000.0
CHUNK = 128
RET_HEADS = 4
RET_DK = 128
RET_DV = 256
SSM_HEADS = 16
SSM_HEAD_DIM = 64
SSM_D_INNER = 1024
SSM_STATE = 64
SSM_GROUPS = 2
CONV_W = 4
CONV_DIM = 1280
HYB_IN = 5392
HYB_IN_PAD = 5632
HYB_MIX = 2048
MLA_HEADS = 16
Q_LORA = 512
KV_LORA = 256
QK_NOPE = 64
QK_ROPE = 32
V_DIM = 64
MLA_IN = 800
MLA_IN_PAD = 896
MLA_SCALE = (QK_NOPE + QK_ROPE) ** -0.5
Q_SCALE = MLA_SCALE * math.log2(math.e)
PAST_LEN = 16384
PAGE_SIZE = 128

OFF_XBC = 0
OFF_DT = 1280
OFF_Q = 1408
OFF_K = 1920
OFF_V = 2560
OFF_G = 3584
OFF_Z = 4608

LANES = 128
SUBLANES = 8
VMEM_LIMIT_BYTES = 56 * 1024 * 1024

NT_DIMS = (((1,), (1,)), ((), ()))


def _cparams(sem):
    return pltpu.CompilerParams(dimension_semantics=sem, vmem_limit_bytes=VMEM_LIMIT_BYTES)


def _silu(x):
    return x * (0.5 * jnp.tanh(0.5 * x) + 0.5)


def _rms(x, w):
    return x * lax.rsqrt(jnp.mean(x * x, axis=-1, keepdims=True) + RMS_EPS) * w


def _bf16_round(x):
    return x.astype(BF16).astype(F32)


def _pick_tile(m, pref):
    t = min(m, pref)
    while m % t:
        t //= 2
    return t


def _ffn_kernel(x_ref, nw_ref, wg_ref, wu_ref, wo_ref, fw_ref, o_ref, *, final_norm):
    x = x_ref[...]
    h = _rms(x, nw_ref[...]).astype(BF16)
    g = jnp.dot(h, wg_ref[...], preferred_element_type=F32)
    u = jnp.dot(h, wu_ref[...], preferred_element_type=F32)
    a = (_silu(g) * u).astype(BF16)
    y = x + 0.5 * jnp.dot(a, wo_ref[...], preferred_element_type=F32)
    if final_norm:
        y = _rms(y, fw_ref[...])
    o_ref[...] = y


def _ffn(x, nw, w_in, w_out, fw, *, final_norm=False):
    m, d = x.shape
    tm = _pick_tile(m, 512)
    once = pl.Buffered(1)
    return pl.pallas_call(
        functools.partial(_ffn_kernel, final_norm=final_norm),
        grid=(m // tm,),
        in_specs=[
            pl.BlockSpec((tm, d), lambda i: (i, 0)),
            pl.BlockSpec((1, d), lambda i: (0, 0)),
            pl.BlockSpec((d, D_FF), lambda i: (0, 0), pipeline_mode=once),
            pl.BlockSpec((d, D_FF), lambda i: (0, 1), pipeline_mode=once),
            pl.BlockSpec((D_FF, d), lambda i: (0, 0), pipeline_mode=once),
            pl.BlockSpec((1, d), lambda i: (0, 0)),
        ],
        out_specs=pl.BlockSpec((tm, d), lambda i: (i, 0)),
        out_shape=jax.ShapeDtypeStruct((m, d), F32),
        compiler_params=_cparams(("parallel",)),
        name="ffn",
    )(x, nw, w_in, w_in, w_out, fw)


def _norm_matmul_kernel(x_ref, nw_ref, w_ref, o_ref):
    h = _rms(x_ref[...], nw_ref[...]).astype(BF16)
    o_ref[...] = jnp.dot(h, w_ref[...], preferred_element_type=F32)


def _norm_matmul(x, nw, w):
    m, d = x.shape
    n = w.shape[1]
    tm = _pick_tile(m, 512)
    return pl.pallas_call(
        _norm_matmul_kernel,
        grid=(m // tm,),
        in_specs=[
            pl.BlockSpec((tm, d), lambda i: (i, 0)),
            pl.BlockSpec((1, d), lambda i: (0, 0)),
            pl.BlockSpec((d, n), lambda i: (0, 0), pipeline_mode=pl.Buffered(1)),
        ],
        out_specs=pl.BlockSpec((tm, n), lambda i: (i, 0)),
        out_shape=jax.ShapeDtypeStruct((m, n), F32),
        compiler_params=_cparams(("parallel",)),
        name="norm_matmul",
    )(x, nw, w)


def _matmul_res_kernel(*refs, nparts):
    a_refs, w_refs, r_ref, o_ref = refs[:nparts], refs[nparts:2 * nparts], refs[2 * nparts], refs[2 * nparts + 1]
    acc = r_ref[...]
    for a_ref, w_ref in zip(a_refs, w_refs):
        acc = acc + jnp.dot(a_ref[...].astype(BF16), w_ref[...], preferred_element_type=F32)
    o_ref[...] = acc


def _matmul_res(parts, w, res):
    m = res.shape[0]
    n = w.shape[1]
    k = parts[0].shape[1]
    assert all(a.shape == (m, k) for a in parts) and w.shape[0] == k * len(parts)
    tm = _pick_tile(m, 512)
    nparts = len(parts)
    return pl.pallas_call(
        functools.partial(_matmul_res_kernel, nparts=nparts),
        grid=(m // tm,),
        in_specs=([pl.BlockSpec((tm, k), lambda i: (i, 0)) for _ in parts]
                  + [pl.BlockSpec((k, n), functools.partial(lambda i, j: (j, 0), j=j)) for j in range(nparts)]
                  + [pl.BlockSpec((tm, n), lambda i: (i, 0))]),
        out_specs=pl.BlockSpec((tm, n), lambda i: (i, 0)),
        out_shape=jax.ShapeDtypeStruct((m, n), F32),
        compiler_params=_cparams(("parallel",)),
        name="matmul_res",
    )(*parts, *([w] * nparts), res)


def _rope_full(x, cosf, sinf):
    return x * cosf + pltpu.roll(x, RET_DK // 2, 1) * sinf


def _ret_kernel(q_ref, k_ref, v_ref, g_ref, cos_ref, sin_ref, lg_ref, o_ref, so_ref, s_ref, *, cps):
    c = pl.program_id(2)

    @pl.when(c == 0)
    def _():
        s_ref[...] = jnp.zeros_like(s_ref)

    lg = lg_ref[0]
    ii = lax.broadcasted_iota(jnp.int32, (CHUNK, CHUNK), 0)
    jj = lax.broadcasted_iota(jnp.int32, (CHUNK, CHUNK), 1)
    seg = jnp.where(ii >= jj, jnp.exp((ii - jj).astype(F32) * lg), 0.0)
    ri = lax.broadcasted_iota(jnp.int32, (CHUNK, RET_DK), 0).astype(F32)
    qdec = jnp.exp((ri + 1.0) * lg)
    kdec = jnp.exp((CHUNK - 1.0 - ri) * lg)
    cdec = jnp.exp(CHUNK * lg)[:, 0:1]

    s = s_ref[...]
    for t in range(cps):
        rows = pl.ds(t * CHUNK, CHUNK)
        cosf = cos_ref[rows, :]
        sinf = sin_ref[rows, :]
        q = _rope_full(q_ref[0, rows, :], cosf, sinf)
        k = _rope_full(k_ref[0, rows, :], cosf, sinf) * (RET_DK ** -0.5)
        vb = v_ref[0, rows, :].astype(BF16)
        sc = lax.dot_general(q.astype(BF16), k.astype(BF16), NT_DIMS, preferred_element_type=F32) * seg
        y = jnp.dot(sc.astype(BF16), vb, preferred_element_type=F32)
        y = y + jnp.dot((q * qdec).astype(BF16), s.astype(BF16), preferred_element_type=F32)
        kend_t = jnp.transpose(k * kdec).astype(BF16)
        s = cdec * s + jnp.dot(kend_t, vb, preferred_element_type=F32)
        y = y * lax.rsqrt(jnp.mean(y * y, axis=-1, keepdims=True) + RMS_EPS)
        o_ref[0, rows, :] = (y * _silu(g_ref[0, rows, :])).astype(BF16)
    s_ref[...] = s

    @pl.when(c == pl.num_programs(2) - 1)
    def _():
        so_ref[0, 0] = s


def _retention_prompt(proj, cosf, sinf, lg_rows, *, cps=8):
    b, l, _ = proj.shape
    while l % (cps * CHUNK):
        cps //= 2
    rows = cps * CHUNK
    qb = OFF_Q // RET_DK
    kb = OFF_K // RET_DK
    vb = OFF_V // RET_DV
    gb = OFF_G // RET_DV
    return pl.pallas_call(
        functools.partial(_ret_kernel, cps=cps),
        grid=(b, RET_HEADS, l // rows),
        in_specs=[
            pl.BlockSpec((1, rows, RET_DK), lambda i, h, c: (i, c, qb + h)),
            pl.BlockSpec((1, rows, RET_DK), lambda i, h, c: (i, c, kb + h)),
            pl.BlockSpec((1, rows, RET_DV), lambda i, h, c: (i, c, vb + h)),
            pl.BlockSpec((1, rows, RET_DV), lambda i, h, c: (i, c, gb + h)),
            pl.BlockSpec((rows, RET_DK), lambda i, h, c: (c, 0)),
            pl.BlockSpec((rows, RET_DK), lambda i, h, c: (c, 0)),
            pl.BlockSpec((1, 1, LANES), lambda i, h, c: (h, 0, 0)),
        ],
        out_specs=[
            pl.BlockSpec((1, rows, RET_DV), lambda i, h, c: (i, c, h)),
            pl.BlockSpec((1, 1, RET_DK, RET_DV), lambda i, h, c: (i, h, 0, 0)),
        ],
        out_shape=[
            jax.ShapeDtypeStruct((b, l, RET_HEADS * RET_DV), BF16),
            jax.ShapeDtypeStruct((b, RET_HEADS, RET_DK, RET_DV), F32),
        ],
        scratch_shapes=[pltpu.VMEM((RET_DK, RET_DV), F32)],
        compiler_params=_cparams(("parallel", "parallel", "arbitrary")),
        name="retention_prompt",
    )(proj, proj, proj, proj, cosf, sinf, lg_rows)


def _split3(x):
    hi = x.astype(BF16)
    r1 = x - hi.astype(F32)
    mid = r1.astype(BF16)
    lo = (r1 - mid.astype(F32)).astype(BF16)
    return hi, mid, lo


def _cumsum_rows(tril_bf, x):
    hi, mid, lo = _split3(x)
    out = jnp.dot(tril_bf, lo, preferred_element_type=F32)
    out = out + jnp.dot(tril_bf, mid, preferred_element_type=F32)
    return out + jnp.dot(tril_bf, hi, preferred_element_type=F32)


def _shift_rows(cur, tail, s):
    r = pltpu.roll(cur, s, 0)
    pt = pltpu.roll(tail, s, 0)
    row = lax.broadcasted_iota(jnp.int32, (SUBLANES, cur.shape[1]), 0)
    top = jnp.where(row < s, pt, r[0:SUBLANES])
    return jnp.concatenate([top, r[SUBLANES:]], axis=0)


def _softplus(x):
    return jnp.maximum(x, 0.0) + jnp.log(1.0 + jnp.exp(-jnp.abs(x)))


def _expand(x, e_bf, parts=3):
    hi, mid, lo = _split3(x)
    out = jnp.dot(mid, e_bf, preferred_element_type=F32)
    if parts == 3:
        out = jnp.dot(lo, e_bf, preferred_element_type=F32) + out
    return out + jnp.dot(hi, e_bf, preferred_element_type=F32)


def _ssd_kernel(xbc_ref, z0_ref, z1_ref, dt_ref, cw_ref, cb_ref, dtb_ref, a_ref, dskx_ref, nw_ref,
                e64_ref, e128_ref, o_ref, so_ref, co_ref, s_ref, tail_ref):
    c = pl.program_id(1)
    z_refs = (z0_ref, z1_ref)

    @pl.when(c == 0)
    def _():
        s_ref[...] = jnp.zeros_like(s_ref)
        tail_ref[...] = jnp.zeros_like(tail_ref)

    raw = xbc_ref[0]
    tail = tail_ref[...]
    acc = raw * cw_ref[CONV_W - 1:CONV_W, :] + cb_ref[...]
    for s in range(1, CONV_W):
        acc = acc + _shift_rows(raw, tail, s) * cw_ref[CONV_W - 1 - s:CONV_W - s, :]
    xbc = _silu(acc)
    tail_ref[...] = raw[CHUNK - SUBLANES:, :]

    @pl.when(c == pl.num_programs(1) - 1)
    def _():
        co_ref[0] = raw[CHUNK - (CONV_W - 1):, :]

    ii = lax.broadcasted_iota(jnp.int32, (CHUNK, CHUNK), 0)
    jj = lax.broadcasted_iota(jnp.int32, (CHUNK, CHUNK), 1)
    causal = ii >= jj
    tril_bf = jnp.where(causal, 1.0, 0.0).astype(BF16)

    dt = _softplus(dt_ref[0] + dtb_ref[...])
    la = dt * a_ref[...]
    cum = _cumsum_rows(tril_bf, la)
    cum_t = jnp.transpose(cum)

    dt_x = _expand(dt, e64_ref[...], parts=2)
    cum_x = _expand(cum, e64_ref[...])
    cc_all = _expand(cum, e128_ref[...])
    x_all = xbc[:, :SSM_D_INNER]
    xdt_all = x_all * dt_x
    ecum_x = jnp.exp(cum_x)
    dec_x = ecum_x[CHUNK - 1:CHUNK, :]

    b_all = xbc[:, SSM_D_INNER:SSM_D_INNER + LANES]
    c_all = xbc[:, SSM_D_INNER + LANES:SSM_D_INNER + 2 * LANES]
    b_all_t = jnp.transpose(b_all)
    left = lax.broadcasted_iota(jnp.int32, (CHUNK, LANES), 1) < SSM_HEAD_DIM
    top = lax.broadcasted_iota(jnp.int32, (CHUNK, LANES), 0) < SSM_STATE
    blockdiag = left == top
    pairs_per_group = SSM_HEADS // SSM_GROUPS // 2
    gw = SSM_D_INNER // SSM_GROUPS
    for g in range(SSM_GROUPS):
        bg = b_all[:, g * SSM_STATE:(g + 1) * SSM_STATE]
        cg = c_all[:, g * SSM_STATE:(g + 1) * SSM_STATE]
        bg_t = b_all_t[g * SSM_STATE:(g + 1) * SSM_STATE, :]
        cb = lax.dot_general(cg.astype(BF16), bg.astype(BF16), NT_DIMS, preferred_element_type=F32)
        cg2 = jnp.concatenate([cg, cg], axis=1)
        ys = []
        for pp in range(pairs_per_group):
            pr = g * pairs_per_group + pp
            lanes = slice(pr * LANES, (pr + 1) * LANES)
            lhs, kend = [], []
            for h in (2 * pr, 2 * pr + 1):
                cr = cum_t[h:h + 1, :]
                seg = jnp.exp(jnp.where(causal, cc_all[:, h * LANES:(h + 1) * LANES] - cr, -jnp.inf))
                lhs.append((cb * seg).astype(BF16))
                kend.append(bg_t * jnp.exp(cr[:, CHUNK - 1:CHUNK] - cr))
            xdt_p = xdt_all[:, lanes]
            rhs = jnp.concatenate([jnp.where(left, xdt_p, 0.0), jnp.where(left, 0.0, xdt_p)], axis=0)
            y = jnp.dot(jnp.concatenate(lhs, axis=1), rhs.astype(BF16), preferred_element_type=F32)
            sp = s_ref[pr]
            y = y + jnp.dot((cg2 * ecum_x[:, lanes]).astype(BF16), sp.astype(BF16), preferred_element_type=F32)
            ys.append(y + dskx_ref[:, lanes] * x_all[:, lanes])
            upd = jnp.dot(jnp.concatenate(kend, axis=0).astype(BF16), xdt_p.astype(BF16),
                          preferred_element_type=F32)
            s_ref[pr] = sp * dec_x[:, lanes] + jnp.where(blockdiag, upd, 0.0)
        yg = jnp.concatenate(ys, axis=1) * _silu(z_refs[g][0])
        yg = yg * lax.rsqrt(jnp.mean(yg * yg, axis=-1, keepdims=True) + RMS_EPS)
        o_ref[0, :, g * gw:(g + 1) * gw] = (yg * nw_ref[:, g * gw:(g + 1) * gw]).astype(BF16)

    @pl.when(c == pl.num_programs(1) - 1)
    def _():
        for pr in range(SSM_HEADS // 2):
            sp = s_ref[pr]
            so_ref[0, 2 * pr] = sp[:SSM_STATE, :SSM_HEAD_DIM]
            so_ref[0, 2 * pr + 1] = sp[SSM_STATE:, SSM_HEAD_DIM:]


def _head_expansion(width):
    col = jnp.arange(SSM_HEADS * width) // width
    return (jnp.arange(LANES)[:, None] == col[None, :]).astype(BF16)


def _ssd_prompt(proj, conv_w, conv_b, dtb_row, a_row, dsk_x, norm_w):
    b, l, _ = proj.shape
    gw = SSM_D_INNER // SSM_GROUPS
    full = lambda i, c: (0, 0)
    return pl.pallas_call(
        _ssd_kernel,
        grid=(b, l // CHUNK),
        in_specs=[
            pl.BlockSpec((1, CHUNK, CONV_DIM), lambda i, c: (i, c, OFF_XBC // CONV_DIM)),
            pl.BlockSpec((1, CHUNK, gw), lambda i, c: (i, c, OFF_Z // gw)),
            pl.BlockSpec((1, CHUNK, gw), lambda i, c: (i, c, OFF_Z // gw + 1)),
            pl.BlockSpec((1, CHUNK, LANES), lambda i, c: (i, c, OFF_DT // LANES)),
            pl.BlockSpec((CONV_W, CONV_DIM), full),
            pl.BlockSpec((1, CONV_DIM), full),
            pl.BlockSpec((1, LANES), full),
            pl.BlockSpec((1, LANES), full),
            pl.BlockSpec((1, SSM_D_INNER), full),
            pl.BlockSpec((1, SSM_D_INNER), full),
            pl.BlockSpec((LANES, SSM_HEADS * SSM_HEAD_DIM), full),
            pl.BlockSpec((LANES, SSM_HEADS * LANES), full),
        ],
        out_specs=[
            pl.BlockSpec((1, CHUNK, SSM_D_INNER), lambda i, c: (i, c, 0)),
            pl.BlockSpec((1, SSM_HEADS, SSM_STATE, SSM_HEAD_DIM), lambda i, c: (i, 0, 0, 0)),
            pl.BlockSpec((1, CONV_W - 1, CONV_DIM), lambda i, c: (i, 0, 0)),
        ],
        out_shape=[
            jax.ShapeDtypeStruct((b, l, SSM_D_INNER), BF16),
            jax.ShapeDtypeStruct((b, SSM_HEADS, SSM_STATE, SSM_HEAD_DIM), F32),
            jax.ShapeDtypeStruct((b, CONV_W - 1, CONV_DIM), F32),
        ],
        scratch_shapes=[
            pltpu.VMEM((SSM_HEADS // 2, 2 * SSM_STATE, 2 * SSM_HEAD_DIM), F32),
            pltpu.VMEM((SUBLANES, CONV_DIM), F32),
        ],
        compiler_params=_cparams(("parallel", "arbitrary")),
        name="ssd_prompt",
    )(proj, proj, proj, proj, conv_w, conv_b, dtb_row, a_row, dsk_x, norm_w,
      _head_expansion(SSM_HEAD_DIM), _head_expansion(LANES))


def _bcast_rows(x, n):
    return jnp.broadcast_to(x, (n, x.shape[1]))


def _column_matrix(row):
    return jnp.transpose(_bcast_rows(row, LANES))


def _hyb_decode_kernel(row_ref, sr_ref, ss_ref, cs_ref, cos_ref, sin_ref, lg_ref, cw_ref, cb_ref,
                       dtb_ref, a_ref, dsk_ref, nw_ref, o_ref, sro_ref, sso_ref, co_ref):
    row = row_ref[0]
    cosf = cos_ref[...]
    sinf = sin_ref[...]
    for h in range(RET_HEADS):
        q = _rope_full(_bcast_rows(row[:, OFF_Q + h * RET_DK:OFF_Q + (h + 1) * RET_DK], SUBLANES), cosf, sinf)
        k = _rope_full(_bcast_rows(row[:, OFF_K + h * RET_DK:OFF_K + (h + 1) * RET_DK], SUBLANES), cosf, sinf)
        k = k * (RET_DK ** -0.5)
        v = _bf16_round(row[:, OFF_V + h * RET_DV:OFF_V + (h + 1) * RET_DV])
        g = row[:, OFF_G + h * RET_DV:OFF_G + (h + 1) * RET_DV]
        gamma = jnp.exp(lg_ref[h:h + 1, :])
        qb = _bf16_round(q)
        kb = _bf16_round(k)
        kcol = _column_matrix(kb[0:1])
        s0 = sr_ref[0, h]
        sro_ref[0, h] = gamma[:, 0:1] * s0 + jnp.concatenate([kcol, kcol], axis=1) * v
        qcol = _column_matrix(_bf16_round(q * gamma)[0:1])
        y = jnp.sum(jnp.concatenate([qcol, qcol], axis=1) * _bf16_round(s0), axis=0, keepdims=True)
        score = jnp.sum(qb * kb, axis=-1, keepdims=True)[0:1]
        y = y + _bf16_round(score) * v
        y = y * lax.rsqrt(jnp.mean(y * y, axis=-1, keepdims=True) + RMS_EPS)
        o_ref[0, :, h * RET_DV:(h + 1) * RET_DV] = y * _silu(g)
    cs = cs_ref[0]
    raw = row[:, OFF_XBC:OFF_XBC + CONV_DIM]
    acc = raw * cw_ref[CONV_W - 1:CONV_W, :] + cb_ref[...]
    for w in range(CONV_W - 1):
        acc = acc + cs[w:w + 1, :] * cw_ref[w:w + 1, :]
    xbc = _silu(acc)
    co_ref[0, 0:CONV_W - 2, :] = cs[1:CONV_W - 1, :]
    co_ref[0, CONV_W - 2:CONV_W - 1, :] = raw
    dt = _softplus(row[:, OFF_DT:OFF_DT + LANES] + dtb_ref[...])
    la = dt * a_ref[...]
    dec = jnp.exp(la)
    b_all = _bf16_round(xbc[:, SSM_D_INNER:SSM_D_INNER + LANES])
    c_all = xbc[:, SSM_D_INNER + LANES:SSM_D_INNER + 2 * LANES]
    bcol = _column_matrix(b_all)
    ccol = _column_matrix(_bf16_round(c_all))
    rep = SSM_HEADS // SSM_GROUPS
    ys = []
    for h in range(SSM_HEADS):
        g = h // rep
        s0 = ss_ref[0, h]
        xh = xbc[:, h * SSM_HEAD_DIM:(h + 1) * SSM_HEAD_DIM]
        xdt = _bf16_round(xh * dt[:, h:h + 1])
        dech = dec[:, h:h + 1]
        bg = b_all[:, g * SSM_STATE:(g + 1) * SSM_STATE]
        cg = c_all[:, g * SSM_STATE:(g + 1) * SSM_STATE]
        sso_ref[0, h] = dech * s0 + bcol[g * SSM_STATE:(g + 1) * SSM_STATE, 0:SSM_HEAD_DIM] * xdt
        y = dech * jnp.sum(ccol[g * SSM_STATE:(g + 1) * SSM_STATE, 0:SSM_HEAD_DIM] * _bf16_round(s0),
                           axis=0, keepdims=True)
        score = jnp.sum(_bf16_round(cg) * bg, axis=-1, keepdims=True)
        ys.append(y + _bf16_round(score) * xdt + dsk_ref[:, h:h + 1] * xh)
    gw = rep * SSM_HEAD_DIM
    for g in range(SSM_GROUPS):
        yg = jnp.concatenate(ys[g * rep:(g + 1) * rep], axis=1)
        yg = yg * _silu(row[:, OFF_Z + g * gw:OFF_Z + (g + 1) * gw])
        yg = yg * lax.rsqrt(jnp.mean(yg * yg, axis=-1, keepdims=True) + RMS_EPS)
        lo = RET_HEADS * RET_DV + g * gw
        o_ref[0, :, lo:lo + gw] = yg * nw_ref[:, g * gw:(g + 1) * gw]


def _hybrid_decode(proj_s, state_ret, state_ssm, state_conv, cos_row, sin_row, lg_rows, conv_w, conv_b,
                   dtb_row, a_row, dsk_row, norm_w):
    nb = proj_s.shape[0]
    full = lambda i: (0, 0)
    return pl.pallas_call(
        _hyb_decode_kernel,
        grid=(nb,),
        in_specs=[
            pl.BlockSpec((1, 1, HYB_IN_PAD), lambda i: (i, 0, 0)),
            pl.BlockSpec((1, RET_HEADS, RET_DK, RET_DV), lambda i: (i, 0, 0, 0)),
            pl.BlockSpec((1, SSM_HEADS, SSM_STATE, SSM_HEAD_DIM), lambda i: (i, 0, 0, 0)),
            pl.BlockSpec((1, CONV_W - 1, CONV_DIM), lambda i: (i, 0, 0)),
            pl.BlockSpec((1, RET_DK), full),
            pl.BlockSpec((1, RET_DK), full),
            pl.BlockSpec((RET_HEADS, LANES), full),
            pl.BlockSpec((CONV_W, CONV_DIM), full),
            pl.BlockSpec((1, CONV_DIM), full),
            pl.BlockSpec((1, LANES), full),
            pl.BlockSpec((1, LANES), full),
            pl.BlockSpec((1, LANES), full),
            pl.BlockSpec((1, SSM_D_INNER), full),
        ],
        out_specs=[
            pl.BlockSpec((1, 1, HYB_MIX), lambda i: (i, 0, 0)),
            pl.BlockSpec((1, RET_HEADS, RET_DK, RET_DV), lambda i: (i, 0, 0, 0)),
            pl.BlockSpec((1, SSM_HEADS, SSM_STATE, SSM_HEAD_DIM), lambda i: (i, 0, 0, 0)),
            pl.BlockSpec((1, CONV_W - 1, CONV_DIM), lambda i: (i, 0, 0)),
        ],
        out_shape=[
            jax.ShapeDtypeStruct((nb, 1, HYB_MIX), F32),
            jax.ShapeDtypeStruct(state_ret.shape, F32),
            jax.ShapeDtypeStruct(state_ssm.shape, F32),
            jax.ShapeDtypeStruct(state_conv.shape, F32),
        ],
        compiler_params=_cparams(("parallel",)),
        name="hybrid_decode",
    )(proj_s.reshape(nb, 1, HYB_IN_PAD), state_ret, state_ssm, state_conv, cos_row, sin_row, lg_rows,
      conv_w, conv_b, dtb_row, a_row, dsk_row, norm_w)


def _rope_group(x, c, s1, s2):
    half = QK_ROPE // 2
    return x * c + pltpu.roll(x, LANES - half, 1) * s1 + pltpu.roll(x, half, 1) * s2


def _mla_in_kernel(x_ref, nw_ref, w_ref, qnw_ref, kvnw_ref, c_ref, s1_ref, s2_ref,
                   cq_ref, ckv_ref, kr_ref, krp_ref):
    h = _rms(x_ref[...], nw_ref[...]).astype(BF16)
    p = jnp.dot(h, w_ref[...], preferred_element_type=F32)
    cq_ref[...] = _rms(p[:, :Q_LORA], qnw_ref[...]).astype(BF16)
    ckv_ref[...] = _rms(p[:, Q_LORA:Q_LORA + KV_LORA], kvnw_ref[...])
    kr = _rope_group(p[:, Q_LORA + KV_LORA:], c_ref[...], s1_ref[...], s2_ref[...])
    kr_ref[...] = kr[:, :QK_ROPE]
    krp_ref[...] = kr.astype(BF16)


def _mla_in(x, nw, w, qnw, kvnw, tabs):
    m, d = x.shape
    tm = _pick_tile(m, 512)
    nt = tabs[0].shape[0] // tm
    full = lambda i: (0, 0)
    tab = pl.BlockSpec((tm, LANES), lambda i: (i % nt, 0))
    return pl.pallas_call(
        _mla_in_kernel,
        grid=(m // tm,),
        in_specs=[
            pl.BlockSpec((tm, d), lambda i: (i, 0)),
            pl.BlockSpec((1, d), full),
            pl.BlockSpec((d, MLA_IN_PAD), full),
            pl.BlockSpec((1, Q_LORA), full),
            pl.BlockSpec((1, KV_LORA), full),
            tab, tab, tab,
        ],
        out_specs=[
            pl.BlockSpec((tm, Q_LORA), lambda i: (i, 0)),
            pl.BlockSpec((tm, KV_LORA), lambda i: (i, 0)),
            pl.BlockSpec((tm, QK_ROPE), lambda i: (i, 0)),
            pl.BlockSpec((tm, LANES), lambda i: (i, 0)),
        ],
        out_shape=[
            jax.ShapeDtypeStruct((m, Q_LORA), BF16),
            jax.ShapeDtypeStruct((m, KV_LORA), F32),
            jax.ShapeDtypeStruct((m, QK_ROPE), F32),
            jax.ShapeDtypeStruct((m, LANES), BF16),
        ],
        compiler_params=_cparams(("parallel",)),
        name="mla_in",
    )(x, nw, w, qnw, kvnw, *tabs)


Q_HEAD_PAD = 2 * LANES
Q_TN = 2 * Q_HEAD_PAD


def _mla_q_kernel(cq_ref, w_ref, c_ref, s1_ref, s2_ref, o_ref):
    p = jnp.dot(cq_ref[...], w_ref[...], preferred_element_type=F32)
    for gi in range(Q_TN // LANES):
        x = p[:, gi * LANES:(gi + 1) * LANES]
        if gi % 2 == 1:
            x = _rope_group(x, c_ref[...], s1_ref[...], s2_ref[...])
        o_ref[:, gi * LANES:(gi + 1) * LANES] = (x * Q_SCALE).astype(BF16)


def _mla_q(cq, wq, tabs):
    m, k = cq.shape
    n = wq.shape[1]
    tm = _pick_tile(m, 512)
    nt = tabs[0].shape[0] // tm
    tab = pl.BlockSpec((tm, LANES), lambda i, j: (i % nt, 0))
    return pl.pallas_call(
        _mla_q_kernel,
        grid=(m // tm, n // Q_TN),
        in_specs=[
            pl.BlockSpec((tm, k), lambda i, j: (i, 0)),
            pl.BlockSpec((k, Q_TN), lambda i, j: (0, j)),
            tab, tab, tab,
        ],
        out_specs=pl.BlockSpec((tm, Q_TN), lambda i, j: (i, j)),
        out_shape=jax.ShapeDtypeStruct((m, n), BF16),
        compiler_params=_cparams(("parallel", "arbitrary")),
        name="mla_q",
    )(cq, wq, *tabs)


QT_PAIR = 2 * QK_NOPE + 2 * QK_ROPE


def _mla_qt_kernel(cq_ref, wt_ref, cos_ref, sin_ref, o_ref):
    p = lax.dot_general(wt_ref[...], cq_ref[...], NT_DIMS, preferred_element_type=F32)
    cos = cos_ref[...]
    sin = sin_ref[...]
    half = QK_ROPE // 2
    for pr in range(MLA_HEADS // 2):
        base = pr * QT_PAIR
        o_ref[0, base:base + 2 * QK_NOPE, :] = (p[base:base + 2 * QK_NOPE] * Q_SCALE).astype(BF16)
        for hh in range(2):
            r0 = base + 2 * QK_NOPE + hh * QK_ROPE
            x1 = p[r0:r0 + half]
            x2 = p[r0 + half:r0 + QK_ROPE]
            o_ref[0, r0:r0 + half, :] = ((x1 * cos - x2 * sin) * Q_SCALE).astype(BF16)
            o_ref[0, r0 + half:r0 + QK_ROPE, :] = ((x1 * sin + x2 * cos) * Q_SCALE).astype(BF16)


def _mla_qt(cq, wq_t, cos_t, sin_t, b, l):
    m, k = cq.shape
    n = wq_t.shape[0]
    tm = _pick_tile(l, 512)
    nt = l // tm
    half = QK_ROPE // 2
    return pl.pallas_call(
        _mla_qt_kernel,
        grid=(m // tm,),
        in_specs=[
            pl.BlockSpec((tm, k), lambda i: (i, 0)),
            pl.BlockSpec((n, k), lambda i: (0, 0)),
            pl.BlockSpec((half, tm), lambda i: (0, i % nt)),
            pl.BlockSpec((half, tm), lambda i: (0, i % nt)),
        ],
        out_specs=pl.BlockSpec((1, n, tm), lambda i: (i // nt, 0, i % nt)),
        out_shape=jax.ShapeDtypeStruct((b, n, l), BF16),
        compiler_params=_cparams(("parallel",)),
        name="mla_qt",
    )(cq, wq_t, cos_t, sin_t)


def _mla_kv_kernel(c_ref, wk_ref, wvt_ref, k_ref, vt_ref):
    cb = c_ref[...].astype(BF16)
    k_ref[...] = jnp.dot(cb, wk_ref[...], preferred_element_type=F32).astype(BF16)
    vt_ref[0] = lax.dot_general(wvt_ref[...], cb, NT_DIMS, preferred_element_type=F32).astype(BF16)


def _mla_kv(ckv, w_uk, w_uv_t, b, l):
    m, k = ckv.shape
    n = w_uk.shape[1]
    tm = _pick_tile(l, 512)
    nt = l // tm
    return pl.pallas_call(
        _mla_kv_kernel,
        grid=(m // tm,),
        in_specs=[
            pl.BlockSpec((tm, k), lambda i: (i, 0)),
            pl.BlockSpec((k, n), lambda i: (0, 0)),
            pl.BlockSpec((n, k), lambda i: (0, 0)),
        ],
        out_specs=[
            pl.BlockSpec((tm, n), lambda i: (i, 0)),
            pl.BlockSpec((1, n, tm), lambda i: (i // nt, 0, i % nt)),
        ],
        out_shape=[
            jax.ShapeDtypeStruct((m, n), BF16),
            jax.ShapeDtypeStruct((b, n, l), BF16),
        ],
        compiler_params=_cparams(("parallel",)),
        name="mla_kv",
    )(ckv, w_uk, w_uv_t)


FLASH_PAIRS = 4


def _flash_kernel(qi_ref, ki_ref, q_ref, kn_ref, kr_ref, vt_ref, o_ref, m_ref, l_ref, acc_ref, *, t):
    step = pl.program_id(2)
    qi = qi_ref[step]
    ki = ki_ref[step]
    nheads = 2 * FLASH_PAIRS

    @pl.when(ki == 0)
    def _():
        m_ref[...] = jnp.full_like(m_ref, -jnp.inf)
        l_ref[...] = jnp.zeros_like(l_ref)
        acc_ref[...] = jnp.zeros_like(acc_ref)

    def update(masked):
        zn = jnp.zeros((QK_NOPE, t), BF16)
        zt = jnp.zeros((LANES - QK_ROPE, t), BF16)
        rope0 = 2 * QK_NOPE
        kr = kr_ref[0]
        sts = []
        for pi in range(FLASH_PAIRS):
            kcat = jnp.concatenate([kn_ref[0, :, pi * LANES:(pi + 1) * LANES], kr], axis=1)
            qp = q_ref[0, pi * QT_PAIR:(pi + 1) * QT_PAIR, :]
            qts = [jnp.concatenate([qp[0:QK_NOPE], zn, qp[rope0:rope0 + QK_ROPE], zt], axis=0),
                   jnp.concatenate([zn, qp[QK_NOPE:rope0], qp[rope0 + QK_ROPE:], zt], axis=0)]
            sts += [jnp.dot(kcat, qts[hh], preferred_element_type=F32) for hh in range(2)]
        for hh in range(nheads):
            rows = pl.ds(hh * V_DIM, V_DIM)
            st = sts[hh]
            if masked:
                key = lax.broadcasted_iota(jnp.int32, (t, t), 0)
                qry = lax.broadcasted_iota(jnp.int32, (t, t), 1)
                st = jnp.where(key <= qry, st, -jnp.inf)
            m_prev = m_ref[hh]
            m_new = jnp.maximum(m_prev, jnp.max(st, axis=0, keepdims=True))
            alpha = jnp.exp2(m_prev - m_new)
            p = jnp.exp2(st - m_new)
            l_ref[hh] = alpha * l_ref[hh] + jnp.sum(p, axis=0, keepdims=True)
            m_ref[hh] = m_new
            pv = jnp.dot(vt_ref[0, rows, :], p.astype(BF16), preferred_element_type=F32)
            acc_ref[rows, :] = alpha * acc_ref[rows, :] + pv

    @pl.when(ki < qi)
    def _():
        update(False)

    @pl.when(ki == qi)
    def _():
        update(True)
        inv = jnp.concatenate([jnp.broadcast_to(1.0 / l_ref[hh], (V_DIM, t)) for hh in range(nheads)], axis=0)
        o_ref[0] = jnp.transpose(acc_ref[...] * inv).astype(BF16)


def _flash_prompt(qt, kn, krp, vt, *, t=512):
    b, l, _ = kn.shape
    n = l // t
    qi_tab = jnp.asarray([qi for qi in range(n) for _ in range(qi + 1)], jnp.int32)
    ki_tab = jnp.asarray([ki for qi in range(n) for ki in range(qi + 1)], jnp.int32)
    nsteps = MLA_HEADS // (2 * FLASH_PAIRS)
    hw = 2 * FLASH_PAIRS * V_DIM
    grid_spec = pltpu.PrefetchScalarGridSpec(
        num_scalar_prefetch=2,
        grid=(b, nsteps, int(qi_tab.shape[0])),
        in_specs=[
            pl.BlockSpec((1, FLASH_PAIRS * QT_PAIR, t), lambda i, p, s, qt, kt: (i, p, qt[s])),
            pl.BlockSpec((1, t, FLASH_PAIRS * LANES), lambda i, p, s, qt, kt: (i, kt[s], p)),
            pl.BlockSpec((1, t, LANES), lambda i, p, s, qt, kt: (i, kt[s], 0)),
            pl.BlockSpec((1, hw, t), lambda i, p, s, qt, kt: (i, p, kt[s])),
        ],
        out_specs=pl.BlockSpec((1, t, hw), lambda i, p, s, qt, kt: (i, qt[s], p)),
        scratch_shapes=[
            pltpu.VMEM((2 * FLASH_PAIRS, 1, t), F32),
            pltpu.VMEM((2 * FLASH_PAIRS, 1, t), F32),
            pltpu.VMEM((hw, t), F32),
        ],
    )
    return pl.pallas_call(
        functools.partial(_flash_kernel, t=t),
        grid_spec=grid_spec,
        out_shape=jax.ShapeDtypeStruct((b, l, MLA_HEADS * V_DIM), BF16),
        compiler_params=_cparams(("parallel", "parallel", "arbitrary")),
        name="mla_flash",
    )(qi_tab, ki_tab, qt, kn, krp, vt)


Q_CAT = KV_LORA + LANES


def _qlat_kernel(q_ref, w_ref, o_ref):
    for hh in range(2):
        qh = q_ref[:, hh * Q_HEAD_PAD:(hh + 1) * Q_HEAD_PAD]
        lat = jnp.dot(qh[:, :LANES], w_ref[...], preferred_element_type=F32)
        o_ref[hh] = jnp.concatenate([lat.astype(BF16), qh[:, LANES:]], axis=1)


def _qlat(qp_s, w_uk_t):
    nb = qp_s.shape[0]
    npairs = MLA_HEADS // 2
    return pl.pallas_call(
        _qlat_kernel,
        grid=(npairs,),
        in_specs=[
            pl.BlockSpec((nb, Q_TN), lambda p: (0, p)),
            pl.BlockSpec((LANES, KV_LORA), lambda p: (p, 0)),
        ],
        out_specs=pl.BlockSpec((2, nb, Q_CAT), lambda p: (p, 0, 0)),
        out_shape=jax.ShapeDtypeStruct((MLA_HEADS, nb, Q_CAT), BF16),
        compiler_params=_cparams(("parallel",)),
        name="mla_qlat",
    )(qp_s, w_uk_t)


def _olat_kernel(o_ref, w_ref, out_ref):
    nb = o_ref.shape[0]
    first = lax.broadcasted_iota(jnp.int32, (nb, LANES), 1) < V_DIM
    r0 = jnp.dot(o_ref[:, :KV_LORA], w_ref[...], preferred_element_type=F32)
    r1 = jnp.dot(o_ref[:, KV_LORA:], w_ref[...], preferred_element_type=F32)
    out_ref[...] = jnp.where(first, r0, r1).astype(BF16)


def _olat(o_lat, w_uv):
    nb = o_lat.shape[0]
    npairs = MLA_HEADS // 2
    return pl.pallas_call(
        _olat_kernel,
        grid=(npairs,),
        in_specs=[
            pl.BlockSpec((nb, 2 * KV_LORA), lambda p: (0, p)),
            pl.BlockSpec((KV_LORA, LANES), lambda p: (0, p)),
        ],
        out_specs=pl.BlockSpec((nb, LANES), lambda p: (0, p)),
        out_shape=jax.ShapeDtypeStruct((nb, MLA_HEADS * V_DIM), BF16),
        compiler_params=_cparams(("parallel",)),
        name="mla_olat",
    )(o_lat, w_uv)


PAGES_PER_CHUNK = 32


def _mla_decode_kernel(pt_ref, q_ref, cn_ref, kn_ref, ckv_hbm, krt_hbm, o_ref, ckbuf, krbuf, sem,
                       *, layer, nch):
    ppc = PAGES_PER_CHUNK
    b = pl.program_id(0)
    nb = pl.num_programs(0)

    def copies(bb, c, slot):
        out = []
        for i in range(ppc):
            pg = pt_ref[bb, c * ppc + i]
            out.append(pltpu.make_async_copy(ckv_hbm.at[layer, pg], ckbuf.at[slot, i], sem.at[slot]))
            out.append(pltpu.make_async_copy(krt_hbm.at[layer, pg], krbuf.at[slot, i], sem.at[slot]))
        return out

    def start(bb, c, slot):
        for cp in copies(bb, c, slot):
            cp.start()

    @pl.when(b == 0)
    def _():
        start(0, 0, 0)

    q = q_ref[0]
    q_lat = q[:, :KV_LORA]
    q_rope = q[:, KV_LORA:KV_LORA + QK_ROPE]
    rows = ppc * PAGE_SIZE
    m_prev = jnp.full((MLA_HEADS, 1), -jnp.inf, F32)
    l_prev = jnp.zeros((MLA_HEADS, 1), F32)
    acc = jnp.zeros((MLA_HEADS, KV_LORA), F32)
    for c in range(nch):
        slot = c % 2
        if c + 1 < nch:
            start(b, c + 1, 1 - slot)
        else:
            @pl.when(b + 1 < nb)
            def _():
                start(b + 1, 0, 1 - slot)
        for cp in copies(b, c, slot):
            cp.wait()
        ck = ckbuf[slot].reshape(rows, KV_LORA).astype(BF16)
        krt = jnp.concatenate([krbuf[slot, i] for i in range(ppc)], axis=1).astype(BF16)
        s = lax.dot_general(q_lat, ck, NT_DIMS, preferred_element_type=F32)
        s = s + jnp.dot(q_rope, krt, preferred_element_type=F32)
        m_new = jnp.maximum(m_prev, jnp.max(s, axis=-1, keepdims=True))
        alpha = jnp.exp2(m_prev - m_new)
        p = jnp.exp2(s - m_new)
        l_prev = alpha * l_prev + jnp.sum(p, axis=-1, keepdims=True)
        acc = alpha * acc + jnp.dot(p.astype(BF16), ck, preferred_element_type=F32)
        m_prev = m_new

    cn = cn_ref[0]
    knew = jnp.concatenate([cn.astype(BF16), kn_ref[0]], axis=1).astype(F32)
    s_new = jnp.sum(q.astype(F32) * knew, axis=-1, keepdims=True)
    m_new = jnp.maximum(m_prev, s_new)
    alpha = jnp.exp2(m_prev - m_new)
    p = jnp.exp2(s_new - m_new)
    l_new = alpha * l_prev + p
    acc = alpha * acc + _bf16_round(p) * _bf16_round(cn)
    o_ref[0] = (acc / l_new).astype(BF16)


def _mla_decode(page_table, qcat, ckv_new, krp_new, cache_ckv, cache_krope_t, layer):
    nb, npages = page_table.shape
    nch = npages // PAGES_PER_CHUNK
    assert npages % PAGES_PER_CHUNK == 0 and nch % 2 == 0
    grid_spec = pltpu.PrefetchScalarGridSpec(
        num_scalar_prefetch=1,
        grid=(nb,),
        in_specs=[
            pl.BlockSpec((1, MLA_HEADS, Q_CAT), lambda i, pt: (i, 0, 0)),
            pl.BlockSpec((1, 1, KV_LORA), lambda i, pt: (i, 0, 0)),
            pl.BlockSpec((1, 1, LANES), lambda i, pt: (i, 0, 0)),
            pl.BlockSpec(memory_space=pl.ANY),
            pl.BlockSpec(memory_space=pl.ANY),
        ],
        out_specs=pl.BlockSpec((1, MLA_HEADS, KV_LORA), lambda i, pt: (i, 0, 0)),
        scratch_shapes=[
            pltpu.VMEM((2, PAGES_PER_CHUNK, PAGE_SIZE, KV_LORA), F32),
            pltpu.VMEM((2, PAGES_PER_CHUNK, QK_ROPE, PAGE_SIZE), F32),
            pltpu.SemaphoreType.DMA((2,)),
        ],
    )
    return pl.pallas_call(
        functools.partial(_mla_decode_kernel, layer=layer, nch=nch),
        grid_spec=grid_spec,
        out_shape=jax.ShapeDtypeStruct((nb, MLA_HEADS, KV_LORA), BF16),
        compiler_params=_cparams(("arbitrary",)),
        name="mla_decode",
    )(page_table, qcat, ckv_new.reshape(nb, 1, KV_LORA), krp_new.reshape(nb, 1, LANES),
      cache_ckv, cache_krope_t)


def _rope_angles(pos, half):
    inv = ROPE_THETA ** (-jnp.arange(half, dtype=F32) / half)
    ang = pos.astype(F32)[:, None] * inv[None, :]
    return jnp.cos(ang), jnp.sin(ang)


def _ret_tables(pos):
    cos, sin = _rope_angles(pos, RET_DK // 2)
    return jnp.concatenate([cos, cos], axis=1), jnp.concatenate([-sin, sin], axis=1)


def _mla_tables(pos, rows):
    half = QK_ROPE // 2
    cos, sin = _rope_angles(pos, half)
    n = pos.shape[0]
    c = jnp.concatenate([cos, cos, jnp.ones((n, LANES - QK_ROPE), F32)], axis=1)
    s1 = jnp.concatenate([-sin, jnp.zeros((n, LANES - half), F32)], axis=1)
    s2 = jnp.concatenate([jnp.zeros((n, half), F32), sin, jnp.zeros((n, LANES - QK_ROPE), F32)], axis=1)
    return tuple(jnp.broadcast_to(t, (rows, LANES)) if n == 1 else t for t in (c, s1, s2))


def _pad_lanes(v):
    return jnp.pad(v.astype(F32), (0, LANES - v.shape[0])).reshape(1, LANES)


def _hyb_w_in_layout(w):
    d = w.shape[0]
    qk = 2 * RET_HEADS * RET_DK
    vg = 2 * RET_HEADS * RET_DV
    q_k = w[:, :qk]
    v_g = w[:, qk:qk + vg]
    z = w[:, qk + vg:qk + vg + SSM_D_INNER]
    xbc = w[:, qk + vg + SSM_D_INNER:qk + vg + SSM_D_INNER + CONV_DIM]
    dt = w[:, qk + vg + SSM_D_INNER + CONV_DIM:]
    pieces = [xbc, dt, jnp.zeros((d, OFF_Q - OFF_DT - SSM_HEADS), w.dtype), q_k,
              jnp.zeros((d, OFF_V - OFF_K - RET_HEADS * RET_DK), w.dtype), v_g, z]
    out = jnp.concatenate(pieces, axis=1)
    assert out.shape[1] == HYB_IN_PAD
    return out.astype(BF16)


def _mla_wqt_layout(w_uq):
    k = w_uq.shape[0]
    w4 = w_uq.reshape(k, MLA_HEADS // 2, 2, QK_NOPE + QK_ROPE)
    nope = w4[..., :QK_NOPE].reshape(k, MLA_HEADS // 2, 2 * QK_NOPE)
    ropew = w4[..., QK_NOPE:].reshape(k, MLA_HEADS // 2, 2 * QK_ROPE)
    return jnp.concatenate([nope, ropew], -1).reshape(k, MLA_HEADS // 2 * QT_PAIR).T.astype(BF16)


def _mla_wq_layout(w_uq):
    k = w_uq.shape[0]
    w3 = w_uq.reshape(k, MLA_HEADS, QK_NOPE + QK_ROPE)
    nope, ropew = w3[:, :, :QK_NOPE], w3[:, :, QK_NOPE:]
    z = jnp.zeros_like(nope)
    even = (jnp.arange(MLA_HEADS) % 2 == 0)[None, :, None]
    first = jnp.where(even, jnp.concatenate([nope, z], -1), jnp.concatenate([z, nope], -1))
    second = jnp.concatenate([ropew, jnp.zeros((k, MLA_HEADS, LANES - QK_ROPE), w_uq.dtype)], -1)
    return jnp.concatenate([first, second], -1).reshape(k, MLA_HEADS * Q_HEAD_PAD).astype(BF16)


def kernel(x_prompt, x_sample, state_ret, state_ssm, state_conv, cache_ckv, cache_krope, page_table,
           norm_ffn1, ffn1_w_in, ffn1_w_out, norm_mix, norm_ffn2, ffn2_w_in, ffn2_w_out,
           hyb_w_in, hyb_w_out, hyb_conv_w, hyb_conv_b, hyb_dt_bias, hyb_a_log, hyb_d_skip,
           hyb_norm_w, mla_w_in, mla_q_norm_w, mla_kv_norm_w, mla_w_uq, mla_w_uk, mla_w_uv,
           mla_w_o, final_norm_w):
    bp, sp, d = x_prompt.shape
    bs, ss, _ = x_sample.shape
    assert ss == 1 and sp % CHUNK == 0
    depth = norm_ffn1.shape[0]
    mp = bp * sp
    xp = x_prompt.reshape(mp, d)
    xs = x_sample.reshape(bs, d)
    pos_p = jnp.arange(sp)
    pos_s = PAST_LEN + jnp.arange(1)
    fw = final_norm_w.reshape(1, d)

    ret_cos_p, ret_sin_p = _ret_tables(pos_p)
    ret_cos_s, ret_sin_s = _ret_tables(pos_s)
    mla_tabs_p = _mla_tables(pos_p, sp)
    mla_tabs_s = _mla_tables(pos_s, bs)
    mla_cos_t, mla_sin_t = (t.T for t in _rope_angles(pos_p, QK_ROPE // 2))
    log_gamma = jnp.log1p(-jnp.exp2(-5.0 - jnp.arange(RET_HEADS, dtype=F32)))
    lg_rows = jnp.broadcast_to(log_gamma[:, None], (RET_HEADS, LANES))

    outs = {k: [] for k in ("ret_p", "ret_s", "ssm_p", "ssm_s", "conv_p", "conv_s",
                            "ckv_p", "ckv_s", "kr_p", "kr_s")}
    for layer in range(depth):
        j = layer // 2
        last = layer == depth - 1
        w1i, w1o = ffn1_w_in[layer].astype(BF16), ffn1_w_out[layer].astype(BF16)
        w2i, w2o = ffn2_w_in[layer].astype(BF16), ffn2_w_out[layer].astype(BF16)
        n1 = norm_ffn1[layer].reshape(1, d)
        nm = norm_mix[layer].reshape(1, d)
        n2 = norm_ffn2[layer].reshape(1, d)
        xp = _ffn(xp, n1, w1i, w1o, fw)
        xs = _ffn(xs, n1, w1i, w1o, fw)
        if layer % 2 == 0:
            w_in = _hyb_w_in_layout(hyb_w_in[j])
            w_out = hyb_w_out[j].astype(BF16)
            conv_w = hyb_conv_w[j].astype(F32)
            conv_b = hyb_conv_b[j].reshape(1, CONV_DIM).astype(F32)
            dtb_row = _pad_lanes(hyb_dt_bias[j])
            a_row = _pad_lanes(-jnp.exp(hyb_a_log[j].astype(F32)))
            dsk_row = _pad_lanes(hyb_d_skip[j])
            gnw = hyb_norm_w[j].reshape(1, SSM_D_INNER).astype(F32)
            proj = _norm_matmul(xp, nm, w_in).reshape(bp, sp, HYB_IN_PAD)
            o_ret, r_p = _retention_prompt(proj, ret_cos_p, ret_sin_p, lg_rows.reshape(RET_HEADS, 1, LANES))
            dsk_x = jnp.repeat(hyb_d_skip[j].astype(F32), SSM_HEAD_DIM).reshape(1, SSM_D_INNER)
            o_ssd, s_p, c_p = _ssd_prompt(proj, conv_w, conv_b, dtb_row, a_row, dsk_x, gnw)
            xp = _matmul_res([o_ret.reshape(mp, -1), o_ssd.reshape(mp, -1)], w_out, xp)
            proj_s = _norm_matmul(xs, nm, w_in)
            mixed_s, r_s, s_s, c_s = _hybrid_decode(
                proj_s, state_ret[j], state_ssm[j], state_conv[j], ret_cos_s, ret_sin_s, lg_rows,
                conv_w, conv_b, dtb_row, a_row, dsk_row, gnw)
            xs = _matmul_res([mixed_s.reshape(bs, HYB_MIX)], w_out, xs)
            outs["ret_p"].append(r_p); outs["ret_s"].append(r_s)
            outs["ssm_p"].append(s_p); outs["ssm_s"].append(s_s)
            outs["conv_p"].append(c_p); outs["conv_s"].append(c_s)
        else:
            w_in = jnp.pad(mla_w_in[j], ((0, 0), (0, MLA_IN_PAD - MLA_IN))).astype(BF16)
            qnw = mla_q_norm_w[j].reshape(1, Q_LORA)
            kvnw = mla_kv_norm_w[j].reshape(1, KV_LORA)
            wq = _mla_wq_layout(mla_w_uq[j])
            w_uk2 = mla_w_uk[j].reshape(KV_LORA, MLA_HEADS * QK_NOPE)
            w_uv2 = mla_w_uv[j].reshape(KV_LORA, MLA_HEADS * V_DIM)
            krope_t = jnp.swapaxes(cache_krope, 2, 3)
            w_o = mla_w_o[j].astype(BF16)
            cq, ckv, kr, krp = _mla_in(xp, nm, w_in, qnw, kvnw, mla_tabs_p)
            qt = _mla_qt(cq, _mla_wqt_layout(mla_w_uq[j]), mla_cos_t, mla_sin_t, bp, sp)
            kn, vt = _mla_kv(ckv, w_uk2.astype(BF16), w_uv2.T.astype(BF16), bp, sp)
            o = _flash_prompt(qt, kn.reshape(bp, sp, -1), krp.reshape(bp, sp, LANES), vt)
            xp = _matmul_res([o.reshape(mp, MLA_HEADS * V_DIM)], w_o, xp)
            outs["ckv_p"].append(ckv.reshape(bp, sp, KV_LORA))
            outs["kr_p"].append(kr.reshape(bp, sp, QK_ROPE))
            cq_s, ckv_s, kr_s, krp_s = _mla_in(xs, nm, w_in, qnw, kvnw, mla_tabs_s)
            qp_s = _mla_q(cq_s, wq, mla_tabs_s)
            qcat = jnp.transpose(_qlat(qp_s, w_uk2.T.astype(BF16)), (1, 0, 2))
            o_lat = _mla_decode(page_table, qcat, ckv_s, krp_s, cache_ckv, krope_t, j)
            o_s = _olat(o_lat.reshape(bs, MLA_HEADS * KV_LORA), w_uv2.astype(BF16))
            xs = _matmul_res([o_s], w_o, xs)
            outs["ckv_s"].append(ckv_s.reshape(bs, 1, KV_LORA))
            outs["kr_s"].append(kr_s.reshape(bs, 1, QK_ROPE))
        xp = _ffn(xp, n2, w2i, w2o, fw, final_norm=last)
        xs = _ffn(xs, n2, w2i, w2o, fw, final_norm=last)
    if depth == 0:
        raise ValueError("depth must be positive")
    return (xp.reshape(bp, sp, d), xs.reshape(bs, 1, d),
            jnp.stack(outs["ret_p"]), jnp.stack(outs["ret_s"]),
            jnp.stack(outs["ssm_p"]), jnp.stack(outs["ssm_s"]),
            jnp.stack(outs["conv_p"]), jnp.stack(outs["conv_s"]),
            jnp.stack(outs["ckv_p"]), jnp.stack(outs["ckv_s"]),
            jnp.stack(outs["kr_p"]), jnp.stack(outs["kr_s"]))
```

```python
import functools
import math

import jax
import jax.numpy as jnp
from jax import lax
from jax.experimental import pallas as pl
from jax.experimental.pallas import tpu as pltpu

F32 = jnp.float32
BF16 = jnp.bfloat16

D_MODEL = 1024
D_FF = 2816
RMS_EPS = 1e-6
ROPE_THETA = 10000.0
CHUNK = 128
RET_HEADS = 4
RET_DK = 128
RET_DV = 256
SSM_HEADS = 16
SSM_HEAD_DIM = 64
SSM_D_INNER = 1024
SSM_STATE = 64
SSM_GROUPS = 2
CONV_W = 4
CONV_DIM = 1280
HYB_IN = 5392
HYB_IN_PAD = 5632
HYB_MIX = 2048
MLA_HEADS = 16
Q_LORA = 512
KV_LORA = 256
QK_NOPE = 64
QK_ROPE = 32
V_DIM = 64
MLA_IN = 800
MLA_IN_PAD = 896
MLA_SCALE = (QK_NOPE + QK_ROPE) ** -0.5
Q_SCALE = MLA_SCALE * math.log2(math.e)
PAST_LEN = 16384
PAGE_SIZE = 128

OFF_XBC = 0
OFF_DT = 1280
OFF_Q = 1408
OFF_K = 1920
OFF_V = 2560
OFF_G = 3584
OFF_Z = 4608

LANES = 128
SUBLANES = 8
VMEM_LIMIT_BYTES = 56 * 1024 * 1024

NT_DIMS = (((1,), (1,)), ((), ()))


def _cparams(sem):
    return pltpu.CompilerParams(dimension_semantics=sem, vmem_limit_bytes=VMEM_LIMIT_BYTES)


def _silu(x):
    return x * (0.5 * jnp.tanh(0.5 * x) + 0.5)


def _rms(x, w):
    return x * lax.rsqrt(jnp.mean(x * x, axis=-1, keepdims=True) + RMS_EPS) * w


def _bf16_round(x):
    return x.astype(BF16).astype(F32)


def _pick_tile(m, pref):
    t = min(m, pref)
    while m % t:
        t //= 2
    return t


def _cast_kernel(w_ref, o_ref):
    o_ref[...] = w_ref[0].astype(BF16)


def _cast_layer(w, layer):
    _, k, n = w.shape
    tr = _pick_tile(k, 256)
    return pl.pallas_call(
        _cast_kernel,
        grid=(k // tr,),
        in_specs=[pl.BlockSpec((1, tr, n), lambda i: (layer, i, 0))],
        out_specs=pl.BlockSpec((tr, n), lambda i: (i, 0)),
        out_shape=jax.ShapeDtypeStruct((k, n), BF16),
        compiler_params=_cparams(("parallel",)),
        name="cast_layer",
    )(w)


def _ffn_kernel(x_ref, nw_ref, wi_ref, wo_ref, fw_ref, o_ref, *, final_norm):
    x = x_ref[...]
    h = _rms(x, nw_ref[...]).astype(BF16)
    g = jnp.dot(h, wi_ref[:, :D_FF], preferred_element_type=F32)
    u = jnp.dot(h, wi_ref[:, D_FF:], preferred_element_type=F32)
    a = (_silu(g) * u).astype(BF16)
    y = x + 0.5 * jnp.dot(a, wo_ref[...], preferred_element_type=F32)
    if final_norm:
        y = _rms(y, fw_ref[...])
    o_ref[...] = y


def _ffn(x, nw, w_in, w_out, fw, *, final_norm=False):
    m, d = x.shape
    tm = _pick_tile(m, 512)
    once = pl.Buffered(1)
    return pl.pallas_call(
        functools.partial(_ffn_kernel, final_norm=final_norm),
        grid=(m // tm,),
        in_specs=[
            pl.BlockSpec((tm, d), lambda i: (i, 0)),
            pl.BlockSpec((1, d), lambda i: (0, 0)),
            pl.BlockSpec((d, 2 * D_FF), lambda i: (0, 0), pipeline_mode=once),
            pl.BlockSpec((D_FF, d), lambda i: (0, 0), pipeline_mode=once),
            pl.BlockSpec((1, d), lambda i: (0, 0)),
        ],
        out_specs=pl.BlockSpec((tm, d), lambda i: (i, 0)),
        out_shape=jax.ShapeDtypeStruct((m, d), F32),
        compiler_params=_cparams(("parallel",)),
        name="ffn",
    )(x, nw, w_in, w_out, fw)


def _norm_matmul_kernel(x_ref, nw_ref, w_ref, o_ref):
    h = _rms(x_ref[...], nw_ref[...]).astype(BF16)
    o_ref[...] = jnp.dot(h, w_ref[...], preferred_element_type=F32)


def _norm_matmul(x, nw, w):
    m, d = x.shape
    n = w.shape[1]
    tm = _pick_tile(m, 512)
    return pl.pallas_call(
        _norm_matmul_kernel,
        grid=(m // tm,),
        in_specs=[
            pl.BlockSpec((tm, d), lambda i: (i, 0)),
            pl.BlockSpec((1, d), lambda i: (0, 0)),
            pl.BlockSpec((d, n), lambda i: (0, 0), pipeline_mode=pl.Buffered(1)),
        ],
        out_specs=pl.BlockSpec((tm, n), lambda i: (i, 0)),
        out_shape=jax.ShapeDtypeStruct((m, n), F32),
        compiler_params=_cparams(("parallel",)),
        name="norm_matmul",
    )(x, nw, w)


def _matmul_res_kernel(*refs, nparts):
    a_refs, w_refs, r_ref, o_ref = refs[:nparts], refs[nparts:2 * nparts], refs[2 * nparts], refs[2 * nparts + 1]
    acc = r_ref[...]
    for a_ref, w_ref in zip(a_refs, w_refs):
        acc = acc + jnp.dot(a_ref[...].astype(BF16), w_ref[...], preferred_element_type=F32)
    o_ref[...] = acc


def _matmul_res(parts, w, res):
    m = res.shape[0]
    n = w.shape[1]
    k = parts[0].shape[1]
    assert all(a.shape == (m, k) for a in parts) and w.shape[0] == k * len(parts)
    tm = _pick_tile(m, 512)
    nparts = len(parts)
    return pl.pallas_call(
        functools.partial(_matmul_res_kernel, nparts=nparts),
        grid=(m // tm,),
        in_specs=([pl.BlockSpec((tm, k), lambda i: (i, 0)) for _ in parts]
                  + [pl.BlockSpec((k, n), functools.partial(lambda i, j: (j, 0), j=j)) for j in range(nparts)]
                  + [pl.BlockSpec((tm, n), lambda i: (i, 0))]),
        out_specs=pl.BlockSpec((tm, n), lambda i: (i, 0)),
        out_shape=jax.ShapeDtypeStruct((m, n), F32),
        compiler_params=_cparams(("parallel",)),
        name="matmul_res",
    )(*parts, *([w] * nparts), res)


def _rope_full(x, cosf, sinf):
    return x * cosf + pltpu.roll(x, RET_DK // 2, 1) * sinf


def _ret_kernel(q_ref, k_ref, v_ref, g_ref, cos_ref, sin_ref, lg_ref, o_ref, so_ref, s_ref, *, cps):
    c = pl.program_id(2)

    @pl.when(c == 0)
    def _():
        s_ref[...] = jnp.zeros_like(s_ref)

    lg = lg_ref[0]
    ii = lax.broadcasted_iota(jnp.int32, (CHUNK, CHUNK), 0)
    jj = lax.broadcasted_iota(jnp.int32, (CHUNK, CHUNK), 1)
    seg = jnp.where(ii >= jj, jnp.exp((ii - jj).astype(F32) * lg), 0.0)
    ri = lax.broadcasted_iota(jnp.int32, (CHUNK, RET_DK), 0).astype(F32)
    qdec = jnp.exp((ri + 1.0) * lg)
    kdec = jnp.exp((CHUNK - 1.0 - ri) * lg)
    cdec = jnp.exp(CHUNK * lg)[:, 0:1]

    s = s_ref[...]
    for t in range(cps):
        rows = pl.ds(t * CHUNK, CHUNK)
        cosf = cos_ref[rows, :]
        sinf = sin_ref[rows, :]
        q = _rope_full(q_ref[0, rows, :], cosf, sinf)
        k = _rope_full(k_ref[0, rows, :], cosf, sinf) * (RET_DK ** -0.5)
        vb = v_ref[0, rows, :].astype(BF16)
        sc = lax.dot_general(q.astype(BF16), k.astype(BF16), NT_DIMS, preferred_element_type=F32) * seg
        y = jnp.dot(sc.astype(BF16), vb, preferred_element_type=F32)
        y = y + jnp.dot((q * qdec).astype(BF16), s.astype(BF16), preferred_element_type=F32)
        kend_t = jnp.transpose(k * kdec).astype(BF16)
        s = cdec * s + jnp.dot(kend_t, vb, preferred_element_type=F32)
        y = y * lax.rsqrt(jnp.mean(y * y, axis=-1, keepdims=True) + RMS_EPS)
        o_ref[0, rows, :] = (y * _silu(g_ref[0, rows, :])).astype(BF16)
    s_ref[...] = s

    @pl.when(c == pl.num_programs(2) - 1)
    def _():
        so_ref[0, 0] = s


def _retention_prompt(proj, cosf, sinf, lg_rows, *, cps=8):
    b, l, _ = proj.shape
    while l % (cps * CHUNK):
        cps //= 2
    rows = cps * CHUNK
    qb = OFF_Q // RET_DK
    kb = OFF_K // RET_DK
    vb = OFF_V // RET_DV
    gb = OFF_G // RET_DV
    return pl.pallas_call(
        functools.partial(_ret_kernel, cps=cps),
        grid=(b, RET_HEADS, l // rows),
        in_specs=[
            pl.BlockSpec((1, rows, RET_DK), lambda i, h, c: (i, c, qb + h)),
            pl.BlockSpec((1, rows, RET_DK), lambda i, h, c: (i, c, kb + h)),
            pl.BlockSpec((1, rows, RET_DV), lambda i, h, c: (i, c, vb + h)),
            pl.BlockSpec((1, rows, RET_DV), lambda i, h, c: (i, c, gb + h)),
            pl.BlockSpec((rows, RET_DK), lambda i, h, c: (c, 0)),
            pl.BlockSpec((rows, RET_DK), lambda i, h, c: (c, 0)),
            pl.BlockSpec((1, 1, LANES), lambda i, h, c: (h, 0, 0)),
        ],
        out_specs=[
            pl.BlockSpec((1, rows, RET_DV), lambda i, h, c: (i, c, h)),
            pl.BlockSpec((1, 1, RET_DK, RET_DV), lambda i, h, c: (i, h, 0, 0)),
        ],
        out_shape=[
            jax.ShapeDtypeStruct((b, l, RET_HEADS * RET_DV), BF16),
            jax.ShapeDtypeStruct((b, RET_HEADS, RET_DK, RET_DV), F32),
        ],
        scratch_shapes=[pltpu.VMEM((RET_DK, RET_DV), F32)],
        compiler_params=_cparams(("parallel", "parallel", "arbitrary")),
        name="retention_prompt",
    )(proj, proj, proj, proj, cosf, sinf, lg_rows)


def _split3(x):
    hi = x.astype(BF16)
    r1 = x - hi.astype(F32)
    mid = r1.astype(BF16)
    lo = (r1 - mid.astype(F32)).astype(BF16)
    return hi, mid, lo


def _cumsum_rows(tril_bf, x):
    hi, mid, lo = _split3(x)
    out = jnp.dot(tril_bf, lo, preferred_element_type=F32)
    out = out + jnp.dot(tril_bf, mid, preferred_element_type=F32)
    return out + jnp.dot(tril_bf, hi, preferred_element_type=F32)


def _shift_rows(cur, tail, s):
    r = pltpu.roll(cur, s, 0)
    pt = pltpu.roll(tail, s, 0)
    row = lax.broadcasted_iota(jnp.int32, (SUBLANES, cur.shape[1]), 0)
    top = jnp.where(row < s, pt, r[0:SUBLANES])
    return jnp.concatenate([top, r[SUBLANES:]], axis=0)


def _softplus(x):
    return jnp.maximum(x, 0.0) + jnp.log(1.0 + jnp.exp(-jnp.abs(x)))


def _expand(x, e_bf, parts=3):
    hi, mid, lo = _split3(x)
    out = jnp.dot(mid, e_bf, preferred_element_type=F32)
    if parts == 3:
        out = jnp.dot(lo, e_bf, preferred_element_type=F32) + out
    return out + jnp.dot(hi, e_bf, preferred_element_type=F32)


def _ssd_kernel(xbc_ref, z0_ref, z1_ref, dt_ref, cw_ref, cb_ref, dtb_ref, a_ref, dskx_ref, nw_ref,
                e64_ref, e128_ref, o_ref, so_ref, co_ref, s_ref, tail_ref):
    c = pl.program_id(1)
    z_refs = (z0_ref, z1_ref)

    @pl.when(c == 0)
    def _():
        s_ref[...] = jnp.zeros_like(s_ref)
        tail_ref[...] = jnp.zeros_like(tail_ref)

    raw = xbc_ref[0]
    tail = tail_ref[...]
    acc = raw * cw_ref[CONV_W - 1:CONV_W, :] + cb_ref[...]
    for s in range(1, CONV_W):
        acc = acc + _shift_rows(raw, tail, s) * cw_ref[CONV_W - 1 - s:CONV_W - s, :]
    xbc = _silu(acc)
    tail_ref[...] = raw[CHUNK - SUBLANES:, :]

    @pl.when(c == pl.num_programs(1) - 1)
    def _():
        co_ref[0] = raw[CHUNK - (CONV_W - 1):, :]

    ii = lax.broadcasted_iota(jnp.int32, (CHUNK, CHUNK), 0)
    jj = lax.broadcasted_iota(jnp.int32, (CHUNK, CHUNK), 1)
    causal = ii >= jj
    tril_bf = jnp.where(causal, 1.0, 0.0).astype(BF16)

    dt = _softplus(dt_ref[0] + dtb_ref[...])
    la = dt * a_ref[...]
    cum = _cumsum_rows(tril_bf, la)
    cum_t = jnp.transpose(cum)

    dt_x = _expand(dt, e64_ref[...], parts=2)
    cum_x = _expand(cum, e64_ref[...])
    cc_all = _expand(cum, e128_ref[...])
    x_all = xbc[:, :SSM_D_INNER]
    xdt_all = x_all * dt_x
    ecum_x = jnp.exp(cum_x)
    dec_x = ecum_x[CHUNK - 1:CHUNK, :]

    b_all = xbc[:, SSM_D_INNER:SSM_D_INNER + LANES]
    c_all = xbc[:, SSM_D_INNER + LANES:SSM_D_INNER + 2 * LANES]
    b_all_t = jnp.transpose(b_all)
    left = lax.broadcasted_iota(jnp.int32, (CHUNK, LANES), 1) < SSM_HEAD_DIM
    top = lax.broadcasted_iota(jnp.int32, (CHUNK, LANES), 0) < SSM_STATE
    blockdiag = left == top
    pairs_per_group = SSM_HEADS // SSM_GROUPS // 2
    gw = SSM_D_INNER // SSM_GROUPS
    for g in range(SSM_GROUPS):
        bg = b_all[:, g * SSM_STATE:(g + 1) * SSM_STATE]
        cg = c_all[:, g * SSM_STATE:(g + 1) * SSM_STATE]
        bg_t = b_all_t[g * SSM_STATE:(g + 1) * SSM_STATE, :]
        cb = lax.dot_general(cg.astype(BF16), bg.astype(BF16), NT_DIMS, preferred_element_type=F32)
        cg2 = jnp.concatenate([cg, cg], axis=1)
        ys = []
        for pp in range(pairs_per_group):
            pr = g * pairs_per_group + pp
            lanes = slice(pr * LANES, (pr + 1) * LANES)
            lhs, kend = [], []
            for h in (2 * pr, 2 * pr + 1):
                cr = cum_t[h:h + 1, :]
                seg = jnp.exp(jnp.where(causal, cc_all[:, h * LANES:(h + 1) * LANES] - cr, -jnp.inf))
                lhs.append((cb * seg).astype(BF16))
                kend.append(bg_t * jnp.exp(cr[:, CHUNK - 1:CHUNK] - cr))
            xdt_p = xdt_all[:, lanes]
            rhs = jnp.concatenate([jnp.where(left, xdt_p, 0.0), jnp.where(left, 0.0, xdt_p)], axis=0)
            y = jnp.dot(jnp.concatenate(lhs, axis=1), rhs.astype(BF16), preferred_element_type=F32)
            sp = s_ref[pr]
            y = y + jnp.dot((cg2 * ecum_x[:, lanes]).astype(BF16), sp.astype(BF16), preferred_element_type=F32)
            ys.append(y + dskx_ref[:, lanes] * x_all[:, lanes])
            upd = jnp.dot(jnp.concatenate(kend, axis=0).astype(BF16), xdt_p.astype(BF16),
                          preferred_element_type=F32)
            s_ref[pr] = sp * dec_x[:, lanes] + jnp.where(blockdiag, upd, 0.0)
        yg = jnp.concatenate(ys, axis=1) * _silu(z_refs[g][0])
        yg = yg * lax.rsqrt(jnp.mean(yg * yg, axis=-1, keepdims=True) + RMS_EPS)
        o_ref[0, :, g * gw:(g + 1) * gw] = (yg * nw_ref[:, g * gw:(g + 1) * gw]).astype(BF16)

    @pl.when(c == pl.num_programs(1) - 1)
    def _():
        for pr in range(SSM_HEADS // 2):
            sp = s_ref[pr]
            so_ref[0, 2 * pr] = sp[:SSM_STATE, :SSM_HEAD_DIM]
            so_ref[0, 2 * pr + 1] = sp[SSM_STATE:, SSM_HEAD_DIM:]


def _head_expansion(width):
    col = jnp.arange(SSM_HEADS * width) // width
    return (jnp.arange(LANES)[:, None] == col[None, :]).astype(BF16)


def _ssd_prompt(proj, conv_w, conv_b, dtb_row, a_row, dsk_x, norm_w):
    b, l, _ = proj.shape
    gw = SSM_D_INNER // SSM_GROUPS
    full = lambda i, c: (0, 0)
    return pl.pallas_call(
        _ssd_kernel,
        grid=(b, l // CHUNK),
        in_specs=[
            pl.BlockSpec((1, CHUNK, CONV_DIM), lambda i, c: (i, c, OFF_XBC // CONV_DIM)),
            pl.BlockSpec((1, CHUNK, gw), lambda i, c: (i, c, OFF_Z // gw)),
            pl.BlockSpec((1, CHUNK, gw), lambda i, c: (i, c, OFF_Z // gw + 1)),
            pl.BlockSpec((1, CHUNK, LANES), lambda i, c: (i, c, OFF_DT // LANES)),
            pl.BlockSpec((CONV_W, CONV_DIM), full),
            pl.BlockSpec((1, CONV_DIM), full),
            pl.BlockSpec((1, LANES), full),
            pl.BlockSpec((1, LANES), full),
            pl.BlockSpec((1, SSM_D_INNER), full),
            pl.BlockSpec((1, SSM_D_INNER), full),
            pl.BlockSpec((LANES, SSM_HEADS * SSM_HEAD_DIM), full),
            pl.BlockSpec((LANES, SSM_HEADS * LANES), full),
        ],
        out_specs=[
            pl.BlockSpec((1, CHUNK, SSM_D_INNER), lambda i, c: (i, c, 0)),
            pl.BlockSpec((1, SSM_HEADS, SSM_STATE, SSM_HEAD_DIM), lambda i, c: (i, 0, 0, 0)),
            pl.BlockSpec((1, CONV_W - 1, CONV_DIM), lambda i, c: (i, 0, 0)),
        ],
        out_shape=[
            jax.ShapeDtypeStruct((b, l, SSM_D_INNER), BF16),
            jax.ShapeDtypeStruct((b, SSM_HEADS, SSM_STATE, SSM_HEAD_DIM), F32),
            jax.ShapeDtypeStruct((b, CONV_W - 1, CONV_DIM), F32),
        ],
        scratch_shapes=[
            pltpu.VMEM((SSM_HEADS // 2, 2 * SSM_STATE, 2 * SSM_HEAD_DIM), F32),
            pltpu.VMEM((SUBLANES, CONV_DIM), F32),
        ],
        compiler_params=_cparams(("parallel", "arbitrary")),
        name="ssd_prompt",
    )(proj, proj, proj, proj, conv_w, conv_b, dtb_row, a_row, dsk_x, norm_w,
      _head_expansion(SSM_HEAD_DIM), _head_expansion(LANES))


def _bcast_rows(x, n):
    return jnp.broadcast_to(x, (n, x.shape[1]))


def _column_matrix(row):
    return jnp.transpose(_bcast_rows(row, LANES))


def _hyb_decode_kernel(row_ref, sr_ref, ss_ref, cs_ref, cos_ref, sin_ref, lg_ref, cw_ref, cb_ref,
                       dtb_ref, a_ref, dsk_ref, nw_ref, o_ref, sro_ref, sso_ref, co_ref):
    row = row_ref[0]
    cosf = cos_ref[...]
    sinf = sin_ref[...]
    for h in range(RET_HEADS):
        q = _rope_full(_bcast_rows(row[:, OFF_Q + h * RET_DK:OFF_Q + (h + 1) * RET_DK], SUBLANES), cosf, sinf)
        k = _rope_full(_bcast_rows(row[:, OFF_K + h * RET_DK:OFF_K + (h + 1) * RET_DK], SUBLANES), cosf, sinf)
        k = k * (RET_DK ** -0.5)
        v = _bf16_round(row[:, OFF_V + h * RET_DV:OFF_V + (h + 1) * RET_DV])
        g = row[:, OFF_G + h * RET_DV:OFF_G + (h + 1) * RET_DV]
        gamma = jnp.exp(lg_ref[h:h + 1, :])
        qb = _bf16_round(q)
        kb = _bf16_round(k)
        kcol = _column_matrix(kb[0:1])
        s0 = sr_ref[0, h]
        sro_ref[0, h] = gamma[:, 0:1] * s0 + jnp.concatenate([kcol, kcol], axis=1) * v
        qcol = _column_matrix(_bf16_round(q * gamma)[0:1])
        y = jnp.sum(jnp.concatenate([qcol, qcol], axis=1) * _bf16_round(s0), axis=0, keepdims=True)
        score = jnp.sum(qb * kb, axis=-1, keepdims=True)[0:1]
        y = y + _bf16_round(score) * v
        y = y * lax.rsqrt(jnp.mean(y * y, axis=-1, keepdims=True) + RMS_EPS)
        o_ref[0, :, h * RET_DV:(h + 1) * RET_DV] = y * _silu(g)
    cs = cs_ref[0]
    raw = row[:, OFF_XBC:OFF_XBC + CONV_DIM]
    acc = raw * cw_ref[CONV_W - 1:CONV_W, :] + cb_ref[...]
    for w in range(CONV_W - 1):
        acc = acc + cs[w:w + 1, :] * cw_ref[w:w + 1, :]
    xbc = _silu(acc)
    co_ref[0, 0:CONV_W - 2, :] = cs[1:CONV_W - 1, :]
    co_ref[0, CONV_W - 2:CONV_W - 1, :] = raw
    dt = _softplus(row[:, OFF_DT:OFF_DT + LANES] + dtb_ref[...])
    la = dt * a_ref[...]
    dec = jnp.exp(la)
    b_all = _bf16_round(xbc[:, SSM_D_INNER:SSM_D_INNER + LANES])
    c_all = xbc[:, SSM_D_INNER + LANES:SSM_D_INNER + 2 * LANES]
    bcol = _column_matrix(b_all)
    ccol = _column_matrix(_bf16_round(c_all))
    rep = SSM_HEADS // SSM_GROUPS
    ys = []
    for h in range(SSM_HEADS):
        g = h // rep
        s0 = ss_ref[0, h]
        xh = xbc[:, h * SSM_HEAD_DIM:(h + 1) * SSM_HEAD_DIM]
        xdt = _bf16_round(xh * dt[:, h:h + 1])
        dech = dec[:, h:h + 1]
        bg = b_all[:, g * SSM_STATE:(g + 1) * SSM_STATE]
        cg = c_all[:, g * SSM_STATE:(g + 1) * SSM_STATE]
        sso_ref[0, h] = dech * s0 + bcol[g * SSM_STATE:(g + 1) * SSM_STATE, 0:SSM_HEAD_DIM] * xdt
        y = dech * jnp.sum(ccol[g * SSM_STATE:(g + 1) * SSM_STATE, 0:SSM_HEAD_DIM] * _bf16_round(s0),
                           axis=0, keepdims=True)
        score = jnp.sum(_bf16_round(cg) * bg, axis=-1, keepdims=True)
        ys.append(y + _bf16_round(score) * xdt + dsk_ref[:, h:h + 1] * xh)
    gw = rep * SSM_HEAD_DIM
    for g in range(SSM_GROUPS):
        yg = jnp.concatenate(ys[g * rep:(g + 1) * rep], axis=1)
        yg = yg * _silu(row[:, OFF_Z + g * gw:OFF_Z + (g + 1) * gw])
        yg = yg * lax.rsqrt(jnp.mean(yg * yg, axis=-1, keepdims=True) + RMS_EPS)
        lo = RET_HEADS * RET_DV + g * gw
        o_ref[0, :, lo:lo + gw] = yg * nw_ref[:, g * gw:(g + 1) * gw]


def _hybrid_decode(proj_s, state_ret, state_ssm, state_conv, cos_row, sin_row, lg_rows, conv_w, conv_b,
                   dtb_row, a_row, dsk_row, norm_w):
    nb = proj_s.shape[0]
    full = lambda i: (0, 0)
    return pl.pallas_call(
        _hyb_decode_kernel,
        grid=(nb,),
        in_specs=[
            pl.BlockSpec((1, 1, HYB_IN_PAD), lambda i: (i, 0, 0)),
            pl.BlockSpec((1, RET_HEADS, RET_DK, RET_DV), lambda i: (i, 0, 0, 0)),
            pl.BlockSpec((1, SSM_HEADS, SSM_STATE, SSM_HEAD_DIM), lambda i: (i, 0, 0, 0)),
            pl.BlockSpec((1, CONV_W - 1, CONV_DIM), lambda i: (i, 0, 0)),
            pl.BlockSpec((1, RET_DK), full),
            pl.BlockSpec((1, RET_DK), full),
            pl.BlockSpec((RET_HEADS, LANES), full),
            pl.BlockSpec((CONV_W, CONV_DIM), full),
            pl.BlockSpec((1, CONV_DIM), full),
            pl.BlockSpec((1, LANES), full),
            pl.BlockSpec((1, LANES), full),
            pl.BlockSpec((1, LANES), full),
            pl.BlockSpec((1, SSM_D_INNER), full),
        ],
        out_specs=[
            pl.BlockSpec((1, 1, HYB_MIX), lambda i: (i, 0, 0)),
            pl.BlockSpec((1, RET_HEADS, RET_DK, RET_DV), lambda i: (i, 0, 0, 0)),
            pl.BlockSpec((1, SSM_HEADS, SSM_STATE, SSM_HEAD_DIM), lambda i: (i, 0, 0, 0)),
            pl.BlockSpec((1, CONV_W - 1, CONV_DIM), lambda i: (i, 0, 0)),
        ],
        out_shape=[
            jax.ShapeDtypeStruct((nb, 1, HYB_MIX), F32),
            jax.ShapeDtypeStruct(state_ret.shape, F32),
            jax.ShapeDtypeStruct(state_ssm.shape, F32),
            jax.ShapeDtypeStruct(state_conv.shape, F32),
        ],
        compiler_params=_cparams(("parallel",)),
        name="hybrid_decode",
    )(proj_s.reshape(nb, 1, HYB_IN_PAD), state_ret, state_ssm, state_conv, cos_row, sin_row, lg_rows,
      conv_w, conv_b, dtb_row, a_row, dsk_row, norm_w)


def _rope_group(x, c, s1, s2):
    half = QK_ROPE // 2
    return x * c + pltpu.roll(x, LANES - half, 1) * s1 + pltpu.roll(x, half, 1) * s2


def _mla_in_kernel(x_ref, nw_ref, w_ref, qnw_ref, kvnw_ref, c_ref, s1_ref, s2_ref,
                   cq_ref, ckv_ref, kr_ref, krp_ref):
    h = _rms(x_ref[...], nw_ref[...]).astype(BF16)
    p = jnp.dot(h, w_ref[...], preferred_element_type=F32)
    cq_ref[...] = _rms(p[:, :Q_LORA], qnw_ref[...]).astype(BF16)
    ckv_ref[...] = _rms(p[:, Q_LORA:Q_LORA + KV_LORA], kvnw_ref[...])
    kr = _rope_group(p[:, Q_LORA + KV_LORA:], c_ref[...], s1_ref[...], s2_ref[...])
    kr_ref[...] = kr[:, :QK_ROPE]
    krp_ref[...] = kr.astype(BF16)


def _mla_in(x, nw, w, qnw, kvnw, tabs):
    m, d = x.shape
    tm = _pick_tile(m, 512)
    nt = tabs[0].shape[0] // tm
    full = lambda i: (0, 0)
    tab = pl.BlockSpec((tm, LANES), lambda i: (i % nt, 0))
    return pl.pallas_call(
        _mla_in_kernel,
        grid=(m // tm,),
        in_specs=[
            pl.BlockSpec((tm, d), lambda i: (i, 0)),
            pl.BlockSpec((1, d), full),
            pl.BlockSpec((d, MLA_IN_PAD), full),
            pl.BlockSpec((1, Q_LORA), full),
            pl.BlockSpec((1, KV_LORA), full),
            tab, tab, tab,
        ],
        out_specs=[
            pl.BlockSpec((tm, Q_LORA), lambda i: (i, 0)),
            pl.BlockSpec((tm, KV_LORA), lambda i: (i, 0)),
            pl.BlockSpec((tm, QK_ROPE), lambda i: (i, 0)),
            pl.BlockSpec((tm, LANES), lambda i: (i, 0)),
        ],
        out_shape=[
            jax.ShapeDtypeStruct((m, Q_LORA), BF16),
            jax.ShapeDtypeStruct((m, KV_LORA), F32),
            jax.ShapeDtypeStruct((m, QK_ROPE), F32),
            jax.ShapeDtypeStruct((m, LANES), BF16),
        ],
        compiler_params=_cparams(("parallel",)),
        name="mla_in",
    )(x, nw, w, qnw, kvnw, *tabs)


Q_HEAD_PAD = 2 * LANES
Q_TN = 2 * Q_HEAD_PAD


def _mla_q_kernel(cq_ref, w_ref, c_ref, s1_ref, s2_ref, o_ref):
    p = jnp.dot(cq_ref[...], w_ref[...], preferred_element_type=F32)
    for gi in range(Q_TN // LANES):
        x = p[:, gi * LANES:(gi + 1) * LANES]
        if gi % 2 == 1:
            x = _rope_group(x, c_ref[...], s1_ref[...], s2_ref[...])
        o_ref[:, gi * LANES:(gi + 1) * LANES] = (x * Q_SCALE).astype(BF16)


def _mla_q(cq, wq, tabs):
    m, k = cq.shape
    n = wq.shape[1]
    tm = _pick_tile(m, 512)
    nt = tabs[0].shape[0] // tm
    tab = pl.BlockSpec((tm, LANES), lambda i, j: (i % nt, 0))
    return pl.pallas_call(
        _mla_q_kernel,
        grid=(m // tm, n // Q_TN),
        in_specs=[
            pl.BlockSpec((tm, k), lambda i, j: (i, 0)),
            pl.BlockSpec((k, Q_TN), lambda i, j: (0, j)),
            tab, tab, tab,
        ],
        out_specs=pl.BlockSpec((tm, Q_TN), lambda i, j: (i, j)),
        out_shape=jax.ShapeDtypeStruct((m, n), BF16),
        compiler_params=_cparams(("parallel", "arbitrary")),
        name="mla_q",
    )(cq, wq, *tabs)


QT_PAIR = 2 * QK_NOPE + 2 * QK_ROPE


def _mla_qt_kernel(cq_ref, wt_ref, cos_ref, sin_ref, o_ref):
    p = lax.dot_general(wt_ref[...], cq_ref[...], NT_DIMS, preferred_element_type=F32)
    cos = cos_ref[...]
    sin = sin_ref[...]
    half = QK_ROPE // 2
    for pr in range(MLA_HEADS // 2):
        base = pr * QT_PAIR
        o_ref[0, base:base + 2 * QK_NOPE, :] = (p[base:base + 2 * QK_NOPE] * Q_SCALE).astype(BF16)
        for hh in range(2):
            r0 = base + 2 * QK_NOPE + hh * QK_ROPE
            x1 = p[r0:r0 + half]
            x2 = p[r0 + half:r0 + QK_ROPE]
            o_ref[0, r0:r0 + half, :] = ((x1 * cos - x2 * sin) * Q_SCALE).astype(BF16)
            o_ref[0, r0 + half:r0 + QK_ROPE, :] = ((x1 * sin + x2 * cos) * Q_SCALE).astype(BF16)


def _mla_qt(cq, wq_t, cos_t, sin_t, b, l):
    m, k = cq.shape
    n = wq_t.shape[0]
    tm = _pick_tile(l, 512)
    nt = l // tm
    half = QK_ROPE // 2
    return pl.pallas_call(
        _mla_qt_kernel,
        grid=(m // tm,),
        in_specs=[
            pl.BlockSpec((tm, k), lambda i: (i, 0)),
            pl.BlockSpec((n, k), lambda i: (0, 0)),
            pl.BlockSpec((half, tm), lambda i: (0, i % nt)),
            pl.BlockSpec((half, tm), lambda i: (0, i % nt)),
        ],
        out_specs=pl.BlockSpec((1, n, tm), lambda i: (i // nt, 0, i % nt)),
        out_shape=jax.ShapeDtypeStruct((b, n, l), BF16),
        compiler_params=_cparams(("parallel",)),
        name="mla_qt",
    )(cq, wq_t, cos_t, sin_t)


def _mla_kv_kernel(c_ref, wk_ref, wvt_ref, k_ref, vt_ref):
    cb = c_ref[...].astype(BF16)
    k_ref[...] = jnp.dot(cb, wk_ref[...], preferred_element_type=F32).astype(BF16)
    vt_ref[0] = lax.dot_general(wvt_ref[...], cb, NT_DIMS, preferred_element_type=F32).astype(BF16)


def _mla_kv(ckv, w_uk, w_uv_t, b, l):
    m, k = ckv.shape
    n = w_uk.shape[1]
    tm = _pick_tile(l, 512)
    nt = l // tm
    return pl.pallas_call(
        _mla_kv_kernel,
        grid=(m // tm,),
        in_specs=[
            pl.BlockSpec((tm, k), lambda i: (i, 0)),
            pl.BlockSpec((k, n), lambda i: (0, 0)),
            pl.BlockSpec((n, k), lambda i: (0, 0)),
        ],
        out_specs=[
            pl.BlockSpec((tm, n), lambda i: (i, 0)),
            pl.BlockSpec((1, n, tm), lambda i: (i // nt, 0, i % nt)),
        ],
        out_shape=[
            jax.ShapeDtypeStruct((m, n), BF16),
            jax.ShapeDtypeStruct((b, n, l), BF16),
        ],
        compiler_params=_cparams(("parallel",)),
        name="mla_kv",
    )(ckv, w_uk, w_uv_t)


FLASH_PAIRS = 4


def _flash_kernel(qi_ref, ki_ref, q_ref, kn_ref, kr_ref, vt_ref, o_ref, m_ref, l_ref, acc_ref, *, t):
    step = pl.program_id(2)
    qi = qi_ref[step]
    ki = ki_ref[step]
    nheads = 2 * FLASH_PAIRS

    @pl.when(ki == 0)
    def _():
        m_ref[...] = jnp.full_like(m_ref, -jnp.inf)
        l_ref[...] = jnp.zeros_like(l_ref)
        acc_ref[...] = jnp.zeros_like(acc_ref)

    def update(masked):
        zn = jnp.zeros((QK_NOPE, t), BF16)
        zt = jnp.zeros((LANES - QK_ROPE, t), BF16)
        rope0 = 2 * QK_NOPE
        kr = kr_ref[0]
        sts = []
        for pi in range(FLASH_PAIRS):
            kcat = jnp.concatenate([kn_ref[0, :, pi * LANES:(pi + 1) * LANES], kr], axis=1)
            qp = q_ref[0, pi * QT_PAIR:(pi + 1) * QT_PAIR, :]
            qts = [jnp.concatenate([qp[0:QK_NOPE], zn, qp[rope0:rope0 + QK_ROPE], zt], axis=0),
                   jnp.concatenate([zn, qp[QK_NOPE:rope0], qp[rope0 + QK_ROPE:], zt], axis=0)]
            sts += [jnp.dot(kcat, qts[hh], preferred_element_type=F32) for hh in range(2)]
        for hh in range(nheads):
            rows = pl.ds(hh * V_DIM, V_DIM)
            st = sts[hh]
            if masked:
                key = lax.broadcasted_iota(jnp.int32, (t, t), 0)
                qry = lax.broadcasted_iota(jnp.int32, (t, t), 1)
                st = jnp.where(key <= qry, st, -jnp.inf)
            m_prev = m_ref[hh]
            m_new = jnp.maximum(m_prev, jnp.max(st, axis=0, keepdims=True))
            alpha = jnp.exp2(m_prev - m_new)
            p = jnp.exp2(st - m_new)
            l_ref[hh] = alpha * l_ref[hh] + jnp.sum(p, axis=0, keepdims=True)
            m_ref[hh] = m_new
            pv = jnp.dot(vt_ref[0, rows, :], p.astype(BF16), preferred_element_type=F32)
            acc_ref[rows, :] = alpha * acc_ref[rows, :] + pv

    @pl.when(ki < qi)
    def _():
        update(False)

    @pl.when(ki == qi)
    def _():
        update(True)
        inv = jnp.concatenate([jnp.broadcast_to(1.0 / l_ref[hh], (V_DIM, t)) for hh in range(nheads)], axis=0)
        o_ref[0] = jnp.transpose(acc_ref[...] * inv).astype(BF16)


def _flash_prompt(qt, kn, krp, vt, *, t=512):
    b, l, _ = kn.shape
    n = l // t
    qi_tab = jnp.asarray([qi for qi in range(n) for _ in range(qi + 1)], jnp.int32)
    ki_tab = jnp.asarray([ki for qi in range(n) for ki in range(qi + 1)], jnp.int32)
    nsteps = MLA_HEADS // (2 * FLASH_PAIRS)
    hw = 2 * FLASH_PAIRS * V_DIM
    grid_spec = pltpu.PrefetchScalarGridSpec(
        num_scalar_prefetch=2,
        grid=(b, nsteps, int(qi_tab.shape[0])),
        in_specs=[
            pl.BlockSpec((1, FLASH_PAIRS * QT_PAIR, t), lambda i, p, s, qt, kt: (i, p, qt[s])),
            pl.BlockSpec((1, t, FLASH_PAIRS * LANES), lambda i, p, s, qt, kt: (i, kt[s], p)),
            pl.BlockSpec((1, t, LANES), lambda i, p, s, qt, kt: (i, kt[s], 0)),
            pl.BlockSpec((1, hw, t), lambda i, p, s, qt, kt: (i, p, kt[s])),
        ],
        out_specs=pl.BlockSpec((1, t, hw), lambda i, p, s, qt, kt: (i, qt[s], p)),
        scratch_shapes=[
            pltpu.VMEM((2 * FLASH_PAIRS, 1, t), F32),
            pltpu.VMEM((2 * FLASH_PAIRS, 1, t), F32),
            pltpu.VMEM((hw, t), F32),
        ],
    )
    return pl.pallas_call(
        functools.partial(_flash_kernel, t=t),
        grid_spec=grid_spec,
        out_shape=jax.ShapeDtypeStruct((b, l, MLA_HEADS * V_DIM), BF16),
        compiler_params=_cparams(("parallel", "parallel", "arbitrary")),
        name="mla_flash",
    )(qi_tab, ki_tab, qt, kn, krp, vt)


Q_CAT = KV_LORA + LANES


def _qlat_kernel(q_ref, w_ref, o_ref):
    for hh in range(2):
        qh = q_ref[:, hh * Q_HEAD_PAD:(hh + 1) * Q_HEAD_PAD]
        lat = jnp.dot(qh[:, :LANES], w_ref[...], preferred_element_type=F32)
        o_ref[hh] = jnp.concatenate([lat.astype(BF16), qh[:, LANES:]], axis=1)


def _qlat(qp_s, w_uk_t):
    nb = qp_s.shape[0]
    npairs = MLA_HEADS // 2
    return pl.pallas_call(
        _qlat_kernel,
        grid=(npairs,),
        in_specs=[
            pl.BlockSpec((nb, Q_TN), lambda p: (0, p)),
            pl.BlockSpec((LANES, KV_LORA), lambda p: (p, 0)),
        ],
        out_specs=pl.BlockSpec((2, nb, Q_CAT), lambda p: (p, 0, 0)),
        out_shape=jax.ShapeDtypeStruct((MLA_HEADS, nb, Q_CAT), BF16),
        compiler_params=_cparams(("parallel",)),
        name="mla_qlat",
    )(qp_s, w_uk_t)


def _olat_kernel(o_ref, w_ref, out_ref):
    nb = o_ref.shape[0]
    first = lax.broadcasted_iota(jnp.int32, (nb, LANES), 1) < V_DIM
    r0 = jnp.dot(o_ref[:, :KV_LORA], w_ref[...], preferred_element_type=F32)
    r1 = jnp.dot(o_ref[:, KV_LORA:], w_ref[...], preferred_element_type=F32)
    out_ref[...] = jnp.where(first, r0, r1).astype(BF16)


def _olat(o_lat, w_uv):
    nb = o_lat.shape[0]
    npairs = MLA_HEADS // 2
    return pl.pallas_call(
        _olat_kernel,
        grid=(npairs,),
        in_specs=[
            pl.BlockSpec((nb, 2 * KV_LORA), lambda p: (0, p)),
            pl.BlockSpec((KV_LORA, LANES), lambda p: (0, p)),
        ],
        out_specs=pl.BlockSpec((nb, LANES), lambda p: (0, p)),
        out_shape=jax.ShapeDtypeStruct((nb, MLA_HEADS * V_DIM), BF16),
        compiler_params=_cparams(("parallel",)),
        name="mla_olat",
    )(o_lat, w_uv)


PAGES_PER_CHUNK = 32
DECODE_SUBBLOCKS = 4


def _mla_decode_kernel(pt_ref, q_ref, cn_ref, kn_ref, ckv_hbm, krt_hbm, o_ref, ckbuf, krbuf, sem,
                       *, layer, nch):
    ppc = PAGES_PER_CHUNK
    b = pl.program_id(0)
    nb = pl.num_programs(0)

    def copies(bb, c, slot):
        out = []
        for i in range(ppc):
            pg = pt_ref[bb, c * ppc + i]
            out.append(pltpu.make_async_copy(ckv_hbm.at[layer, pg], ckbuf.at[slot, i], sem.at[slot]))
            out.append(pltpu.make_async_copy(krt_hbm.at[layer, pg], krbuf.at[slot, i], sem.at[slot]))
        return out

    def start(bb, c, slot):
        for cp in copies(bb, c, slot):
            cp.start()

    @pl.when(b == 0)
    def _():
        start(0, 0, 0)

    q = q_ref[0]
    q_lat = q[:, :KV_LORA]
    q_rope = q[:, KV_LORA:KV_LORA + QK_ROPE]
    psub = ppc // DECODE_SUBBLOCKS
    m_prev = jnp.full((MLA_HEADS, 1), -jnp.inf, F32)
    l_prev = jnp.zeros((MLA_HEADS, 1), F32)
    acc = jnp.zeros((MLA_HEADS, KV_LORA), F32)
    for c in range(nch):
        slot = c % 2
        if c + 1 < nch:
            start(b, c + 1, 1 - slot)
        else:
            @pl.when(b + 1 < nb)
            def _():
                start(b + 1, 0, 1 - slot)
        for cp in copies(b, c, slot):
            cp.wait()
        cks, ss = [], []
        for u in range(DECODE_SUBBLOCKS):
            p0 = u * psub
            ck = ckbuf[slot, p0:p0 + psub].reshape(psub * PAGE_SIZE, KV_LORA).astype(BF16)
            krt = jnp.concatenate([krbuf[slot, p0 + i] for i in range(psub)], axis=1).astype(BF16)
            s = lax.dot_general(q_lat, ck, NT_DIMS, preferred_element_type=F32)
            cks.append(ck)
            ss.append(s + jnp.dot(q_rope, krt, preferred_element_type=F32))
        m_new = m_prev
        for s in ss:
            m_new = jnp.maximum(m_new, jnp.max(s, axis=-1, keepdims=True))
        alpha = jnp.exp2(m_prev - m_new)
        l_prev = alpha * l_prev
        acc = alpha * acc
        for ck, s in zip(cks, ss):
            p = jnp.exp2(s - m_new)
            l_prev = l_prev + jnp.sum(p, axis=-1, keepdims=True)
            acc = acc + jnp.dot(p.astype(BF16), ck, preferred_element_type=F32)
        m_prev = m_new

    cn = cn_ref[0]
    knew = jnp.concatenate([cn.astype(BF16), kn_ref[0]], axis=1).astype(F32)
    s_new = jnp.sum(q.astype(F32) * knew, axis=-1, keepdims=True)
    m_new = jnp.maximum(m_prev, s_new)
    alpha = jnp.exp2(m_prev - m_new)
    p = jnp.exp2(s_new - m_new)
    l_new = alpha * l_prev + p
    acc = alpha * acc + _bf16_round(p) * _bf16_round(cn)
    o_ref[0] = (acc / l_new).astype(BF16)


def _mla_decode(page_table, qcat, ckv_new, krp_new, cache_ckv, cache_krope_t, layer):
    nb, npages = page_table.shape
    nch = npages // PAGES_PER_CHUNK
    assert npages % PAGES_PER_CHUNK == 0 and nch % 2 == 0
    grid_spec = pltpu.PrefetchScalarGridSpec(
        num_scalar_prefetch=1,
        grid=(nb,),
        in_specs=[
            pl.BlockSpec((1, MLA_HEADS, Q_CAT), lambda i, pt: (i, 0, 0)),
            pl.BlockSpec((1, 1, KV_LORA), lambda i, pt: (i, 0, 0)),
            pl.BlockSpec((1, 1, LANES), lambda i, pt: (i, 0, 0)),
            pl.BlockSpec(memory_space=pl.ANY),
            pl.BlockSpec(memory_space=pl.ANY),
        ],
        out_specs=pl.BlockSpec((1, MLA_HEADS, KV_LORA), lambda i, pt: (i, 0, 0)),
        scratch_shapes=[
            pltpu.VMEM((2, PAGES_PER_CHUNK, PAGE_SIZE, KV_LORA), F32),
            pltpu.VMEM((2, PAGES_PER_CHUNK, QK_ROPE, PAGE_SIZE), F32),
            pltpu.SemaphoreType.DMA((2,)),
        ],
    )
    return pl.pallas_call(
        functools.partial(_mla_decode_kernel, layer=layer, nch=nch),
        grid_spec=grid_spec,
        out_shape=jax.ShapeDtypeStruct((nb, MLA_HEADS, KV_LORA), BF16),
        compiler_params=_cparams(("arbitrary",)),
        name="mla_decode",
    )(page_table, qcat, ckv_new.reshape(nb, 1, KV_LORA), krp_new.reshape(nb, 1, LANES),
      cache_ckv, cache_krope_t)


def _rope_angles(pos, half):
    inv = ROPE_THETA ** (-jnp.arange(half, dtype=F32) / half)
    ang = pos.astype(F32)[:, None] * inv[None, :]
    return jnp.cos(ang), jnp.sin(ang)


def _ret_tables(pos):
    cos, sin = _rope_angles(pos, RET_DK // 2)
    return jnp.concatenate([cos, cos], axis=1), jnp.concatenate([-sin, sin], axis=1)


def _mla_tables(pos, rows):
    half = QK_ROPE // 2
    cos, sin = _rope_angles(pos, half)
    n = pos.shape[0]
    c = jnp.concatenate([cos, cos, jnp.ones((n, LANES - QK_ROPE), F32)], axis=1)
    s1 = jnp.concatenate([-sin, jnp.zeros((n, LANES - half), F32)], axis=1)
    s2 = jnp.concatenate([jnp.zeros((n, half), F32), sin, jnp.zeros((n, LANES - QK_ROPE), F32)], axis=1)
    return tuple(jnp.broadcast_to(t, (rows, LANES)) if n == 1 else t for t in (c, s1, s2))


def _pad_lanes(v):
    return jnp.pad(v.astype(F32), (0, LANES - v.shape[0])).reshape(1, LANES)


def _hyb_w_in_layout(w):
    d = w.shape[0]
    qk = 2 * RET_HEADS * RET_DK
    vg = 2 * RET_HEADS * RET_DV
    q_k = w[:, :qk]
    v_g = w[:, qk:qk + vg]
    z = w[:, qk + vg:qk + vg + SSM_D_INNER]
    xbc = w[:, qk + vg + SSM_D_INNER:qk + vg + SSM_D_INNER + CONV_DIM]
    dt = w[:, qk + vg + SSM_D_INNER + CONV_DIM:]
    pieces = [xbc, dt, jnp.zeros((d, OFF_Q - OFF_DT - SSM_HEADS), w.dtype), q_k,
              jnp.zeros((d, OFF_V - OFF_K - RET_HEADS * RET_DK), w.dtype), v_g, z]
    out = jnp.concatenate(pieces, axis=1)
    assert out.shape[1] == HYB_IN_PAD
    return out.astype(BF16)


def _mla_wqt_layout(w_uq):
    k = w_uq.shape[0]
    w4 = w_uq.reshape(k, MLA_HEADS // 2, 2, QK_NOPE + QK_ROPE)
    nope = w4[..., :QK_NOPE].reshape(k, MLA_HEADS // 2, 2 * QK_NOPE)
    ropew = w4[..., QK_NOPE:].reshape(k, MLA_HEADS // 2, 2 * QK_ROPE)
    return jnp.concatenate([nope, ropew], -1).reshape(k, MLA_HEADS // 2 * QT_PAIR).T.astype(BF16)


def _mla_wq_layout(w_uq):
    k = w_uq.shape[0]
    w3 = w_uq.reshape(k, MLA_HEADS, QK_NOPE + QK_ROPE)
    nope, ropew = w3[:, :, :QK_NOPE], w3[:, :, QK_NOPE:]
    z = jnp.zeros_like(nope)
    even = (jnp.arange(MLA_HEADS) % 2 == 0)[None, :, None]
    first = jnp.where(even, jnp.concatenate([nope, z], -1), jnp.concatenate([z, nope], -1))
    second = jnp.concatenate([ropew, jnp.zeros((k, MLA_HEADS, LANES - QK_ROPE), w_uq.dtype)], -1)
    return jnp.concatenate([first, second], -1).reshape(k, MLA_HEADS * Q_HEAD_PAD).astype(BF16)


def kernel(x_prompt, x_sample, state_ret, state_ssm, state_conv, cache_ckv, cache_krope, page_table,
           norm_ffn1, ffn1_w_in, ffn1_w_out, norm_mix, norm_ffn2, ffn2_w_in, ffn2_w_out,
           hyb_w_in, hyb_w_out, hyb_conv_w, hyb_conv_b, hyb_dt_bias, hyb_a_log, hyb_d_skip,
           hyb_norm_w, mla_w_in, mla_q_norm_w, mla_kv_norm_w, mla_w_uq, mla_w_uk, mla_w_uv,
           mla_w_o, final_norm_w):
    bp, sp, d = x_prompt.shape
    bs, ss, _ = x_sample.shape
    assert ss == 1 and sp % CHUNK == 0
    depth = norm_ffn1.shape[0]
    mp = bp * sp
    xp = x_prompt.reshape(mp, d)
    xs = x_sample.reshape(bs, d)
    pos_p = jnp.arange(sp)
    pos_s = PAST_LEN + jnp.arange(1)
    fw = final_norm_w.reshape(1, d)

    ret_cos_p, ret_sin_p = _ret_tables(pos_p)
    ret_cos_s, ret_sin_s = _ret_tables(pos_s)
    mla_tabs_p = _mla_tables(pos_p, sp)
    mla_tabs_s = _mla_tables(pos_s, bs)
    mla_cos_t, mla_sin_t = (t.T for t in _rope_angles(pos_p, QK_ROPE // 2))
    log_gamma = jnp.log1p(-jnp.exp2(-5.0 - jnp.arange(RET_HEADS, dtype=F32)))
    lg_rows = jnp.broadcast_to(log_gamma[:, None], (RET_HEADS, LANES))

    outs = {k: [] for k in ("ret_p", "ret_s", "ssm_p", "ssm_s", "conv_p", "conv_s",
                            "ckv_p", "ckv_s", "kr_p", "kr_s")}
    for layer in range(depth):
        j = layer // 2
        last = layer == depth - 1
        w1i, w1o = _cast_layer(ffn1_w_in, layer), _cast_layer(ffn1_w_out, layer)
        w2i, w2o = _cast_layer(ffn2_w_in, layer), _cast_layer(ffn2_w_out, layer)
        n1 = norm_ffn1[layer].reshape(1, d)
        nm = norm_mix[layer].reshape(1, d)
        n2 = norm_ffn2[layer].reshape(1, d)
        xp = _ffn(xp, n1, w1i, w1o, fw)
        xs = _ffn(xs, n1, w1i, w1o, fw)
        if layer % 2 == 0:
            w_in = _hyb_w_in_layout(hyb_w_in[j])
            w_out = hyb_w_out[j].astype(BF16)
            conv_w = hyb_conv_w[j].astype(F32)
            conv_b = hyb_conv_b[j].reshape(1, CONV_DIM).astype(F32)
            dtb_row = _pad_lanes(hyb_dt_bias[j])
            a_row = _pad_lanes(-jnp.exp(hyb_a_log[j].astype(F32)))
            dsk_row = _pad_lanes(hyb_d_skip[j])
            gnw = hyb_norm_w[j].reshape(1, SSM_D_INNER).astype(F32)
            proj = _norm_matmul(xp, nm, w_in).reshape(bp, sp, HYB_IN_PAD)
            o_ret, r_p = _retention_prompt(proj, ret_cos_p, ret_sin_p, lg_rows.reshape(RET_HEADS, 1, LANES))
            dsk_x = jnp.repeat(hyb_d_skip[j].astype(F32), SSM_HEAD_DIM).reshape(1, SSM_D_INNER)
            o_ssd, s_p, c_p = _ssd_prompt(proj, conv_w, conv_b, dtb_row, a_row, dsk_x, gnw)
            xp = _matmul_res([o_ret.reshape(mp, -1), o_ssd.reshape(mp, -1)], w_out, xp)
            proj_s = _norm_matmul(xs, nm, w_in)
            mixed_s, r_s, s_s, c_s = _hybrid_decode(
                proj_s, state_ret[j], state_ssm[j], state_conv[j], ret_cos_s, ret_sin_s, lg_rows,
                conv_w, conv_b, dtb_row, a_row, dsk_row, gnw)
            xs = _matmul_res([mixed_s.reshape(bs, HYB_MIX)], w_out, xs)
            outs["ret_p"].append(r_p); outs["ret_s"].append(r_s)
            outs["ssm_p"].append(s_p); outs["ssm_s"].append(s_s)
            outs["conv_p"].append(c_p); outs["conv_s"].append(c_s)
        else:
            w_in = jnp.pad(mla_w_in[j], ((0, 0), (0, MLA_IN_PAD - MLA_IN))).astype(BF16)
            qnw = mla_q_norm_w[j].reshape(1, Q_LORA)
            kvnw = mla_kv_norm_w[j].reshape(1, KV_LORA)
            wq = _mla_wq_layout(mla_w_uq[j])
            w_uk2 = mla_w_uk[j].reshape(KV_LORA, MLA_HEADS * QK_NOPE)
            w_uv2 = mla_w_uv[j].reshape(KV_LORA, MLA_HEADS * V_DIM)
            krope_t = jnp.swapaxes(cache_krope, 2, 3)
            w_o = mla_w_o[j].astype(BF16)
            cq, ckv, kr, krp = _mla_in(xp, nm, w_in, qnw, kvnw, mla_tabs_p)
            qt = _mla_qt(cq, _mla_wqt_layout(mla_w_uq[j]), mla_cos_t, mla_sin_t, bp, sp)
            kn, vt = _mla_kv(ckv, w_uk2.astype(BF16), w_uv2.T.astype(BF16), bp, sp)
            o = _flash_prompt(qt, kn.reshape(bp, sp, -1), krp.reshape(bp, sp, LANES), vt)
            xp = _matmul_res([o.reshape(mp, MLA_HEADS * V_DIM)], w_o, xp)
            outs["ckv_p"].append(ckv.reshape(bp, sp, KV_LORA))
            outs["kr_p"].append(kr.reshape(bp, sp, QK_ROPE))
            cq_s, ckv_s, kr_s, krp_s = _mla_in(xs, nm, w_in, qnw, kvnw, mla_tabs_s)
            qp_s = _mla_q(cq_s, wq, mla_tabs_s)
            qcat = jnp.transpose(_qlat(qp_s, w_uk2.T.astype(BF16)), (1, 0, 2))
            o_lat = _mla_decode(page_table, qcat, ckv_s, krp_s, cache_ckv, krope_t, j)
            o_s = _olat(o_lat.reshape(bs, MLA_HEADS * KV_LORA), w_uv2.astype(BF16))
            xs = _matmul_res([o_s], w_o, xs)
            outs["ckv_s"].append(ckv_s.reshape(bs, 1, KV_LORA))
            outs["kr_s"].append(kr_s.reshape(bs, 1, QK_ROPE))
        xp = _ffn(xp, n2, w2i, w2o, fw, final_norm=last)
        xs = _ffn(xs, n2, w2i, w2o, fw, final_norm=last)
    if depth == 0:
        raise ValueError("depth must be positive")
    return (xp.reshape(bp, sp, d), xs.reshape(bs, 1, d),
            jnp.stack(outs["ret_p"]), jnp.stack(outs["ret_s"]),
            jnp.stack(outs["ssm_p"]), jnp.stack(outs["ssm_s"]),
            jnp.stack(outs["conv_p"]), jnp.stack(outs["conv_s"]),
            jnp.stack(outs["ckv_p"]), jnp.stack(outs["ckv_s"]),
            jnp.stack(outs["kr_p"]), jnp.stack(outs["kr_s"]))
```

```python
import functools
import math

import jax
import jax.numpy as jnp
from jax import lax
from jax.experimental import pallas as pl
from jax.experimental.pallas import tpu as pltpu

F32 = jnp.float32
BF16 = jnp.bfloat16

D_MODEL = 1024
D_FF = 2816
RMS_EPS = 1e-6
ROPE_THETA = 10000.0
CHUNK = 128
RET_HEADS = 4
RET_DK = 128
RET_DV = 256
SSM_HEADS = 16
SSM_HEAD_DIM = 64
SSM_D_INNER = 1024
SSM_STATE = 64
SSM_GROUPS = 2
CONV_W = 4
CONV_DIM = 1280
HYB_IN = 5392
HYB_IN_PAD = 5632
HYB_MIX = 2048
MLA_HEADS = 16
Q_LORA = 512
KV_LORA = 256
QK_NOPE = 64
QK_ROPE = 32
V_DIM = 64
MLA_IN = 800
MLA_IN_PAD = 896
MLA_SCALE = (QK_NOPE + QK_ROPE) ** -0.5
Q_SCALE = MLA_SCALE * math.log2(math.e)
PAST_LEN = 16384
PAGE_SIZE = 128

OFF_XBC = 0
OFF_DT = 1280
OFF_Q = 1408
OFF_K = 1920
OFF_V = 2560
OFF_G = 3584
OFF_Z = 4608

LANES = 128
SUBLANES = 8
VMEM_LIMIT_BYTES = 56 * 1024 * 1024

NT_DIMS = (((1,), (1,)), ((), ()))


def _cparams(sem):
    return pltpu.CompilerParams(dimension_semantics=sem, vmem_limit_bytes=VMEM_LIMIT_BYTES)


def _silu(x):
    return x * (0.5 * jnp.tanh(0.5 * x) + 0.5)


def _rms(x, w):
    return x * lax.rsqrt(jnp.mean(x * x, axis=-1, keepdims=True) + RMS_EPS) * w


def _bf16_round(x):
    return x.astype(BF16).astype(F32)


def _pick_tile(m, pref):
    t = min(m, pref)
    while m % t:
        t //= 2
    return t


def _cast_kernel(w_ref, o_ref):
    o_ref[...] = w_ref[0].astype(BF16)


def _cast_layer(w, layer):
    _, k, n = w.shape
    tr = _pick_tile(k, 128)
    return pl.pallas_call(
        _cast_kernel,
        grid=(k // tr,),
        in_specs=[pl.BlockSpec((1, tr, n), lambda i: (layer, i, 0))],
        out_specs=pl.BlockSpec((tr, n), lambda i: (i, 0)),
        out_shape=jax.ShapeDtypeStruct((k, n), BF16),
        compiler_params=_cparams(("parallel",)),
        name="cast_layer",
    )(w)


def _ffn_kernel(x_ref, nw_ref, wi_ref, wo_ref, fw_ref, o_ref, *, final_norm):
    x = x_ref[...]
    h = _rms(x, nw_ref[...]).astype(BF16)
    g = jnp.dot(h, wi_ref[:, :D_FF], preferred_element_type=F32)
    u = jnp.dot(h, wi_ref[:, D_FF:], preferred_element_type=F32)
    a = (_silu(g) * u).astype(BF16)
    y = x + 0.5 * jnp.dot(a, wo_ref[...], preferred_element_type=F32)
    if final_norm:
        y = _rms(y, fw_ref[...])
    o_ref[...] = y


def _ffn(x, nw, w_in, w_out, fw, *, final_norm=False):
    m, d = x.shape
    tm = _pick_tile(m, 512)
    once = pl.Buffered(1)
    return pl.pallas_call(
        functools.partial(_ffn_kernel, final_norm=final_norm),
        grid=(m // tm,),
        in_specs=[
            pl.BlockSpec((tm, d), lambda i: (i, 0)),
            pl.BlockSpec((1, d), lambda i: (0, 0)),
            pl.BlockSpec((d, 2 * D_FF), lambda i: (0, 0), pipeline_mode=once),
            pl.BlockSpec((D_FF, d), lambda i: (0, 0), pipeline_mode=once),
            pl.BlockSpec((1, d), lambda i: (0, 0)),
        ],
        out_specs=pl.BlockSpec((tm, d), lambda i: (i, 0)),
        out_shape=jax.ShapeDtypeStruct((m, d), F32),
        compiler_params=_cparams(("parallel",)),
        name="ffn",
    )(x, nw, w_in, w_out, fw)


def _norm_matmul_kernel(x_ref, nw_ref, w_ref, o_ref):
    h = _rms(x_ref[...], nw_ref[...]).astype(BF16)
    o_ref[...] = jnp.dot(h, w_ref[...], preferred_element_type=F32)


def _norm_matmul(x, nw, w):
    m, d = x.shape
    n = w.shape[1]
    tm = _pick_tile(m, 512)
    return pl.pallas_call(
        _norm_matmul_kernel,
        grid=(m // tm,),
        in_specs=[
            pl.BlockSpec((tm, d), lambda i: (i, 0)),
            pl.BlockSpec((1, d), lambda i: (0, 0)),
            pl.BlockSpec((d, n), lambda i: (0, 0), pipeline_mode=pl.Buffered(1)),
        ],
        out_specs=pl.BlockSpec((tm, n), lambda i: (i, 0)),
        out_shape=jax.ShapeDtypeStruct((m, n), F32),
        compiler_params=_cparams(("parallel",)),
        name="norm_matmul",
    )(x, nw, w)


def _matmul_res_kernel(*refs, nparts):
    a_refs, w_refs, r_ref, o_ref = refs[:nparts], refs[nparts:2 * nparts], refs[2 * nparts], refs[2 * nparts + 1]
    acc = r_ref[...]
    for a_ref, w_ref in zip(a_refs, w_refs):
        acc = acc + jnp.dot(a_ref[...].astype(BF16), w_ref[...], preferred_element_type=F32)
    o_ref[...] = acc


def _matmul_res(parts, w, res):
    m = res.shape[0]
    n = w.shape[1]
    k = parts[0].shape[1]
    assert all(a.shape == (m, k) for a in parts) and w.shape[0] == k * len(parts)
    tm = _pick_tile(m, 512)
    nparts = len(parts)
    return pl.pallas_call(
        functools.partial(_matmul_res_kernel, nparts=nparts),
        grid=(m // tm,),
        in_specs=([pl.BlockSpec((tm, k), lambda i: (i, 0)) for _ in parts]
                  + [pl.BlockSpec((k, n), functools.partial(lambda i, j: (j, 0), j=j)) for j in range(nparts)]
                  + [pl.BlockSpec((tm, n), lambda i: (i, 0))]),
        out_specs=pl.BlockSpec((tm, n), lambda i: (i, 0)),
        out_shape=jax.ShapeDtypeStruct((m, n), F32),
        compiler_params=_cparams(("parallel",)),
        name="matmul_res",
    )(*parts, *([w] * nparts), res)


def _rope_full(x, cosf, sinf):
    return x * cosf + pltpu.roll(x, RET_DK // 2, 1) * sinf


def _ret_kernel(q_ref, k_ref, v_ref, g_ref, cos_ref, sin_ref, lg_ref, o_ref, so_ref, s_ref, *, cps):
    c = pl.program_id(2)

    @pl.when(c == 0)
    def _():
        s_ref[...] = jnp.zeros_like(s_ref)

    lg = lg_ref[0]
    ii = lax.broadcasted_iota(jnp.int32, (CHUNK, CHUNK), 0)
    jj = lax.broadcasted_iota(jnp.int32, (CHUNK, CHUNK), 1)
    seg = jnp.where(ii >= jj, jnp.exp((ii - jj).astype(F32) * lg), 0.0)
    ri = lax.broadcasted_iota(jnp.int32, (CHUNK, RET_DK), 0).astype(F32)
    qdec = jnp.exp((ri + 1.0) * lg)
    kdec = jnp.exp((CHUNK - 1.0 - ri) * lg)
    cdec = jnp.exp(CHUNK * lg)[:, 0:1]

    s = s_ref[...]
    for t in range(cps):
        rows = pl.ds(t * CHUNK, CHUNK)
        cosf = cos_ref[rows, :]
        sinf = sin_ref[rows, :]
        q = _rope_full(q_ref[0, rows, :], cosf, sinf)
        k = _rope_full(k_ref[0, rows, :], cosf, sinf) * (RET_DK ** -0.5)
        vb = v_ref[0, rows, :].astype(BF16)
        sc = lax.dot_general(q.astype(BF16), k.astype(BF16), NT_DIMS, preferred_element_type=F32) * seg
        y = jnp.dot(sc.astype(BF16), vb, preferred_element_type=F32)
        y = y + jnp.dot((q * qdec).astype(BF16), s.astype(BF16), preferred_element_type=F32)
        kend_t = jnp.transpose(k * kdec).astype(BF16)
        s = cdec * s + jnp.dot(kend_t, vb, preferred_element_type=F32)
        y = y * lax.rsqrt(jnp.mean(y * y, axis=-1, keepdims=True) + RMS_EPS)
        o_ref[0, rows, :] = (y * _silu(g_ref[0, rows, :])).astype(BF16)
    s_ref[...] = s

    @pl.when(c == pl.num_programs(2) - 1)
    def _():
        so_ref[0, 0] = s


def _retention_prompt(proj, cosf, sinf, lg_rows, *, cps=8):
    b, l, _ = proj.shape
    while l % (cps * CHUNK):
        cps //= 2
    rows = cps * CHUNK
    qb = OFF_Q // RET_DK
    kb = OFF_K // RET_DK
    vb = OFF_V // RET_DV
    gb = OFF_G // RET_DV
    return pl.pallas_call(
        functools.partial(_ret_kernel, cps=cps),
        grid=(b, RET_HEADS, l // rows),
        in_specs=[
            pl.BlockSpec((1, rows, RET_DK), lambda i, h, c: (i, c, qb + h)),
            pl.BlockSpec((1, rows, RET_DK), lambda i, h, c: (i, c, kb + h)),
            pl.BlockSpec((1, rows, RET_DV), lambda i, h, c: (i, c, vb + h)),
            pl.BlockSpec((1, rows, RET_DV), lambda i, h, c: (i, c, gb + h)),
            pl.BlockSpec((rows, RET_DK), lambda i, h, c: (c, 0)),
            pl.BlockSpec((rows, RET_DK), lambda i, h, c: (c, 0)),
            pl.BlockSpec((1, 1, LANES), lambda i, h, c: (h, 0, 0)),
        ],
        out_specs=[
            pl.BlockSpec((1, rows, RET_DV), lambda i, h, c: (i, c, h)),
            pl.BlockSpec((1, 1, RET_DK, RET_DV), lambda i, h, c: (i, h, 0, 0)),
        ],
        out_shape=[
            jax.ShapeDtypeStruct((b, l, RET_HEADS * RET_DV), BF16),
            jax.ShapeDtypeStruct((b, RET_HEADS, RET_DK, RET_DV), F32),
        ],
        scratch_shapes=[pltpu.VMEM((RET_DK, RET_DV), F32)],
        compiler_params=_cparams(("parallel", "parallel", "arbitrary")),
        name="retention_prompt",
    )(proj, proj, proj, proj, cosf, sinf, lg_rows)


def _split3(x):
    hi = x.astype(BF16)
    r1 = x - hi.astype(F32)
    mid = r1.astype(BF16)
    lo = (r1 - mid.astype(F32)).astype(BF16)
    return hi, mid, lo


def _cumsum_rows(tril_bf, x):
    hi, mid, lo = _split3(x)
    out = jnp.dot(tril_bf, lo, preferred_element_type=F32)
    out = out + jnp.dot(tril_bf, mid, preferred_element_type=F32)
    return out + jnp.dot(tril_bf, hi, preferred_element_type=F32)


def _shift_rows(cur, tail, s):
    r = pltpu.roll(cur, s, 0)
    pt = pltpu.roll(tail, s, 0)
    row = lax.broadcasted_iota(jnp.int32, (SUBLANES, cur.shape[1]), 0)
    top = jnp.where(row < s, pt, r[0:SUBLANES])
    return jnp.concatenate([top, r[SUBLANES:]], axis=0)


def _softplus(x):
    return jnp.maximum(x, 0.0) + jnp.log(1.0 + jnp.exp(-jnp.abs(x)))


def _expand(x, e_bf, parts=3):
    hi, mid, lo = _split3(x)
    out = jnp.dot(mid, e_bf, preferred_element_type=F32)
    if parts == 3:
        out = jnp.dot(lo, e_bf, preferred_element_type=F32) + out
    return out + jnp.dot(hi, e_bf, preferred_element_type=F32)


def _ssd_kernel(xbc_ref, z0_ref, z1_ref, dt_ref, cw_ref, cb_ref, dtb_ref, a_ref, dskx_ref, nw_ref,
                e64_ref, e128_ref, o_ref, so_ref, co_ref, s_ref, tail_ref):
    c = pl.program_id(1)
    z_refs = (z0_ref, z1_ref)

    @pl.when(c == 0)
    def _():
        s_ref[...] = jnp.zeros_like(s_ref)
        tail_ref[...] = jnp.zeros_like(tail_ref)

    raw = xbc_ref[0]
    tail = tail_ref[...]
    acc = raw * cw_ref[CONV_W - 1:CONV_W, :] + cb_ref[...]
    for s in range(1, CONV_W):
        acc = acc + _shift_rows(raw, tail, s) * cw_ref[CONV_W - 1 - s:CONV_W - s, :]
    xbc = _silu(acc)
    tail_ref[...] = raw[CHUNK - SUBLANES:, :]

    @pl.when(c == pl.num_programs(1) - 1)
    def _():
        co_ref[0] = raw[CHUNK - (CONV_W - 1):, :]

    ii = lax.broadcasted_iota(jnp.int32, (CHUNK, CHUNK), 0)
    jj = lax.broadcasted_iota(jnp.int32, (CHUNK, CHUNK), 1)
    causal = ii >= jj
    tril_bf = jnp.where(causal, 1.0, 0.0).astype(BF16)

    dt = _softplus(dt_ref[0] + dtb_ref[...])
    la = dt * a_ref[...]
    cum = _cumsum_rows(tril_bf, la)
    cum_t = jnp.transpose(cum)

    dt_x = _expand(dt, e64_ref[...], parts=2)
    cum_x = _expand(cum, e64_ref[...])
    cc_all = _expand(cum, e128_ref[...])
    x_all = xbc[:, :SSM_D_INNER]
    xdt_all = x_all * dt_x
    ecum_x = jnp.exp(cum_x)
    dec_x = ecum_x[CHUNK - 1:CHUNK, :]

    b_all = xbc[:, SSM_D_INNER:SSM_D_INNER + LANES]
    c_all = xbc[:, SSM_D_INNER + LANES:SSM_D_INNER + 2 * LANES]
    b_all_t = jnp.transpose(b_all)
    left = lax.broadcasted_iota(jnp.int32, (CHUNK, LANES), 1) < SSM_HEAD_DIM
    top = lax.broadcasted_iota(jnp.int32, (CHUNK, LANES), 0) < SSM_STATE
    blockdiag = left == top
    pairs_per_group = SSM_HEADS // SSM_GROUPS // 2
    gw = SSM_D_INNER // SSM_GROUPS
    for g in range(SSM_GROUPS):
        bg = b_all[:, g * SSM_STATE:(g + 1) * SSM_STATE]
        cg = c_all[:, g * SSM_STATE:(g + 1) * SSM_STATE]
        bg_t = b_all_t[g * SSM_STATE:(g + 1) * SSM_STATE, :]
        cb = lax.dot_general(cg.astype(BF16), bg.astype(BF16), NT_DIMS, preferred_element_type=F32)
        cg2 = jnp.concatenate([cg, cg], axis=1)
        ys = []
        for pp in range(pairs_per_group):
            pr = g * pairs_per_group + pp
            lanes = slice(pr * LANES, (pr + 1) * LANES)
            lhs, kend = [], []
            for h in (2 * pr, 2 * pr + 1):
                cr = cum_t[h:h + 1, :]
                seg = jnp.exp(jnp.where(causal, cc_all[:, h * LANES:(h + 1) * LANES] - cr, -jnp.inf))
                lhs.append((cb * seg).astype(BF16))
                kend.append(bg_t * jnp.exp(cr[:, CHUNK - 1:CHUNK] - cr))
            xdt_p = xdt_all[:, lanes]
            rhs = jnp.concatenate([jnp.where(left, xdt_p, 0.0), jnp.where(left, 0.0, xdt_p)], axis=0)
            y = jnp.dot(jnp.concatenate(lhs, axis=1), rhs.astype(BF16), preferred_element_type=F32)
            sp = s_ref[pr]
            y = y + jnp.dot((cg2 * ecum_x[:, lanes]).astype(BF16), sp.astype(BF16), preferred_element_type=F32)
            ys.append(y + dskx_ref[:, lanes] * x_all[:, lanes])
            upd = jnp.dot(jnp.concatenate(kend, axis=0).astype(BF16), xdt_p.astype(BF16),
                          preferred_element_type=F32)
            s_ref[pr] = sp * dec_x[:, lanes] + jnp.where(blockdiag, upd, 0.0)
        yg = jnp.concatenate(ys, axis=1) * _silu(z_refs[g][0])
        yg = yg * lax.rsqrt(jnp.mean(yg * yg, axis=-1, keepdims=True) + RMS_EPS)
        o_ref[0, :, g * gw:(g + 1) * gw] = (yg * nw_ref[:, g * gw:(g + 1) * gw]).astype(BF16)

    @pl.when(c == pl.num_programs(1) - 1)
    def _():
        for pr in range(SSM_HEADS // 2):
            sp = s_ref[pr]
            so_ref[0, 2 * pr] = sp[:SSM_STATE, :SSM_HEAD_DIM]
            so_ref[0, 2 * pr + 1] = sp[SSM_STATE:, SSM_HEAD_DIM:]


def _head_expansion(width):
    col = jnp.arange(SSM_HEADS * width) // width
    return (jnp.arange(LANES)[:, None] == col[None, :]).astype(BF16)


def _ssd_prompt(proj, conv_w, conv_b, dtb_row, a_row, dsk_x, norm_w):
    b, l, _ = proj.shape
    gw = SSM_D_INNER // SSM_GROUPS
    full = lambda i, c: (0, 0)
    return pl.pallas_call(
        _ssd_kernel,
        grid=(b, l // CHUNK),
        in_specs=[
            pl.BlockSpec((1, CHUNK, CONV_DIM), lambda i, c: (i, c, OFF_XBC // CONV_DIM)),
            pl.BlockSpec((1, CHUNK, gw), lambda i, c: (i, c, OFF_Z // gw)),
            pl.BlockSpec((1, CHUNK, gw), lambda i, c: (i, c, OFF_Z // gw + 1)),
            pl.BlockSpec((1, CHUNK, LANES), lambda i, c: (i, c, OFF_DT // LANES)),
            pl.BlockSpec((CONV_W, CONV_DIM), full),
            pl.BlockSpec((1, CONV_DIM), full),
            pl.BlockSpec((1, LANES), full),
            pl.BlockSpec((1, LANES), full),
            pl.BlockSpec((1, SSM_D_INNER), full),
            pl.BlockSpec((1, SSM_D_INNER), full),
            pl.BlockSpec((LANES, SSM_HEADS * SSM_HEAD_DIM), full),
            pl.BlockSpec((LANES, SSM_HEADS * LANES), full),
        ],
        out_specs=[
            pl.BlockSpec((1, CHUNK, SSM_D_INNER), lambda i, c: (i, c, 0)),
            pl.BlockSpec((1, SSM_HEADS, SSM_STATE, SSM_HEAD_DIM), lambda i, c: (i, 0, 0, 0)),
            pl.BlockSpec((1, CONV_W - 1, CONV_DIM), lambda i, c: (i, 0, 0)),
        ],
        out_shape=[
            jax.ShapeDtypeStruct((b, l, SSM_D_INNER), BF16),
            jax.ShapeDtypeStruct((b, SSM_HEADS, SSM_STATE, SSM_HEAD_DIM), F32),
            jax.ShapeDtypeStruct((b, CONV_W - 1, CONV_DIM), F32),
        ],
        scratch_shapes=[
            pltpu.VMEM((SSM_HEADS // 2, 2 * SSM_STATE, 2 * SSM_HEAD_DIM), F32),
            pltpu.VMEM((SUBLANES, CONV_DIM), F32),
        ],
        compiler_params=_cparams(("parallel", "arbitrary")),
        name="ssd_prompt",
    )(proj, proj, proj, proj, conv_w, conv_b, dtb_row, a_row, dsk_x, norm_w,
      _head_expansion(SSM_HEAD_DIM), _head_expansion(LANES))


def _bcast_rows(x, n):
    return jnp.broadcast_to(x, (n, x.shape[1]))


def _column_matrix(row):
    return jnp.transpose(_bcast_rows(row, LANES))


def _hyb_decode_kernel(row_ref, sr_ref, ss_ref, cs_ref, cos_ref, sin_ref, lg_ref, cw_ref, cb_ref,
                       dtb_ref, a_ref, dsk_ref, nw_ref, o_ref, sro_ref, sso_ref, co_ref):
    row = row_ref[0]
    cosf = cos_ref[...]
    sinf = sin_ref[...]
    for h in range(RET_HEADS):
        q = _rope_full(_bcast_rows(row[:, OFF_Q + h * RET_DK:OFF_Q + (h + 1) * RET_DK], SUBLANES), cosf, sinf)
        k = _rope_full(_bcast_rows(row[:, OFF_K + h * RET_DK:OFF_K + (h + 1) * RET_DK], SUBLANES), cosf, sinf)
        k = k * (RET_DK ** -0.5)
        v = _bf16_round(row[:, OFF_V + h * RET_DV:OFF_V + (h + 1) * RET_DV])
        g = row[:, OFF_G + h * RET_DV:OFF_G + (h + 1) * RET_DV]
        gamma = jnp.exp(lg_ref[h:h + 1, :])
        qb = _bf16_round(q)
        kb = _bf16_round(k)
        kcol = _column_matrix(kb[0:1])
        s0 = sr_ref[0, h]
        sro_ref[0, h] = gamma[:, 0:1] * s0 + jnp.concatenate([kcol, kcol], axis=1) * v
        qcol = _column_matrix(_bf16_round(q * gamma)[0:1])
        y = jnp.sum(jnp.concatenate([qcol, qcol], axis=1) * _bf16_round(s0), axis=0, keepdims=True)
        score = jnp.sum(qb * kb, axis=-1, keepdims=True)[0:1]
        y = y + _bf16_round(score) * v
        y = y * lax.rsqrt(jnp.mean(y * y, axis=-1, keepdims=True) + RMS_EPS)
        o_ref[0, :, h * RET_DV:(h + 1) * RET_DV] = y * _silu(g)
    cs = cs_ref[0]
    raw = row[:, OFF_XBC:OFF_XBC + CONV_DIM]
    acc = raw * cw_ref[CONV_W - 1:CONV_W, :] + cb_ref[...]
    for w in range(CONV_W - 1):
        acc = acc + cs[w:w + 1, :] * cw_ref[w:w + 1, :]
    xbc = _silu(acc)
    co_ref[0, 0:CONV_W - 2, :] = cs[1:CONV_W - 1, :]
    co_ref[0, CONV_W - 2:CONV_W - 1, :] = raw
    dt = _softplus(row[:, OFF_DT:OFF_DT + LANES] + dtb_ref[...])
    la = dt * a_ref[...]
    dec = jnp.exp(la)
    b_all = _bf16_round(xbc[:, SSM_D_INNER:SSM_D_INNER + LANES])
    c_all = xbc[:, SSM_D_INNER + LANES:SSM_D_INNER + 2 * LANES]
    bcol = _column_matrix(b_all)
    ccol = _column_matrix(_bf16_round(c_all))
    rep = SSM_HEADS // SSM_GROUPS
    ys = []
    for h in range(SSM_HEADS):
        g = h // rep
        s0 = ss_ref[0, h]
        xh = xbc[:, h * SSM_HEAD_DIM:(h + 1) * SSM_HEAD_DIM]
        xdt = _bf16_round(xh * dt[:, h:h + 1])
        dech = dec[:, h:h + 1]
        bg = b_all[:, g * SSM_STATE:(g + 1) * SSM_STATE]
        cg = c_all[:, g * SSM_STATE:(g + 1) * SSM_STATE]
        sso_ref[0, h] = dech * s0 + bcol[g * SSM_STATE:(g + 1) * SSM_STATE, 0:SSM_HEAD_DIM] * xdt
        y = dech * jnp.sum(ccol[g * SSM_STATE:(g + 1) * SSM_STATE, 0:SSM_HEAD_DIM] * _bf16_round(s0),
                           axis=0, keepdims=True)
        score = jnp.sum(_bf16_round(cg) * bg, axis=-1, keepdims=True)
        ys.append(y + _bf16_round(score) * xdt + dsk_ref[:, h:h + 1] * xh)
    gw = rep * SSM_HEAD_DIM
    for g in range(SSM_GROUPS):
        yg = jnp.concatenate(ys[g * rep:(g + 1) * rep], axis=1)
        yg = yg * _silu(row[:, OFF_Z + g * gw:OFF_Z + (g + 1) * gw])
        yg = yg * lax.rsqrt(jnp.mean(yg * yg, axis=-1, keepdims=True) + RMS_EPS)
        lo = RET_HEADS * RET_DV + g * gw
        o_ref[0, :, lo:lo + gw] = yg * nw_ref[:, g * gw:(g + 1) * gw]


def _hybrid_decode(proj_s, state_ret, state_ssm, state_conv, cos_row, sin_row, lg_rows, conv_w, conv_b,
                   dtb_row, a_row, dsk_row, norm_w):
    nb = proj_s.shape[0]
    full = lambda i: (0, 0)
    return pl.pallas_call(
        _hyb_decode_kernel,
        grid=(nb,),
        in_specs=[
            pl.BlockSpec((1, 1, HYB_IN_PAD), lambda i: (i, 0, 0)),
            pl.BlockSpec((1, RET_HEADS, RET_DK, RET_DV), lambda i: (i, 0, 0, 0)),
            pl.BlockSpec((1, SSM_HEADS, SSM_STATE, SSM_HEAD_DIM), lambda i: (i, 0, 0, 0)),
            pl.BlockSpec((1, CONV_W - 1, CONV_DIM), lambda i: (i, 0, 0)),
            pl.BlockSpec((1, RET_DK), full),
            pl.BlockSpec((1, RET_DK), full),
            pl.BlockSpec((RET_HEADS, LANES), full),
            pl.BlockSpec((CONV_W, CONV_DIM), full),
            pl.BlockSpec((1, CONV_DIM), full),
            pl.BlockSpec((1, LANES), full),
            pl.BlockSpec((1, LANES), full),
            pl.BlockSpec((1, LANES), full),
            pl.BlockSpec((1, SSM_D_INNER), full),
        ],
        out_specs=[
            pl.BlockSpec((1, 1, HYB_MIX), lambda i: (i, 0, 0)),
            pl.BlockSpec((1, RET_HEADS, RET_DK, RET_DV), lambda i: (i, 0, 0, 0)),
            pl.BlockSpec((1, SSM_HEADS, SSM_STATE, SSM_HEAD_DIM), lambda i: (i, 0, 0, 0)),
            pl.BlockSpec((1, CONV_W - 1, CONV_DIM), lambda i: (i, 0, 0)),
        ],
        out_shape=[
            jax.ShapeDtypeStruct((nb, 1, HYB_MIX), F32),
            jax.ShapeDtypeStruct(state_ret.shape, F32),
            jax.ShapeDtypeStruct(state_ssm.shape, F32),
            jax.ShapeDtypeStruct(state_conv.shape, F32),
        ],
        compiler_params=_cparams(("parallel",)),
        name="hybrid_decode",
    )(proj_s.reshape(nb, 1, HYB_IN_PAD), state_ret, state_ssm, state_conv, cos_row, sin_row, lg_rows,
      conv_w, conv_b, dtb_row, a_row, dsk_row, norm_w)


def _rope_group(x, c, s1, s2):
    half = QK_ROPE // 2
    return x * c + pltpu.roll(x, LANES - half, 1) * s1 + pltpu.roll(x, half, 1) * s2


def _mla_in_kernel(x_ref, nw_ref, w_ref, qnw_ref, kvnw_ref, c_ref, s1_ref, s2_ref,
                   cq_ref, ckv_ref, kr_ref, krp_ref):
    h = _rms(x_ref[...], nw_ref[...]).astype(BF16)
    p = jnp.dot(h, w_ref[...], preferred_element_type=F32)
    cq_ref[...] = _rms(p[:, :Q_LORA], qnw_ref[...]).astype(BF16)
    ckv_ref[...] = _rms(p[:, Q_LORA:Q_LORA + KV_LORA], kvnw_ref[...])
    kr = _rope_group(p[:, Q_LORA + KV_LORA:], c_ref[...], s1_ref[...], s2_ref[...])
    kr_ref[...] = kr[:, :QK_ROPE]
    krp_ref[...] = kr.astype(BF16)


def _mla_in(x, nw, w, qnw, kvnw, tabs):
    m, d = x.shape
    tm = _pick_tile(m, 512)
    nt = tabs[0].shape[0] // tm
    full = lambda i: (0, 0)
    tab = pl.BlockSpec((tm, LANES), lambda i: (i % nt, 0))
    return pl.pallas_call(
        _mla_in_kernel,
        grid=(m // tm,),
        in_specs=[
            pl.BlockSpec((tm, d), lambda i: (i, 0)),
            pl.BlockSpec((1, d), full),
            pl.BlockSpec((d, MLA_IN_PAD), full),
            pl.BlockSpec((1, Q_LORA), full),
            pl.BlockSpec((1, KV_LORA), full),
            tab, tab, tab,
        ],
        out_specs=[
            pl.BlockSpec((tm, Q_LORA), lambda i: (i, 0)),
            pl.BlockSpec((tm, KV_LORA), lambda i: (i, 0)),
            pl.BlockSpec((tm, QK_ROPE), lambda i: (i, 0)),
            pl.BlockSpec((tm, LANES), lambda i: (i, 0)),
        ],
        out_shape=[
            jax.ShapeDtypeStruct((m, Q_LORA), BF16),
            jax.ShapeDtypeStruct((m, KV_LORA), F32),
            jax.ShapeDtypeStruct((m, QK_ROPE), F32),
            jax.ShapeDtypeStruct((m, LANES), BF16),
        ],
        compiler_params=_cparams(("parallel",)),
        name="mla_in",
    )(x, nw, w, qnw, kvnw, *tabs)


Q_HEAD_PAD = 2 * LANES
Q_TN = 2 * Q_HEAD_PAD


def _mla_q_kernel(cq_ref, w_ref, c_ref, s1_ref, s2_ref, o_ref):
    p = jnp.dot(cq_ref[...], w_ref[...], preferred_element_type=F32)
    for gi in range(Q_TN // LANES):
        x = p[:, gi * LANES:(gi + 1) * LANES]
        if gi % 2 == 1:
            x = _rope_group(x, c_ref[...], s1_ref[...], s2_ref[...])
        o_ref[:, gi * LANES:(gi + 1) * LANES] = (x * Q_SCALE).astype(BF16)


def _mla_q(cq, wq, tabs):
    m, k = cq.shape
    n = wq.shape[1]
    tm = _pick_tile(m, 512)
    nt = tabs[0].shape[0] // tm
    tab = pl.BlockSpec((tm, LANES), lambda i, j: (i % nt, 0))
    return pl.pallas_call(
        _mla_q_kernel,
        grid=(m // tm, n // Q_TN),
        in_specs=[
            pl.BlockSpec((tm, k), lambda i, j: (i, 0)),
            pl.BlockSpec((k, Q_TN), lambda i, j: (0, j)),
            tab, tab, tab,
        ],
        out_specs=pl.BlockSpec((tm, Q_TN), lambda i, j: (i, j)),
        out_shape=jax.ShapeDtypeStruct((m, n), BF16),
        compiler_params=_cparams(("parallel", "arbitrary")),
        name="mla_q",
    )(cq, wq, *tabs)


def _mla_qt_kernel(cq_ref, wt_ref, cos_ref, sin_ref, o_ref):
    p = lax.dot_general(wt_ref[...], cq_ref[...], NT_DIMS, preferred_element_type=F32)
    cos = cos_ref[...]
    sin = sin_ref[...]
    half = QK_ROPE // 2
    for h in range(MLA_HEADS):
        base = h * LANES
        x1 = p[base:base + half]
        x2 = p[base + half:base + QK_ROPE]
        o_ref[0, base:base + half, :] = ((x1 * cos - x2 * sin) * Q_SCALE).astype(BF16)
        o_ref[0, base + half:base + QK_ROPE, :] = ((x1 * sin + x2 * cos) * Q_SCALE).astype(BF16)
        o_ref[0, base + QK_ROPE:base + LANES, :] = (p[base + QK_ROPE:base + LANES] * Q_SCALE).astype(BF16)


def _mla_qt(cq, wq_t, cos_t, sin_t, b, l):
    m, k = cq.shape
    n = wq_t.shape[0]
    tm = _pick_tile(l, 512)
    nt = l // tm
    half = QK_ROPE // 2
    return pl.pallas_call(
        _mla_qt_kernel,
        grid=(m // tm,),
        in_specs=[
            pl.BlockSpec((tm, k), lambda i: (i, 0)),
            pl.BlockSpec((n, k), lambda i: (0, 0)),
            pl.BlockSpec((half, tm), lambda i: (0, i % nt)),
            pl.BlockSpec((half, tm), lambda i: (0, i % nt)),
        ],
        out_specs=pl.BlockSpec((1, n, tm), lambda i: (i // nt, 0, i % nt)),
        out_shape=jax.ShapeDtypeStruct((b, n, l), BF16),
        compiler_params=_cparams(("parallel",)),
        name="mla_qt",
    )(cq, wq_t, cos_t, sin_t)


def _mla_kv_kernel(c_ref, kr_ref, wk_ref, wvt_ref, k_ref, vt_ref):
    cb = c_ref[...].astype(BF16)
    kn = jnp.dot(cb, wk_ref[...], preferred_element_type=F32)
    kr = kr_ref[...].astype(F32)
    for h in range(MLA_HEADS):
        k_ref[:, h * LANES:(h + 1) * LANES] = (kn[:, h * LANES:(h + 1) * LANES] + kr).astype(BF16)
    vt_ref[0] = lax.dot_general(wvt_ref[...], cb, NT_DIMS, preferred_element_type=F32).astype(BF16)


def _mla_kv(ckv, krp, w_uk_pad, w_uv_t, b, l):
    m, k = ckv.shape
    nk = w_uk_pad.shape[1]
    nv = w_uv_t.shape[0]
    tm = _pick_tile(l, 512)
    nt = l // tm
    return pl.pallas_call(
        _mla_kv_kernel,
        grid=(m // tm,),
        in_specs=[
            pl.BlockSpec((tm, k), lambda i: (i, 0)),
            pl.BlockSpec((tm, LANES), lambda i: (i, 0)),
            pl.BlockSpec((k, nk), lambda i: (0, 0)),
            pl.BlockSpec((nv, k), lambda i: (0, 0)),
        ],
        out_specs=[
            pl.BlockSpec((tm, nk), lambda i: (i, 0)),
            pl.BlockSpec((1, nv, tm), lambda i: (i // nt, 0, i % nt)),
        ],
        out_shape=[
            jax.ShapeDtypeStruct((m, nk), BF16),
            jax.ShapeDtypeStruct((b, nv, l), BF16),
        ],
        compiler_params=_cparams(("parallel",)),
        name="mla_kv",
    )(ckv, krp, w_uk_pad, w_uv_t)


FLASH_HEADS = 16


def _flash_kernel(qi_ref, ki_ref, q_ref, k_ref, vt_ref, o_ref, m_ref, l_ref, acc_ref, *, t):
    step = pl.program_id(2)
    qi = qi_ref[step]
    ki = ki_ref[step]

    @pl.when(ki == 0)
    def _():
        m_ref[...] = jnp.full_like(m_ref, -jnp.inf)
        l_ref[...] = jnp.zeros_like(l_ref)
        acc_ref[...] = jnp.zeros_like(acc_ref)

    def update(masked):
        sts = [jnp.dot(k_ref[0, :, hh * LANES:(hh + 1) * LANES], q_ref[0, hh * LANES:(hh + 1) * LANES, :],
                       preferred_element_type=F32) for hh in range(FLASH_HEADS)]
        for hh in range(FLASH_HEADS):
            rows = pl.ds(hh * V_DIM, V_DIM)
            st = sts[hh]
            if masked:
                key = lax.broadcasted_iota(jnp.int32, (t, t), 0)
                qry = lax.broadcasted_iota(jnp.int32, (t, t), 1)
                st = jnp.where(key <= qry, st, -jnp.inf)
            m_prev = m_ref[hh]
            m_new = jnp.maximum(m_prev, jnp.max(st, axis=0, keepdims=True))
            alpha = jnp.exp2(m_prev - m_new)
            p = jnp.exp2(st - m_new)
            l_ref[hh] = alpha * l_ref[hh] + jnp.sum(p, axis=0, keepdims=True)
            m_ref[hh] = m_new
            pv = jnp.dot(vt_ref[0, rows, :], p.astype(BF16), preferred_element_type=F32)
            acc_ref[rows, :] = alpha * acc_ref[rows, :] + pv

    @pl.when(ki < qi)
    def _():
        update(False)

    @pl.when(ki == qi)
    def _():
        update(True)
        inv = jnp.concatenate([jnp.broadcast_to(1.0 / l_ref[hh], (V_DIM, t)) for hh in range(FLASH_HEADS)], axis=0)
        o_ref[0] = jnp.transpose(acc_ref[...] * inv).astype(BF16)


def _flash_prompt(qt, kk, vt, *, t=512):
    b, l, _ = kk.shape
    n = l // t
    qi_tab = jnp.asarray([qi for qi in range(n) for _ in range(qi + 1)], jnp.int32)
    ki_tab = jnp.asarray([ki for qi in range(n) for ki in range(qi + 1)], jnp.int32)
    nsteps = MLA_HEADS // FLASH_HEADS
    hw = FLASH_HEADS * V_DIM
    grid_spec = pltpu.PrefetchScalarGridSpec(
        num_scalar_prefetch=2,
        grid=(b, nsteps, int(qi_tab.shape[0])),
        in_specs=[
            pl.BlockSpec((1, FLASH_HEADS * LANES, t), lambda i, p, s, qt, kt: (i, p, qt[s])),
            pl.BlockSpec((1, t, FLASH_HEADS * LANES), lambda i, p, s, qt, kt: (i, kt[s], p)),
            pl.BlockSpec((1, hw, t), lambda i, p, s, qt, kt: (i, p, kt[s])),
        ],
        out_specs=pl.BlockSpec((1, t, hw), lambda i, p, s, qt, kt: (i, qt[s], p)),
        scratch_shapes=[
            pltpu.VMEM((FLASH_HEADS, 1, t), F32),
            pltpu.VMEM((FLASH_HEADS, 1, t), F32),
            pltpu.VMEM((hw, t), F32),
        ],
    )
    return pl.pallas_call(
        functools.partial(_flash_kernel, t=t),
        grid_spec=grid_spec,
        out_shape=jax.ShapeDtypeStruct((b, l, MLA_HEADS * V_DIM), BF16),
        compiler_params=_cparams(("parallel", "parallel", "arbitrary")),
        name="mla_flash",
    )(qi_tab, ki_tab, qt, kk, vt)


Q_CAT = KV_LORA + LANES


def _qlat_kernel(q_ref, w_ref, o_ref):
    for hh in range(2):
        qh = q_ref[:, hh * Q_HEAD_PAD:(hh + 1) * Q_HEAD_PAD]
        lat = jnp.dot(qh[:, :LANES], w_ref[...], preferred_element_type=F32)
        o_ref[hh] = jnp.concatenate([lat.astype(BF16), qh[:, LANES:]], axis=1)


def _qlat(qp_s, w_uk_t):
    nb = qp_s.shape[0]
    npairs = MLA_HEADS // 2
    return pl.pallas_call(
        _qlat_kernel,
        grid=(npairs,),
        in_specs=[
            pl.BlockSpec((nb, Q_TN), lambda p: (0, p)),
            pl.BlockSpec((LANES, KV_LORA), lambda p: (p, 0)),
        ],
        out_specs=pl.BlockSpec((2, nb, Q_CAT), lambda p: (p, 0, 0)),
        out_shape=jax.ShapeDtypeStruct((MLA_HEADS, nb, Q_CAT), BF16),
        compiler_params=_cparams(("parallel",)),
        name="mla_qlat",
    )(qp_s, w_uk_t)


def _olat_kernel(o_ref, w_ref, out_ref):
    nb = o_ref.shape[0]
    first = lax.broadcasted_iota(jnp.int32, (nb, LANES), 1) < V_DIM
    r0 = jnp.dot(o_ref[:, :KV_LORA], w_ref[...], preferred_element_type=F32)
    r1 = jnp.dot(o_ref[:, KV_LORA:], w_ref[...], preferred_element_type=F32)
    out_ref[...] = jnp.where(first, r0, r1).astype(BF16)


def _olat(o_lat, w_uv):
    nb = o_lat.shape[0]
    npairs = MLA_HEADS // 2
    return pl.pallas_call(
        _olat_kernel,
        grid=(npairs,),
        in_specs=[
            pl.BlockSpec((nb, 2 * KV_LORA), lambda p: (0, p)),
            pl.BlockSpec((KV_LORA, LANES), lambda p: (0, p)),
        ],
        out_specs=pl.BlockSpec((nb, LANES), lambda p: (0, p)),
        out_shape=jax.ShapeDtypeStruct((nb, MLA_HEADS * V_DIM), BF16),
        compiler_params=_cparams(("parallel",)),
        name="mla_olat",
    )(o_lat, w_uv)


PAGES_PER_CHUNK = 64
DECODE_SUBBLOCKS = 8


def _mla_decode_kernel(pt_ref, q_ref, cn_ref, kn_ref, ckv_hbm, krt_hbm, o_ref, ckbuf, krbuf, sem,
                       *, layer, nch):
    ppc = PAGES_PER_CHUNK
    b = pl.program_id(0)
    nb = pl.num_programs(0)

    def copies(bb, c, slot):
        out = []
        for i in range(ppc):
            pg = pt_ref[bb, c * ppc + i]
            out.append(pltpu.make_async_copy(ckv_hbm.at[layer, pg], ckbuf.at[slot, i], sem.at[slot]))
            out.append(pltpu.make_async_copy(krt_hbm.at[layer, pg], krbuf.at[slot, i], sem.at[slot]))
        return out

    def start(bb, c, slot):
        for cp in copies(bb, c, slot):
            cp.start()

    @pl.when(b == 0)
    def _():
        start(0, 0, 0)

    q = q_ref[0]
    q_lat = q[:, :KV_LORA]
    q_rope = q[:, KV_LORA:KV_LORA + QK_ROPE]
    psub = ppc // DECODE_SUBBLOCKS
    m_prev = jnp.full((MLA_HEADS, 1), -jnp.inf, F32)
    l_prev = jnp.zeros((MLA_HEADS, 1), F32)
    acc = jnp.zeros((MLA_HEADS, KV_LORA), F32)
    for c in range(nch):
        slot = c % 2
        if c + 1 < nch:
            start(b, c + 1, 1 - slot)
        else:
            @pl.when(b + 1 < nb)
            def _():
                start(b + 1, 0, 1 - slot)
        for cp in copies(b, c, slot):
            cp.wait()
        cks, ss = [], []
        for u in range(DECODE_SUBBLOCKS):
            p0 = u * psub
            ck = ckbuf[slot, p0:p0 + psub].reshape(psub * PAGE_SIZE, KV_LORA).astype(BF16)
            krt = jnp.concatenate([krbuf[slot, p0 + i] for i in range(psub)], axis=1).astype(BF16)
            s = lax.dot_general(q_lat, ck, NT_DIMS, preferred_element_type=F32)
            cks.append(ck)
            ss.append(s + jnp.dot(q_rope, krt, preferred_element_type=F32))
        m_new = m_prev
        for s in ss:
            m_new = jnp.maximum(m_new, jnp.max(s, axis=-1, keepdims=True))
        alpha = jnp.exp2(m_prev - m_new)
        l_prev = alpha * l_prev
        acc = alpha * acc
        for ck, s in zip(cks, ss):
            p = jnp.exp2(s - m_new)
            l_prev = l_prev + jnp.sum(p, axis=-1, keepdims=True)
            acc = acc + jnp.dot(p.astype(BF16), ck, preferred_element_type=F32)
        m_prev = m_new

    cn = cn_ref[0]
    knew = jnp.concatenate([cn.astype(BF16), kn_ref[0]], axis=1).astype(F32)
    s_new = jnp.sum(q.astype(F32) * knew, axis=-1, keepdims=True)
    m_new = jnp.maximum(m_prev, s_new)
    alpha = jnp.exp2(m_prev - m_new)
    p = jnp.exp2(s_new - m_new)
    l_new = alpha * l_prev + p
    acc = alpha * acc + _bf16_round(p) * _bf16_round(cn)
    o_ref[0] = (acc / l_new).astype(BF16)


def _mla_decode(page_table, qcat, ckv_new, krp_new, cache_ckv, cache_krope_t, layer):
    nb, npages = page_table.shape
    nch = npages // PAGES_PER_CHUNK
    assert npages % PAGES_PER_CHUNK == 0 and nch % 2 == 0
    grid_spec = pltpu.PrefetchScalarGridSpec(
        num_scalar_prefetch=1,
        grid=(nb,),
        in_specs=[
            pl.BlockSpec((1, MLA_HEADS, Q_CAT), lambda i, pt: (i, 0, 0)),
            pl.BlockSpec((1, 1, KV_LORA), lambda i, pt: (i, 0, 0)),
            pl.BlockSpec((1, 1, LANES), lambda i, pt: (i, 0, 0)),
            pl.BlockSpec(memory_space=pl.ANY),
            pl.BlockSpec(memory_space=pl.ANY),
        ],
        out_specs=pl.BlockSpec((1, MLA_HEADS, KV_LORA), lambda i, pt: (i, 0, 0)),
        scratch_shapes=[
            pltpu.VMEM((2, PAGES_PER_CHUNK, PAGE_SIZE, KV_LORA), F32),
            pltpu.VMEM((2, PAGES_PER_CHUNK, QK_ROPE, PAGE_SIZE), F32),
            pltpu.SemaphoreType.DMA((2,)),
        ],
    )
    return pl.pallas_call(
        functools.partial(_mla_decode_kernel, layer=layer, nch=nch),
        grid_spec=grid_spec,
        out_shape=jax.ShapeDtypeStruct((nb, MLA_HEADS, KV_LORA), BF16),
        compiler_params=_cparams(("arbitrary",)),
        name="mla_decode",
    )(page_table, qcat, ckv_new.reshape(nb, 1, KV_LORA), krp_new.reshape(nb, 1, LANES),
      cache_ckv, cache_krope_t)


def _rope_angles(pos, half):
    inv = ROPE_THETA ** (-jnp.arange(half, dtype=F32) / half)
    ang = pos.astype(F32)[:, None] * inv[None, :]
    return jnp.cos(ang), jnp.sin(ang)


def _ret_tables(pos):
    cos, sin = _rope_angles(pos, RET_DK // 2)
    return jnp.concatenate([cos, cos], axis=1), jnp.concatenate([-sin, sin], axis=1)


def _mla_tables(pos, rows):
    half = QK_ROPE // 2
    cos, sin = _rope_angles(pos, half)
    n = pos.shape[0]
    c = jnp.concatenate([cos, cos, jnp.ones((n, LANES - QK_ROPE), F32)], axis=1)
    s1 = jnp.concatenate([-sin, jnp.zeros((n, LANES - half), F32)], axis=1)
    s2 = jnp.concatenate([jnp.zeros((n, half), F32), sin, jnp.zeros((n, LANES - QK_ROPE), F32)], axis=1)
    return tuple(jnp.broadcast_to(t, (rows, LANES)) if n == 1 else t for t in (c, s1, s2))


def _pad_lanes(v):
    return jnp.pad(v.astype(F32), (0, LANES - v.shape[0])).reshape(1, LANES)


def _hyb_w_in_layout(w):
    d = w.shape[0]
    qk = 2 * RET_HEADS * RET_DK
    vg = 2 * RET_HEADS * RET_DV
    q_k = w[:, :qk]
    v_g = w[:, qk:qk + vg]
    z = w[:, qk + vg:qk + vg + SSM_D_INNER]
    xbc = w[:, qk + vg + SSM_D_INNER:qk + vg + SSM_D_INNER + CONV_DIM]
    dt = w[:, qk + vg + SSM_D_INNER + CONV_DIM:]
    pieces = [xbc, dt, jnp.zeros((d, OFF_Q - OFF_DT - SSM_HEADS), w.dtype), q_k,
              jnp.zeros((d, OFF_V - OFF_K - RET_HEADS * RET_DK), w.dtype), v_g, z]
    out = jnp.concatenate(pieces, axis=1)
    assert out.shape[1] == HYB_IN_PAD
    return out.astype(BF16)


def _mla_wqt_layout(w_uq):
    k = w_uq.shape[0]
    w3 = w_uq.reshape(k, MLA_HEADS, QK_NOPE + QK_ROPE)
    pad = jnp.zeros((k, MLA_HEADS, LANES - QK_NOPE - QK_ROPE), w_uq.dtype)
    w = jnp.concatenate([w3[..., QK_NOPE:], w3[..., :QK_NOPE], pad], -1)
    return w.reshape(k, MLA_HEADS * LANES).T.astype(BF16)


def _mla_wuk_layout(w_uk):
    k = w_uk.shape[0]
    z1 = jnp.zeros((k, MLA_HEADS, QK_ROPE), w_uk.dtype)
    z2 = jnp.zeros((k, MLA_HEADS, LANES - QK_NOPE - QK_ROPE), w_uk.dtype)
    return jnp.concatenate([z1, w_uk, z2], -1).reshape(k, MLA_HEADS * LANES).astype(BF16)


def _mla_wq_layout(w_uq):
    k = w_uq.shape[0]
    w3 = w_uq.reshape(k, MLA_HEADS, QK_NOPE + QK_ROPE)
    nope, ropew = w3[:, :, :QK_NOPE], w3[:, :, QK_NOPE:]
    z = jnp.zeros_like(nope)
    even = (jnp.arange(MLA_HEADS) % 2 == 0)[None, :, None]
    first = jnp.where(even, jnp.concatenate([nope, z], -1), jnp.concatenate([z, nope], -1))
    second = jnp.concatenate([ropew, jnp.zeros((k, MLA_HEADS, LANES - QK_ROPE), w_uq.dtype)], -1)
    return jnp.concatenate([first, second], -1).reshape(k, MLA_HEADS * Q_HEAD_PAD).astype(BF16)


def kernel(x_prompt, x_sample, state_ret, state_ssm, state_conv, cache_ckv, cache_krope, page_table,
           norm_ffn1, ffn1_w_in, ffn1_w_out, norm_mix, norm_ffn2, ffn2_w_in, ffn2_w_out,
           hyb_w_in, hyb_w_out, hyb_conv_w, hyb_conv_b, hyb_dt_bias, hyb_a_log, hyb_d_skip,
           hyb_norm_w, mla_w_in, mla_q_norm_w, mla_kv_norm_w, mla_w_uq, mla_w_uk, mla_w_uv,
           mla_w_o, final_norm_w):
    bp, sp, d = x_prompt.shape
    bs, ss, _ = x_sample.shape
    assert ss == 1 and sp % CHUNK == 0
    depth = norm_ffn1.shape[0]
    mp = bp * sp
    xp = x_prompt.reshape(mp, d)
    xs = x_sample.reshape(bs, d)
    pos_p = jnp.arange(sp)
    pos_s = PAST_LEN + jnp.arange(1)
    fw = final_norm_w.reshape(1, d)

    ret_cos_p, ret_sin_p = _ret_tables(pos_p)
    ret_cos_s, ret_sin_s = _ret_tables(pos_s)
    mla_tabs_p = _mla_tables(pos_p, sp)
    mla_tabs_s = _mla_tables(pos_s, bs)
    mla_cos_t, mla_sin_t = (t.T for t in _rope_angles(pos_p, QK_ROPE // 2))
    log_gamma = jnp.log1p(-jnp.exp2(-5.0 - jnp.arange(RET_HEADS, dtype=F32)))
    lg_rows = jnp.broadcast_to(log_gamma[:, None], (RET_HEADS, LANES))

    outs = {k: [] for k in ("ret_p", "ret_s", "ssm_p", "ssm_s", "conv_p", "conv_s",
                            "ckv_p", "ckv_s", "kr_p", "kr_s")}
    for layer in range(depth):
        j = layer // 2
        last = layer == depth - 1
        w1i, w1o = _cast_layer(ffn1_w_in, layer), _cast_layer(ffn1_w_out, layer)
        w2i, w2o = _cast_layer(ffn2_w_in, layer), _cast_layer(ffn2_w_out, layer)
        n1 = norm_ffn1[layer].reshape(1, d)
        nm = norm_mix[layer].reshape(1, d)
        n2 = norm_ffn2[layer].reshape(1, d)
        xp = _ffn(xp, n1, w1i, w1o, fw)
        xs = _ffn(xs, n1, w1i, w1o, fw)
        if layer % 2 == 0:
            w_in = _hyb_w_in_layout(hyb_w_in[j])
            w_out = hyb_w_out[j].astype(BF16)
            conv_w = hyb_conv_w[j].astype(F32)
            conv_b = hyb_conv_b[j].reshape(1, CONV_DIM).astype(F32)
            dtb_row = _pad_lanes(hyb_dt_bias[j])
            a_row = _pad_lanes(-jnp.exp(hyb_a_log[j].astype(F32)))
            dsk_row = _pad_lanes(hyb_d_skip[j])
            gnw = hyb_norm_w[j].reshape(1, SSM_D_INNER).astype(F32)
            proj = _norm_matmul(xp, nm, w_in).reshape(bp, sp, HYB_IN_PAD)
            o_ret, r_p = _retention_prompt(proj, ret_cos_p, ret_sin_p, lg_rows.reshape(RET_HEADS, 1, LANES))
            dsk_x = jnp.repeat(hyb_d_skip[j].astype(F32), SSM_HEAD_DIM).reshape(1, SSM_D_INNER)
            o_ssd, s_p, c_p = _ssd_prompt(proj, conv_w, conv_b, dtb_row, a_row, dsk_x, gnw)
            xp = _matmul_res([o_ret.reshape(mp, -1), o_ssd.reshape(mp, -1)], w_out, xp)
            proj_s = _norm_matmul(xs, nm, w_in)
            mixed_s, r_s, s_s, c_s = _hybrid_decode(
                proj_s, state_ret[j], state_ssm[j], state_conv[j], ret_cos_s, ret_sin_s, lg_rows,
                conv_w, conv_b, dtb_row, a_row, dsk_row, gnw)
            xs = _matmul_res([mixed_s.reshape(bs, HYB_MIX)], w_out, xs)
            outs["ret_p"].append(r_p); outs["ret_s"].append(r_s)
            outs["ssm_p"].append(s_p); outs["ssm_s"].append(s_s)
            outs["conv_p"].append(c_p); outs["conv_s"].append(c_s)
        else:
            w_in = jnp.pad(mla_w_in[j], ((0, 0), (0, MLA_IN_PAD - MLA_IN))).astype(BF16)
            qnw = mla_q_norm_w[j].reshape(1, Q_LORA)
            kvnw = mla_kv_norm_w[j].reshape(1, KV_LORA)
            wq = _mla_wq_layout(mla_w_uq[j])
            w_uk2 = mla_w_uk[j].reshape(KV_LORA, MLA_HEADS * QK_NOPE)
            w_uv2 = mla_w_uv[j].reshape(KV_LORA, MLA_HEADS * V_DIM)
            krope_t = jnp.swapaxes(cache_krope, 2, 3)
            w_o = mla_w_o[j].astype(BF16)
            cq, ckv, kr, krp = _mla_in(xp, nm, w_in, qnw, kvnw, mla_tabs_p)
            qt = _mla_qt(cq, _mla_wqt_layout(mla_w_uq[j]), mla_cos_t, mla_sin_t, bp, sp)
            kk, vt = _mla_kv(ckv, krp, _mla_wuk_layout(mla_w_uk[j]), w_uv2.T.astype(BF16), bp, sp)
            o = _flash_prompt(qt, kk.reshape(bp, sp, -1), vt)
            xp = _matmul_res([o.reshape(mp, MLA_HEADS * V_DIM)], w_o, xp)
            outs["ckv_p"].append(ckv.reshape(bp, sp, KV_LORA))
            outs["kr_p"].append(kr.reshape(bp, sp, QK_ROPE))
            cq_s, ckv_s, kr_s, krp_s = _mla_in(xs, nm, w_in, qnw, kvnw, mla_tabs_s)
            qp_s = _mla_q(cq_s, wq, mla_tabs_s)
            qcat = jnp.transpose(_qlat(qp_s, w_uk2.T.astype(BF16)), (1, 0, 2))
            o_lat = _mla_decode(page_table, qcat, ckv_s, krp_s, cache_ckv, krope_t, j)
            o_s = _olat(o_lat.reshape(bs, MLA_HEADS * KV_LORA), w_uv2.astype(BF16))
            xs = _matmul_res([o_s], w_o, xs)
            outs["ckv_s"].append(ckv_s.reshape(bs, 1, KV_LORA))
            outs["kr_s"].append(kr_s.reshape(bs, 1, QK_ROPE))
        xp = _ffn(xp, n2, w2i, w2o, fw, final_norm=last)
        xs = _ffn(xs, n2, w2i, w2o, fw, final_norm=last)
    if depth == 0:
        raise ValueError("depth must be positive")
    return (xp.reshape(bp, sp, d), xs.reshape(bs, 1, d),
            jnp.stack(outs["ret_p"]), jnp.stack(outs["ret_s"]),
            jnp.stack(outs["ssm_p"]), jnp.stack(outs["ssm_s"]),
            jnp.stack(outs["conv_p"]), jnp.stack(outs["conv_s"]),
            jnp.stack(outs["ckv_p"]), jnp.stack(outs["ckv_s"]),
            jnp.stack(outs["kr_p"]), jnp.stack(outs["kr_s"]))
```

```python
import functools
import math

import jax
import jax.numpy as jnp
from jax import lax
from jax.experimental import pallas as pl
from jax.experimental.pallas import tpu as pltpu

F32 = jnp.float32
BF16 = jnp.bfloat16

D_MODEL = 1024
D_FF = 2816
RMS_EPS = 1e-6
ROPE_THETA = 10000.0
CHUNK = 128
RET_HEADS = 4
RET_DK = 128
RET_DV = 256
SSM_HEADS = 16
SSM_HEAD_DIM = 64
SSM_D_INNER = 1024
SSM_STATE = 64
SSM_GROUPS = 2
CONV_W = 4
CONV_DIM = 1280
HYB_IN = 5392
HYB_IN_PAD = 5632
HYB_MIX = 2048
MLA_HEADS = 16
Q_LORA = 512
KV_LORA = 256
QK_NOPE = 64
QK_ROPE = 32
V_DIM = 64
MLA_IN = 800
MLA_IN_PAD = 896
MLA_SCALE = (QK_NOPE + QK_ROPE) ** -0.5
Q_SCALE = MLA_SCALE * math.log2(math.e)
PAST_LEN = 16384
PAGE_SIZE = 128

OFF_XBC = 0
OFF_DT = 1280
OFF_Q = 1408
OFF_K = 1920
OFF_V = 2560
OFF_G = 3584
OFF_Z = 4608

LANES = 128
SUBLANES = 8
VMEM_LIMIT_BYTES = 56 * 1024 * 1024

NT_DIMS = (((1,), (1,)), ((), ()))


def _cparams(sem):
    return pltpu.CompilerParams(dimension_semantics=sem, vmem_limit_bytes=VMEM_LIMIT_BYTES)


def _silu(x):
    return x * (0.5 * jnp.tanh(0.5 * x) + 0.5)


def _rms(x, w):
    return x * lax.rsqrt(jnp.mean(x * x, axis=-1, keepdims=True) + RMS_EPS) * w


def _bf16_round(x):
    return x.astype(BF16).astype(F32)


def _pick_tile(m, pref):
    t = min(m, pref)
    while m % t:
        t //= 2
    return t


def _cast_kernel(w_ref, o_ref):
    o_ref[...] = w_ref[0].astype(BF16)


def _cast_layer(w, layer):
    _, k, n = w.shape
    tr = _pick_tile(k, 128)
    return pl.pallas_call(
        _cast_kernel,
        grid=(k // tr,),
        in_specs=[pl.BlockSpec((1, tr, n), lambda i: (layer, i, 0))],
        out_specs=pl.BlockSpec((tr, n), lambda i: (i, 0)),
        out_shape=jax.ShapeDtypeStruct((k, n), BF16),
        compiler_params=_cparams(("parallel",)),
        name="cast_layer",
    )(w)


def _ffn_kernel(x_ref, nw_ref, wi_ref, wo_ref, fw_ref, o_ref, *, final_norm):
    x = x_ref[...]
    h = _rms(x, nw_ref[...]).astype(BF16)
    g = jnp.dot(h, wi_ref[:, :D_FF], preferred_element_type=F32)
    u = jnp.dot(h, wi_ref[:, D_FF:], preferred_element_type=F32)
    a = (_silu(g) * u).astype(BF16)
    y = x + 0.5 * jnp.dot(a, wo_ref[...], preferred_element_type=F32)
    if final_norm:
        y = _rms(y, fw_ref[...])
    o_ref[...] = y


def _ffn(x, nw, w_in, w_out, fw, *, final_norm=False):
    m, d = x.shape
    tm = _pick_tile(m, 512)
    once = pl.Buffered(1)
    return pl.pallas_call(
        functools.partial(_ffn_kernel, final_norm=final_norm),
        grid=(m // tm,),
        in_specs=[
            pl.BlockSpec((tm, d), lambda i: (i, 0)),
            pl.BlockSpec((1, d), lambda i: (0, 0)),
            pl.BlockSpec((d, 2 * D_FF), lambda i: (0, 0), pipeline_mode=once),
            pl.BlockSpec((D_FF, d), lambda i: (0, 0), pipeline_mode=once),
            pl.BlockSpec((1, d), lambda i: (0, 0)),
        ],
        out_specs=pl.BlockSpec((tm, d), lambda i: (i, 0)),
        out_shape=jax.ShapeDtypeStruct((m, d), F32),
        compiler_params=_cparams(("parallel",)),
        name="ffn",
    )(x, nw, w_in, w_out, fw)


def _norm_matmul_kernel(x_ref, nw_ref, w_ref, o_ref):
    h = _rms(x_ref[...], nw_ref[...]).astype(BF16)
    o_ref[...] = jnp.dot(h, w_ref[...], preferred_element_type=F32)


def _norm_matmul(x, nw, w):
    m, d = x.shape
    n = w.shape[1]
    tm = _pick_tile(m, 512)
    return pl.pallas_call(
        _norm_matmul_kernel,
        grid=(m // tm,),
        in_specs=[
            pl.BlockSpec((tm, d), lambda i: (i, 0)),
            pl.BlockSpec((1, d), lambda i: (0, 0)),
            pl.BlockSpec((d, n), lambda i: (0, 0), pipeline_mode=pl.Buffered(1)),
        ],
        out_specs=pl.BlockSpec((tm, n), lambda i: (i, 0)),
        out_shape=jax.ShapeDtypeStruct((m, n), F32),
        compiler_params=_cparams(("parallel",)),
        name="norm_matmul",
    )(x, nw, w)


def _matmul_res_kernel(*refs, nparts):
    a_refs, w_refs, r_ref, o_ref = refs[:nparts], refs[nparts:2 * nparts], refs[2 * nparts], refs[2 * nparts + 1]
    acc = r_ref[...]
    for a_ref, w_ref in zip(a_refs, w_refs):
        acc = acc + jnp.dot(a_ref[...].astype(BF16), w_ref[...], preferred_element_type=F32)
    o_ref[...] = acc


def _matmul_res(parts, w, res):
    m = res.shape[0]
    n = w.shape[1]
    k = parts[0].shape[1]
    assert all(a.shape == (m, k) for a in parts) and w.shape[0] == k * len(parts)
    tm = _pick_tile(m, 512)
    nparts = len(parts)
    return pl.pallas_call(
        functools.partial(_matmul_res_kernel, nparts=nparts),
        grid=(m // tm,),
        in_specs=([pl.BlockSpec((tm, k), lambda i: (i, 0)) for _ in parts]
                  + [pl.BlockSpec((k, n), functools.partial(lambda i, j: (j, 0), j=j)) for j in range(nparts)]
                  + [pl.BlockSpec((tm, n), lambda i: (i, 0))]),
        out_specs=pl.BlockSpec((tm, n), lambda i: (i, 0)),
        out_shape=jax.ShapeDtypeStruct((m, n), F32),
        compiler_params=_cparams(("parallel",)),
        name="matmul_res",
    )(*parts, *([w] * nparts), res)


def _rope_full(x, cosf, sinf):
    return x * cosf + pltpu.roll(x, RET_DK // 2, 1) * sinf


def _ret_kernel(q_ref, k_ref, v_ref, g_ref, cos_ref, sin_ref, lg_ref, o_ref, so_ref, s_ref, *, cps):
    c = pl.program_id(2)

    @pl.when(c == 0)
    def _():
        s_ref[...] = jnp.zeros_like(s_ref)

    lg = lg_ref[0]
    ii = lax.broadcasted_iota(jnp.int32, (CHUNK, CHUNK), 0)
    jj = lax.broadcasted_iota(jnp.int32, (CHUNK, CHUNK), 1)
    seg = jnp.where(ii >= jj, jnp.exp((ii - jj).astype(F32) * lg), 0.0)
    ri = lax.broadcasted_iota(jnp.int32, (CHUNK, RET_DK), 0).astype(F32)
    qdec = jnp.exp((ri + 1.0) * lg)
    kdec = jnp.exp((CHUNK - 1.0 - ri) * lg)
    cdec = jnp.exp(CHUNK * lg)[:, 0:1]

    s = s_ref[...]
    for t in range(cps):
        rows = pl.ds(t * CHUNK, CHUNK)
        cosf = cos_ref[rows, :]
        sinf = sin_ref[rows, :]
        q = _rope_full(q_ref[0, rows, :], cosf, sinf)
        k = _rope_full(k_ref[0, rows, :], cosf, sinf) * (RET_DK ** -0.5)
        vb = v_ref[0, rows, :].astype(BF16)
        sc = lax.dot_general(q.astype(BF16), k.astype(BF16), NT_DIMS, preferred_element_type=F32) * seg
        y = jnp.dot(sc.astype(BF16), vb, preferred_element_type=F32)
        y = y + jnp.dot((q * qdec).astype(BF16), s.astype(BF16), preferred_element_type=F32)
        kend_t = jnp.transpose(k * kdec).astype(BF16)
        s = cdec * s + jnp.dot(kend_t, vb, preferred_element_type=F32)
        y = y * lax.rsqrt(jnp.mean(y * y, axis=-1, keepdims=True) + RMS_EPS)
        o_ref[0, rows, :] = (y * _silu(g_ref[0, rows, :])).astype(BF16)
    s_ref[...] = s

    @pl.when(c == pl.num_programs(2) - 1)
    def _():
        so_ref[0, 0] = s


def _retention_prompt(proj, cosf, sinf, lg_rows, *, cps=16):
    b, l, _ = proj.shape
    while l % (cps * CHUNK):
        cps //= 2
    rows = cps * CHUNK
    qb = OFF_Q // RET_DK
    kb = OFF_K // RET_DK
    vb = OFF_V // RET_DV
    gb = OFF_G // RET_DV
    return pl.pallas_call(
        functools.partial(_ret_kernel, cps=cps),
        grid=(b, RET_HEADS, l // rows),
        in_specs=[
            pl.BlockSpec((1, rows, RET_DK), lambda i, h, c: (i, c, qb + h)),
            pl.BlockSpec((1, rows, RET_DK), lambda i, h, c: (i, c, kb + h)),
            pl.BlockSpec((1, rows, RET_DV), lambda i, h, c: (i, c, vb + h)),
            pl.BlockSpec((1, rows, RET_DV), lambda i, h, c: (i, c, gb + h)),
            pl.BlockSpec((rows, RET_DK), lambda i, h, c: (c, 0)),
            pl.BlockSpec((rows, RET_DK), lambda i, h, c: (c, 0)),
            pl.BlockSpec((1, 1, LANES), lambda i, h, c: (h, 0, 0)),
        ],
        out_specs=[
            pl.BlockSpec((1, rows, RET_DV), lambda i, h, c: (i, c, h)),
            pl.BlockSpec((1, 1, RET_DK, RET_DV), lambda i, h, c: (i, h, 0, 0)),
        ],
        out_shape=[
            jax.ShapeDtypeStruct((b, l, RET_HEADS * RET_DV), BF16),
            jax.ShapeDtypeStruct((b, RET_HEADS, RET_DK, RET_DV), F32),
        ],
        scratch_shapes=[pltpu.VMEM((RET_DK, RET_DV), F32)],
        compiler_params=_cparams(("parallel", "parallel", "arbitrary")),
        name="retention_prompt",
    )(proj, proj, proj, proj, cosf, sinf, lg_rows)


def _split3(x):
    hi = x.astype(BF16)
    r1 = x - hi.astype(F32)
    mid = r1.astype(BF16)
    lo = (r1 - mid.astype(F32)).astype(BF16)
    return hi, mid, lo


def _cumsum_rows(tril_bf, x):
    hi, mid, lo = _split3(x)
    out = jnp.dot(tril_bf, lo, preferred_element_type=F32)
    out = out + jnp.dot(tril_bf, mid, preferred_element_type=F32)
    return out + jnp.dot(tril_bf, hi, preferred_element_type=F32)


def _shift_rows(cur, tail, s):
    r = pltpu.roll(cur, s, 0)
    pt = pltpu.roll(tail, s, 0)
    row = lax.broadcasted_iota(jnp.int32, (SUBLANES, cur.shape[1]), 0)
    top = jnp.where(row < s, pt, r[0:SUBLANES])
    return jnp.concatenate([top, r[SUBLANES:]], axis=0)


def _softplus(x):
    return jnp.maximum(x, 0.0) + jnp.log(1.0 + jnp.exp(-jnp.abs(x)))


def _expand(x, e_bf, parts=3):
    hi, mid, lo = _split3(x)
    out = jnp.dot(mid, e_bf, preferred_element_type=F32)
    if parts == 3:
        out = jnp.dot(lo, e_bf, preferred_element_type=F32) + out
    return out + jnp.dot(hi, e_bf, preferred_element_type=F32)


def _ssd_kernel(xbc_ref, z0_ref, z1_ref, dt_ref, cw_ref, cb_ref, dtb_ref, a_ref, dskx_ref, nw_ref,
                e64_ref, e128_ref, o_ref, so_ref, co_ref, s_ref, tail_ref):
    c = pl.program_id(1)
    z_refs = (z0_ref, z1_ref)

    @pl.when(c == 0)
    def _():
        s_ref[...] = jnp.zeros_like(s_ref)
        tail_ref[...] = jnp.zeros_like(tail_ref)

    raw = xbc_ref[0]
    tail = tail_ref[...]
    acc = raw * cw_ref[CONV_W - 1:CONV_W, :] + cb_ref[...]
    for s in range(1, CONV_W):
        acc = acc + _shift_rows(raw, tail, s) * cw_ref[CONV_W - 1 - s:CONV_W - s, :]
    xbc = _silu(acc)
    tail_ref[...] = raw[CHUNK - SUBLANES:, :]

    @pl.when(c == pl.num_programs(1) - 1)
    def _():
        co_ref[0] = raw[CHUNK - (CONV_W - 1):, :]

    ii = lax.broadcasted_iota(jnp.int32, (CHUNK, CHUNK), 0)
    jj = lax.broadcasted_iota(jnp.int32, (CHUNK, CHUNK), 1)
    causal = ii >= jj
    tril_bf = jnp.where(causal, 1.0, 0.0).astype(BF16)

    dt = _softplus(dt_ref[0] + dtb_ref[...])
    la = dt * a_ref[...]
    cum = _cumsum_rows(tril_bf, la)
    cum_t = jnp.transpose(cum)

    dt_x = _expand(dt, e64_ref[...], parts=2)
    cum_x = _expand(cum, e64_ref[...])
    cc_all = _expand(cum, e128_ref[...])
    x_all = xbc[:, :SSM_D_INNER]
    xdt_all = x_all * dt_x
    ecum_x = jnp.exp(cum_x)
    dec_x = ecum_x[CHUNK - 1:CHUNK, :]

    b_all = xbc[:, SSM_D_INNER:SSM_D_INNER + LANES]
    c_all = xbc[:, SSM_D_INNER + LANES:SSM_D_INNER + 2 * LANES]
    b_all_t = jnp.transpose(b_all)
    left = lax.broadcasted_iota(jnp.int32, (CHUNK, LANES), 1) < SSM_HEAD_DIM
    top = lax.broadcasted_iota(jnp.int32, (CHUNK, LANES), 0) < SSM_STATE
    blockdiag = left == top
    pairs_per_group = SSM_HEADS // SSM_GROUPS // 2
    gw = SSM_D_INNER // SSM_GROUPS
    for g in range(SSM_GROUPS):
        bg = b_all[:, g * SSM_STATE:(g + 1) * SSM_STATE]
        cg = c_all[:, g * SSM_STATE:(g + 1) * SSM_STATE]
        bg_t = b_all_t[g * SSM_STATE:(g + 1) * SSM_STATE, :]
        cb = lax.dot_general(cg.astype(BF16), bg.astype(BF16), NT_DIMS, preferred_element_type=F32)
        cg2 = jnp.concatenate([cg, cg], axis=1)
        ys = []
        for pp in range(pairs_per_group):
            pr = g * pairs_per_group + pp
            lanes = slice(pr * LANES, (pr + 1) * LANES)
            lhs, kend = [], []
            for h in (2 * pr, 2 * pr + 1):
                cr = cum_t[h:h + 1, :]
                seg = jnp.exp(jnp.where(causal, cc_all[:, h * LANES:(h + 1) * LANES] - cr, -jnp.inf))
                lhs.append((cb * seg).astype(BF16))
                kend.append(bg_t * jnp.exp(cr[:, CHUNK - 1:CHUNK] - cr))
            xdt_p = xdt_all[:, lanes]
            rhs = jnp.concatenate([jnp.where(left, xdt_p, 0.0), jnp.where(left, 0.0, xdt_p)], axis=0)
            y = jnp.dot(jnp.concatenate(lhs, axis=1), rhs.astype(BF16), preferred_element_type=F32)
            sp = s_ref[pr]
            y = y + jnp.dot((cg2 * ecum_x[:, lanes]).astype(BF16), sp.astype(BF16), preferred_element_type=F32)
            ys.append(y + dskx_ref[:, lanes] * x_all[:, lanes])
            upd = jnp.dot(jnp.concatenate(kend, axis=0).astype(BF16), xdt_p.astype(BF16),
                          preferred_element_type=F32)
            s_ref[pr] = sp * dec_x[:, lanes] + jnp.where(blockdiag, upd, 0.0)
        yg = jnp.concatenate(ys, axis=1) * _silu(z_refs[g][0])
        yg = yg * lax.rsqrt(jnp.mean(yg * yg, axis=-1, keepdims=True) + RMS_EPS)
        o_ref[0, :, g * gw:(g + 1) * gw] = (yg * nw_ref[:, g * gw:(g + 1) * gw]).astype(BF16)

    @pl.when(c == pl.num_programs(1) - 1)
    def _():
        for pr in range(SSM_HEADS // 2):
            sp = s_ref[pr]
            so_ref[0, 2 * pr] = sp[:SSM_STATE, :SSM_HEAD_DIM]
            so_ref[0, 2 * pr + 1] = sp[SSM_STATE:, SSM_HEAD_DIM:]


def _head_expansion(width):
    col = jnp.arange(SSM_HEADS * width) // width
    return (jnp.arange(LANES)[:, None] == col[None, :]).astype(BF16)


def _ssd_prompt(proj, conv_w, conv_b, dtb_row, a_row, dsk_x, norm_w):
    b, l, _ = proj.shape
    gw = SSM_D_INNER // SSM_GROUPS
    full = lambda i, c: (0, 0)
    return pl.pallas_call(
        _ssd_kernel,
        grid=(b, l // CHUNK),
        in_specs=[
            pl.BlockSpec((1, CHUNK, CONV_DIM), lambda i, c: (i, c, OFF_XBC // CONV_DIM)),
            pl.BlockSpec((1, CHUNK, gw), lambda i, c: (i, c, OFF_Z // gw)),
            pl.BlockSpec((1, CHUNK, gw), lambda i, c: (i, c, OFF_Z // gw + 1)),
            pl.BlockSpec((1, CHUNK, LANES), lambda i, c: (i, c, OFF_DT // LANES)),
            pl.BlockSpec((CONV_W, CONV_DIM), full),
            pl.BlockSpec((1, CONV_DIM), full),
            pl.BlockSpec((1, LANES), full),
            pl.BlockSpec((1, LANES), full),
            pl.BlockSpec((1, SSM_D_INNER), full),
            pl.BlockSpec((1, SSM_D_INNER), full),
            pl.BlockSpec((LANES, SSM_HEADS * SSM_HEAD_DIM), full),
            pl.BlockSpec((LANES, SSM_HEADS * LANES), full),
        ],
        out_specs=[
            pl.BlockSpec((1, CHUNK, SSM_D_INNER), lambda i, c: (i, c, 0)),
            pl.BlockSpec((1, SSM_HEADS, SSM_STATE, SSM_HEAD_DIM), lambda i, c: (i, 0, 0, 0)),
            pl.BlockSpec((1, CONV_W - 1, CONV_DIM), lambda i, c: (i, 0, 0)),
        ],
        out_shape=[
            jax.ShapeDtypeStruct((b, l, SSM_D_INNER), BF16),
            jax.ShapeDtypeStruct((b, SSM_HEADS, SSM_STATE, SSM_HEAD_DIM), F32),
            jax.ShapeDtypeStruct((b, CONV_W - 1, CONV_DIM), F32),
        ],
        scratch_shapes=[
            pltpu.VMEM((SSM_HEADS // 2, 2 * SSM_STATE, 2 * SSM_HEAD_DIM), F32),
            pltpu.VMEM((SUBLANES, CONV_DIM), F32),
        ],
        compiler_params=_cparams(("parallel", "arbitrary")),
        name="ssd_prompt",
    )(proj, proj, proj, proj, conv_w, conv_b, dtb_row, a_row, dsk_x, norm_w,
      _head_expansion(SSM_HEAD_DIM), _head_expansion(LANES))


def _bcast_rows(x, n):
    return jnp.broadcast_to(x, (n, x.shape[1]))


def _column_matrix(row):
    return jnp.transpose(_bcast_rows(row, LANES))


def _hyb_decode_kernel(row_ref, sr_ref, ss_ref, cs_ref, cos_ref, sin_ref, lg_ref, cw_ref, cb_ref,
                       dtb_ref, a_ref, dsk_ref, nw_ref, o_ref, sro_ref, sso_ref, co_ref):
    row = row_ref[0]
    cosf = cos_ref[...]
    sinf = sin_ref[...]
    for h in range(RET_HEADS):
        q = _rope_full(_bcast_rows(row[:, OFF_Q + h * RET_DK:OFF_Q + (h + 1) * RET_DK], SUBLANES), cosf, sinf)
        k = _rope_full(_bcast_rows(row[:, OFF_K + h * RET_DK:OFF_K + (h + 1) * RET_DK], SUBLANES), cosf, sinf)
        k = k * (RET_DK ** -0.5)
        v = _bf16_round(row[:, OFF_V + h * RET_DV:OFF_V + (h + 1) * RET_DV])
        g = row[:, OFF_G + h * RET_DV:OFF_G + (h + 1) * RET_DV]
        gamma = jnp.exp(lg_ref[h:h + 1, :])
        qb = _bf16_round(q)
        kb = _bf16_round(k)
        kcol = _column_matrix(kb[0:1])
        s0 = sr_ref[0, h]
        sro_ref[0, h] = gamma[:, 0:1] * s0 + jnp.concatenate([kcol, kcol], axis=1) * v
        qcol = _column_matrix(_bf16_round(q * gamma)[0:1])
        y = jnp.sum(jnp.concatenate([qcol, qcol], axis=1) * _bf16_round(s0), axis=0, keepdims=True)
        score = jnp.sum(qb * kb, axis=-1, keepdims=True)[0:1]
        y = y + _bf16_round(score) * v
        y = y * lax.rsqrt(jnp.mean(y * y, axis=-1, keepdims=True) + RMS_EPS)
        o_ref[0, :, h * RET_DV:(h + 1) * RET_DV] = y * _silu(g)
    cs = cs_ref[0]
    raw = row[:, OFF_XBC:OFF_XBC + CONV_DIM]
    acc = raw * cw_ref[CONV_W - 1:CONV_W, :] + cb_ref[...]
    for w in range(CONV_W - 1):
        acc = acc + cs[w:w + 1, :] * cw_ref[w:w + 1, :]
    xbc = _silu(acc)
    co_ref[0, 0:CONV_W - 2, :] = cs[1:CONV_W - 1, :]
    co_ref[0, CONV_W - 2:CONV_W - 1, :] = raw
    dt = _softplus(row[:, OFF_DT:OFF_DT + LANES] + dtb_ref[...])
    la = dt * a_ref[...]
    dec = jnp.exp(la)
    b_all = _bf16_round(xbc[:, SSM_D_INNER:SSM_D_INNER + LANES])
    c_all = xbc[:, SSM_D_INNER + LANES:SSM_D_INNER + 2 * LANES]
    bcol = _column_matrix(b_all)
    ccol = _column_matrix(_bf16_round(c_all))
    rep = SSM_HEADS // SSM_GROUPS
    ys = []
    for h in range(SSM_HEADS):
        g = h // rep
        s0 = ss_ref[0, h]
        xh = xbc[:, h * SSM_HEAD_DIM:(h + 1) * SSM_HEAD_DIM]
        xdt = _bf16_round(xh * dt[:, h:h + 1])
        dech = dec[:, h:h + 1]
        bg = b_all[:, g * SSM_STATE:(g + 1) * SSM_STATE]
        cg = c_all[:, g * SSM_STATE:(g + 1) * SSM_STATE]
        sso_ref[0, h] = dech * s0 + bcol[g * SSM_STATE:(g + 1) * SSM_STATE, 0:SSM_HEAD_DIM] * xdt
        y = dech * jnp.sum(ccol[g * SSM_STATE:(g + 1) * SSM_STATE, 0:SSM_HEAD_DIM] * _bf16_round(s0),
                           axis=0, keepdims=True)
        score = jnp.sum(_bf16_round(cg) * bg, axis=-1, keepdims=True)
        ys.append(y + _bf16_round(score) * xdt + dsk_ref[:, h:h + 1] * xh)
    gw = rep * SSM_HEAD_DIM
    for g in range(SSM_GROUPS):
        yg = jnp.concatenate(ys[g * rep:(g + 1) * rep], axis=1)
        yg = yg * _silu(row[:, OFF_Z + g * gw:OFF_Z + (g + 1) * gw])
        yg = yg * lax.rsqrt(jnp.mean(yg * yg, axis=-1, keepdims=True) + RMS_EPS)
        lo = RET_HEADS * RET_DV + g * gw
        o_ref[0, :, lo:lo + gw] = yg * nw_ref[:, g * gw:(g + 1) * gw]


def _hybrid_decode(proj_s, state_ret, state_ssm, state_conv, cos_row, sin_row, lg_rows, conv_w, conv_b,
                   dtb_row, a_row, dsk_row, norm_w):
    nb = proj_s.shape[0]
    full = lambda i: (0, 0)
    return pl.pallas_call(
        _hyb_decode_kernel,
        grid=(nb,),
        in_specs=[
            pl.BlockSpec((1, 1, HYB_IN_PAD), lambda i: (i, 0, 0)),
            pl.BlockSpec((1, RET_HEADS, RET_DK, RET_DV), lambda i: (i, 0, 0, 0)),
            pl.BlockSpec((1, SSM_HEADS, SSM_STATE, SSM_HEAD_DIM), lambda i: (i, 0, 0, 0)),
            pl.BlockSpec((1, CONV_W - 1, CONV_DIM), lambda i: (i, 0, 0)),
            pl.BlockSpec((1, RET_DK), full),
            pl.BlockSpec((1, RET_DK), full),
            pl.BlockSpec((RET_HEADS, LANES), full),
            pl.BlockSpec((CONV_W, CONV_DIM), full),
            pl.BlockSpec((1, CONV_DIM), full),
            pl.BlockSpec((1, LANES), full),
            pl.BlockSpec((1, LANES), full),
            pl.BlockSpec((1, LANES), full),
            pl.BlockSpec((1, SSM_D_INNER), full),
        ],
        out_specs=[
            pl.BlockSpec((1, 1, HYB_MIX), lambda i: (i, 0, 0)),
            pl.BlockSpec((1, RET_HEADS, RET_DK, RET_DV), lambda i: (i, 0, 0, 0)),
            pl.BlockSpec((1, SSM_HEADS, SSM_STATE, SSM_HEAD_DIM), lambda i: (i, 0, 0, 0)),
            pl.BlockSpec((1, CONV_W - 1, CONV_DIM), lambda i: (i, 0, 0)),
        ],
        out_shape=[
            jax.ShapeDtypeStruct((nb, 1, HYB_MIX), F32),
            jax.ShapeDtypeStruct(state_ret.shape, F32),
            jax.ShapeDtypeStruct(state_ssm.shape, F32),
            jax.ShapeDtypeStruct(state_conv.shape, F32),
        ],
        compiler_params=_cparams(("parallel",)),
        name="hybrid_decode",
    )(proj_s.reshape(nb, 1, HYB_IN_PAD), state_ret, state_ssm, state_conv, cos_row, sin_row, lg_rows,
      conv_w, conv_b, dtb_row, a_row, dsk_row, norm_w)


def _rope_group(x, c, s1, s2):
    half = QK_ROPE // 2
    return x * c + pltpu.roll(x, LANES - half, 1) * s1 + pltpu.roll(x, half, 1) * s2


def _mla_in_kernel(x_ref, nw_ref, w_ref, qnw_ref, kvnw_ref, c_ref, s1_ref, s2_ref,
                   cq_ref, ckv_ref, kr_ref, krp_ref):
    h = _rms(x_ref[...], nw_ref[...]).astype(BF16)
    p = jnp.dot(h, w_ref[...], preferred_element_type=F32)
    cq_ref[...] = _rms(p[:, :Q_LORA], qnw_ref[...]).astype(BF16)
    ckv_ref[...] = _rms(p[:, Q_LORA:Q_LORA + KV_LORA], kvnw_ref[...])
    kr = _rope_group(p[:, Q_LORA + KV_LORA:], c_ref[...], s1_ref[...], s2_ref[...])
    kr_ref[...] = kr[:, :QK_ROPE]
    krp_ref[...] = kr.astype(BF16)


def _mla_in(x, nw, w, qnw, kvnw, tabs):
    m, d = x.shape
    tm = _pick_tile(m, 512)
    nt = tabs[0].shape[0] // tm
    full = lambda i: (0, 0)
    tab = pl.BlockSpec((tm, LANES), lambda i: (i % nt, 0))
    return pl.pallas_call(
        _mla_in_kernel,
        grid=(m // tm,),
        in_specs=[
            pl.BlockSpec((tm, d), lambda i: (i, 0)),
            pl.BlockSpec((1, d), full),
            pl.BlockSpec((d, MLA_IN_PAD), full),
            pl.BlockSpec((1, Q_LORA), full),
            pl.BlockSpec((1, KV_LORA), full),
            tab, tab, tab,
        ],
        out_specs=[
            pl.BlockSpec((tm, Q_LORA), lambda i: (i, 0)),
            pl.BlockSpec((tm, KV_LORA), lambda i: (i, 0)),
            pl.BlockSpec((tm, QK_ROPE), lambda i: (i, 0)),
            pl.BlockSpec((tm, LANES), lambda i: (i, 0)),
        ],
        out_shape=[
            jax.ShapeDtypeStruct((m, Q_LORA), BF16),
            jax.ShapeDtypeStruct((m, KV_LORA), F32),
            jax.ShapeDtypeStruct((m, QK_ROPE), F32),
            jax.ShapeDtypeStruct((m, LANES), BF16),
        ],
        compiler_params=_cparams(("parallel",)),
        name="mla_in",
    )(x, nw, w, qnw, kvnw, *tabs)


Q_HEAD_PAD = 2 * LANES
Q_TN = 2 * Q_HEAD_PAD


def _mla_q_kernel(cq_ref, w_ref, c_ref, s1_ref, s2_ref, o_ref):
    p = jnp.dot(cq_ref[...], w_ref[...], preferred_element_type=F32)
    for gi in range(Q_TN // LANES):
        x = p[:, gi * LANES:(gi + 1) * LANES]
        if gi % 2 == 1:
            x = _rope_group(x, c_ref[...], s1_ref[...], s2_ref[...])
        o_ref[:, gi * LANES:(gi + 1) * LANES] = (x * Q_SCALE).astype(BF16)


def _mla_q(cq, wq, tabs):
    m, k = cq.shape
    n = wq.shape[1]
    tm = _pick_tile(m, 512)
    nt = tabs[0].shape[0] // tm
    tab = pl.BlockSpec((tm, LANES), lambda i, j: (i % nt, 0))
    return pl.pallas_call(
        _mla_q_kernel,
        grid=(m // tm, n // Q_TN),
        in_specs=[
            pl.BlockSpec((tm, k), lambda i, j: (i, 0)),
            pl.BlockSpec((k, Q_TN), lambda i, j: (0, j)),
            tab, tab, tab,
        ],
        out_specs=pl.BlockSpec((tm, Q_TN), lambda i, j: (i, j)),
        out_shape=jax.ShapeDtypeStruct((m, n), BF16),
        compiler_params=_cparams(("parallel", "arbitrary")),
        name="mla_q",
    )(cq, wq, *tabs)


def _mla_qt_kernel(cq_ref, wt_ref, cos_ref, sin_ref, o_ref):
    p = lax.dot_general(wt_ref[...], cq_ref[...], NT_DIMS, preferred_element_type=F32)
    cos = cos_ref[...]
    sin = sin_ref[...]
    half = QK_ROPE // 2
    for h in range(MLA_HEADS):
        base = h * LANES
        x1 = p[base:base + half]
        x2 = p[base + half:base + QK_ROPE]
        o_ref[0, base:base + half, :] = ((x1 * cos - x2 * sin) * Q_SCALE).astype(BF16)
        o_ref[0, base + half:base + QK_ROPE, :] = ((x1 * sin + x2 * cos) * Q_SCALE).astype(BF16)
        o_ref[0, base + QK_ROPE:base + LANES, :] = (p[base + QK_ROPE:base + LANES] * Q_SCALE).astype(BF16)


def _mla_qt(cq, wq_t, cos_t, sin_t, b, l):
    m, k = cq.shape
    n = wq_t.shape[0]
    tm = _pick_tile(l, 512)
    nt = l // tm
    half = QK_ROPE // 2
    return pl.pallas_call(
        _mla_qt_kernel,
        grid=(m // tm,),
        in_specs=[
            pl.BlockSpec((tm, k), lambda i: (i, 0)),
            pl.BlockSpec((n, k), lambda i: (0, 0)),
            pl.BlockSpec((half, tm), lambda i: (0, i % nt)),
            pl.BlockSpec((half, tm), lambda i: (0, i % nt)),
        ],
        out_specs=pl.BlockSpec((1, n, tm), lambda i: (i // nt, 0, i % nt)),
        out_shape=jax.ShapeDtypeStruct((b, n, l), BF16),
        compiler_params=_cparams(("parallel",)),
        name="mla_qt",
    )(cq, wq_t, cos_t, sin_t)


def _mla_kv_kernel(c_ref, kr_ref, wk_ref, wvt_ref, k_ref, vt_ref):
    cb = c_ref[...].astype(BF16)
    kn = jnp.dot(cb, wk_ref[...], preferred_element_type=F32)
    kr = kr_ref[...].astype(F32)
    for h in range(MLA_HEADS):
        k_ref[:, h * LANES:(h + 1) * LANES] = (kn[:, h * LANES:(h + 1) * LANES] + kr).astype(BF16)
    vt_ref[0] = lax.dot_general(wvt_ref[...], cb, NT_DIMS, preferred_element_type=F32).astype(BF16)


def _mla_kv(ckv, krp, w_uk_pad, w_uv_t, b, l):
    m, k = ckv.shape
    nk = w_uk_pad.shape[1]
    nv = w_uv_t.shape[0]
    tm = _pick_tile(l, 512)
    nt = l // tm
    return pl.pallas_call(
        _mla_kv_kernel,
        grid=(m // tm,),
        in_specs=[
            pl.BlockSpec((tm, k), lambda i: (i, 0)),
            pl.BlockSpec((tm, LANES), lambda i: (i, 0)),
            pl.BlockSpec((k, nk), lambda i: (0, 0)),
            pl.BlockSpec((nv, k), lambda i: (0, 0)),
        ],
        out_specs=[
            pl.BlockSpec((tm, nk), lambda i: (i, 0)),
            pl.BlockSpec((1, nv, tm), lambda i: (i // nt, 0, i % nt)),
        ],
        out_shape=[
            jax.ShapeDtypeStruct((m, nk), BF16),
            jax.ShapeDtypeStruct((b, nv, l), BF16),
        ],
        compiler_params=_cparams(("parallel",)),
        name="mla_kv",
    )(ckv, krp, w_uk_pad, w_uv_t)


FLASH_HEADS = 16
FLASH_AHEAD = 2


def _flash_kernel(qi_ref, ki_ref, q_ref, k_ref, vt_ref, o_ref, m_ref, l_ref, acc_ref, *, t):
    step = pl.program_id(2)
    qi = qi_ref[step]
    ki = ki_ref[step]

    @pl.when(ki == 0)
    def _():
        m_ref[...] = jnp.full_like(m_ref, -jnp.inf)
        l_ref[...] = jnp.zeros_like(l_ref)
        acc_ref[...] = jnp.zeros_like(acc_ref)

    def update(masked):
        def scores(hh):
            return jnp.dot(k_ref[0, :, hh * LANES:(hh + 1) * LANES], q_ref[0, hh * LANES:(hh + 1) * LANES, :],
                           preferred_element_type=F32)

        sts = [scores(hh) for hh in range(FLASH_AHEAD)]
        for hh in range(FLASH_HEADS):
            if hh + FLASH_AHEAD < FLASH_HEADS:
                sts.append(scores(hh + FLASH_AHEAD))
            rows = pl.ds(hh * V_DIM, V_DIM)
            st = sts[hh]
            if masked:
                key = lax.broadcasted_iota(jnp.int32, (t, t), 0)
                qry = lax.broadcasted_iota(jnp.int32, (t, t), 1)
                st = jnp.where(key <= qry, st, -jnp.inf)
            m_prev = m_ref[hh]
            m_new = jnp.maximum(m_prev, jnp.max(st, axis=0, keepdims=True))
            alpha = jnp.exp2(m_prev - m_new)
            p = jnp.exp2(st - m_new)
            l_ref[hh] = alpha * l_ref[hh] + jnp.sum(p, axis=0, keepdims=True)
            m_ref[hh] = m_new
            pv = jnp.dot(vt_ref[0, rows, :], p.astype(BF16), preferred_element_type=F32)
            acc_ref[rows, :] = alpha * acc_ref[rows, :] + pv

    @pl.when(ki < qi)
    def _():
        update(False)

    @pl.when(ki == qi)
    def _():
        update(True)
        inv = jnp.concatenate([jnp.broadcast_to(1.0 / l_ref[hh], (V_DIM, t)) for hh in range(FLASH_HEADS)], axis=0)
        o_ref[0] = jnp.transpose(acc_ref[...] * inv).astype(BF16)


def _flash_prompt(qt, kk, vt, *, t=512):
    b, l, _ = kk.shape
    n = l // t
    qi_tab = jnp.asarray([qi for qi in range(n) for _ in range(qi + 1)], jnp.int32)
    ki_tab = jnp.asarray([ki for qi in range(n) for ki in range(qi + 1)], jnp.int32)
    nsteps = MLA_HEADS // FLASH_HEADS
    hw = FLASH_HEADS * V_DIM
    grid_spec = pltpu.PrefetchScalarGridSpec(
        num_scalar_prefetch=2,
        grid=(b, nsteps, int(qi_tab.shape[0])),
        in_specs=[
            pl.BlockSpec((1, FLASH_HEADS * LANES, t), lambda i, p, s, qt, kt: (i, p, qt[s])),
            pl.BlockSpec((1, t, FLASH_HEADS * LANES), lambda i, p, s, qt, kt: (i, kt[s], p)),
            pl.BlockSpec((1, hw, t), lambda i, p, s, qt, kt: (i, p, kt[s])),
        ],
        out_specs=pl.BlockSpec((1, t, hw), lambda i, p, s, qt, kt: (i, qt[s], p)),
        scratch_shapes=[
            pltpu.VMEM((FLASH_HEADS, 1, t), F32),
            pltpu.VMEM((FLASH_HEADS, 1, t), F32),
            pltpu.VMEM((hw, t), F32),
        ],
    )
    return pl.pallas_call(
        functools.partial(_flash_kernel, t=t),
        grid_spec=grid_spec,
        out_shape=jax.ShapeDtypeStruct((b, l, MLA_HEADS * V_DIM), BF16),
        compiler_params=_cparams(("parallel", "parallel", "arbitrary")),
        name="mla_flash",
    )(qi_tab, ki_tab, qt, kk, vt)


Q_CAT = KV_LORA + LANES


def _qlat_kernel(q_ref, w_ref, o_ref):
    for hh in range(2):
        qh = q_ref[:, hh * Q_HEAD_PAD:(hh + 1) * Q_HEAD_PAD]
        lat = jnp.dot(qh[:, :LANES], w_ref[...], preferred_element_type=F32)
        o_ref[hh] = jnp.concatenate([lat.astype(BF16), qh[:, LANES:]], axis=1)


def _qlat(qp_s, w_uk_t):
    nb = qp_s.shape[0]
    npairs = MLA_HEADS // 2
    return pl.pallas_call(
        _qlat_kernel,
        grid=(npairs,),
        in_specs=[
            pl.BlockSpec((nb, Q_TN), lambda p: (0, p)),
            pl.BlockSpec((LANES, KV_LORA), lambda p: (p, 0)),
        ],
        out_specs=pl.BlockSpec((2, nb, Q_CAT), lambda p: (p, 0, 0)),
        out_shape=jax.ShapeDtypeStruct((MLA_HEADS, nb, Q_CAT), BF16),
        compiler_params=_cparams(("parallel",)),
        name="mla_qlat",
    )(qp_s, w_uk_t)


def _olat_kernel(o_ref, w_ref, out_ref):
    nb = o_ref.shape[0]
    first = lax.broadcasted_iota(jnp.int32, (nb, LANES), 1) < V_DIM
    r0 = jnp.dot(o_ref[:, :KV_LORA], w_ref[...], preferred_element_type=F32)
    r1 = jnp.dot(o_ref[:, KV_LORA:], w_ref[...], preferred_element_type=F32)
    out_ref[...] = jnp.where(first, r0, r1).astype(BF16)


def _olat(o_lat, w_uv):
    nb = o_lat.shape[0]
    npairs = MLA_HEADS // 2
    return pl.pallas_call(
        _olat_kernel,
        grid=(npairs,),
        in_specs=[
            pl.BlockSpec((nb, 2 * KV_LORA), lambda p: (0, p)),
            pl.BlockSpec((KV_LORA, LANES), lambda p: (0, p)),
        ],
        out_specs=pl.BlockSpec((nb, LANES), lambda p: (0, p)),
        out_shape=jax.ShapeDtypeStruct((nb, MLA_HEADS * V_DIM), BF16),
        compiler_params=_cparams(("parallel",)),
        name="mla_olat",
    )(o_lat, w_uv)


PAGES_PER_CHUNK = 64
DECODE_SUBBLOCKS = 8


def _mla_decode_kernel(pt_ref, q_ref, cn_ref, kn_ref, ckv_hbm, krt_hbm, o_ref, ckbuf, krbuf, sem,
                       *, layer, nch):
    ppc = PAGES_PER_CHUNK
    b = pl.program_id(0)
    nb = pl.num_programs(0)

    def copies(bb, c, slot):
        out = []
        for i in range(ppc):
            pg = pt_ref[bb, c * ppc + i]
            out.append(pltpu.make_async_copy(ckv_hbm.at[layer, pg], ckbuf.at[slot, i], sem.at[slot]))
            out.append(pltpu.make_async_copy(krt_hbm.at[layer, pg], krbuf.at[slot, i], sem.at[slot]))
        return out

    def start(bb, c, slot):
        for cp in copies(bb, c, slot):
            cp.start()

    @pl.when(b == 0)
    def _():
        start(0, 0, 0)

    q = q_ref[0]
    q_lat = q[:, :KV_LORA]
    q_rope = q[:, KV_LORA:KV_LORA + QK_ROPE]
    psub = ppc // DECODE_SUBBLOCKS
    m_prev = jnp.full((MLA_HEADS, 1), -jnp.inf, F32)
    l_prev = jnp.zeros((MLA_HEADS, 1), F32)
    acc = jnp.zeros((MLA_HEADS, KV_LORA), F32)
    for c in range(nch):
        slot = c % 2
        if c + 1 < nch:
            start(b, c + 1, 1 - slot)
        else:
            @pl.when(b + 1 < nb)
            def _():
                start(b + 1, 0, 1 - slot)
        for cp in copies(b, c, slot):
            cp.wait()
        cks, ss = [], []
        for u in range(DECODE_SUBBLOCKS):
            p0 = u * psub
            ck = ckbuf[slot, p0:p0 + psub].reshape(psub * PAGE_SIZE, KV_LORA).astype(BF16)
            krt = jnp.concatenate([krbuf[slot, p0 + i] for i in range(psub)], axis=1).astype(BF16)
            s = lax.dot_general(q_lat, ck, NT_DIMS, preferred_element_type=F32)
            cks.append(ck)
            ss.append(s + jnp.dot(q_rope, krt, preferred_element_type=F32))
        m_new = m_prev
        for s in ss:
            m_new = jnp.maximum(m_new, jnp.max(s, axis=-1, keepdims=True))
        alpha = jnp.exp2(m_prev - m_new)
        l_prev = alpha * l_prev
        acc = alpha * acc
        for ck, s in zip(cks, ss):
            p = jnp.exp2(s - m_new)
            l_prev = l_prev + jnp.sum(p, axis=-1, keepdims=True)
            acc = acc + jnp.dot(p.astype(BF16), ck, preferred_element_type=F32)
        m_prev = m_new

    cn = cn_ref[0]
    knew = jnp.concatenate([cn.astype(BF16), kn_ref[0]], axis=1).astype(F32)
    s_new = jnp.sum(q.astype(F32) * knew, axis=-1, keepdims=True)
    m_new = jnp.maximum(m_prev, s_new)
    alpha = jnp.exp2(m_prev - m_new)
    p = jnp.exp2(s_new - m_new)
    l_new = alpha * l_prev + p
    acc = alpha * acc + _bf16_round(p) * _bf16_round(cn)
    o_ref[0] = (acc / l_new).astype(BF16)


def _mla_decode(page_table, qcat, ckv_new, krp_new, cache_ckv, cache_krope_t, layer):
    nb, npages = page_table.shape
    nch = npages // PAGES_PER_CHUNK
    assert npages % PAGES_PER_CHUNK == 0 and nch % 2 == 0
    grid_spec = pltpu.PrefetchScalarGridSpec(
        num_scalar_prefetch=1,
        grid=(nb,),
        in_specs=[
            pl.BlockSpec((1, MLA_HEADS, Q_CAT), lambda i, pt: (i, 0, 0)),
            pl.BlockSpec((1, 1, KV_LORA), lambda i, pt: (i, 0, 0)),
            pl.BlockSpec((1, 1, LANES), lambda i, pt: (i, 0, 0)),
            pl.BlockSpec(memory_space=pl.ANY),
            pl.BlockSpec(memory_space=pl.ANY),
        ],
        out_specs=pl.BlockSpec((1, MLA_HEADS, KV_LORA), lambda i, pt: (i, 0, 0)),
        scratch_shapes=[
            pltpu.VMEM((2, PAGES_PER_CHUNK, PAGE_SIZE, KV_LORA), F32),
            pltpu.VMEM((2, PAGES_PER_CHUNK, QK_ROPE, PAGE_SIZE), F32),
            pltpu.SemaphoreType.DMA((2,)),
        ],
    )
    return pl.pallas_call(
        functools.partial(_mla_decode_kernel, layer=layer, nch=nch),
        grid_spec=grid_spec,
        out_shape=jax.ShapeDtypeStruct((nb, MLA_HEADS, KV_LORA), BF16),
        compiler_params=_cparams(("arbitrary",)),
        name="mla_decode",
    )(page_table, qcat, ckv_new.reshape(nb, 1, KV_LORA), krp_new.reshape(nb, 1, LANES),
      cache_ckv, cache_krope_t)


def _rope_angles(pos, half):
    inv = ROPE_THETA ** (-jnp.arange(half, dtype=F32) / half)
    ang = pos.astype(F32)[:, None] * inv[None, :]
    return jnp.cos(ang), jnp.sin(ang)


def _ret_tables(pos):
    cos, sin = _rope_angles(pos, RET_DK // 2)
    return jnp.concatenate([cos, cos], axis=1), jnp.concatenate([-sin, sin], axis=1)


def _mla_tables(pos, rows):
    half = QK_ROPE // 2
    cos, sin = _rope_angles(pos, half)
    n = pos.shape[0]
    c = jnp.concatenate([cos, cos, jnp.ones((n, LANES - QK_ROPE), F32)], axis=1)
    s1 = jnp.concatenate([-sin, jnp.zeros((n, LANES - half), F32)], axis=1)
    s2 = jnp.concatenate([jnp.zeros((n, half), F32), sin, jnp.zeros((n, LANES - QK_ROPE), F32)], axis=1)
    return tuple(jnp.broadcast_to(t, (rows, LANES)) if n == 1 else t for t in (c, s1, s2))


def _pad_lanes(v):
    return jnp.pad(v.astype(F32), (0, LANES - v.shape[0])).reshape(1, LANES)


def _hyb_w_in_layout(w):
    d = w.shape[0]
    qk = 2 * RET_HEADS * RET_DK
    vg = 2 * RET_HEADS * RET_DV
    q_k = w[:, :qk]
    v_g = w[:, qk:qk + vg]
    z = w[:, qk + vg:qk + vg + SSM_D_INNER]
    xbc = w[:, qk + vg + SSM_D_INNER:qk + vg + SSM_D_INNER + CONV_DIM]
    dt = w[:, qk + vg + SSM_D_INNER + CONV_DIM:]
    pieces = [xbc, dt, jnp.zeros((d, OFF_Q - OFF_DT - SSM_HEADS), w.dtype), q_k,
              jnp.zeros((d, OFF_V - OFF_K - RET_HEADS * RET_DK), w.dtype), v_g, z]
    out = jnp.concatenate(pieces, axis=1)
    assert out.shape[1] == HYB_IN_PAD
    return out.astype(BF16)


def _mla_wqt_layout(w_uq):
    k = w_uq.shape[0]
    w3 = w_uq.reshape(k, MLA_HEADS, QK_NOPE + QK_ROPE)
    pad = jnp.zeros((k, MLA_HEADS, LANES - QK_NOPE - QK_ROPE), w_uq.dtype)
    w = jnp.concatenate([w3[..., QK_NOPE:], w3[..., :QK_NOPE], pad], -1)
    return w.reshape(k, MLA_HEADS * LANES).T.astype(BF16)


def _mla_wuk_layout(w_uk):
    k = w_uk.shape[0]
    z1 = jnp.zeros((k, MLA_HEADS, QK_ROPE), w_uk.dtype)
    z2 = jnp.zeros((k, MLA_HEADS, LANES - QK_NOPE - QK_ROPE), w_uk.dtype)
    return jnp.concatenate([z1, w_uk, z2], -1).reshape(k, MLA_HEADS * LANES).astype(BF16)


def _mla_wq_layout(w_uq):
    k = w_uq.shape[0]
    w3 = w_uq.reshape(k, MLA_HEADS, QK_NOPE + QK_ROPE)
    nope, ropew = w3[:, :, :QK_NOPE], w3[:, :, QK_NOPE:]
    z = jnp.zeros_like(nope)
    even = (jnp.arange(MLA_HEADS) % 2 == 0)[None, :, None]
    first = jnp.where(even, jnp.concatenate([nope, z], -1), jnp.concatenate([z, nope], -1))
    second = jnp.concatenate([ropew, jnp.zeros((k, MLA_HEADS, LANES - QK_ROPE), w_uq.dtype)], -1)
    return jnp.concatenate([first, second], -1).reshape(k, MLA_HEADS * Q_HEAD_PAD).astype(BF16)


def kernel(x_prompt, x_sample, state_ret, state_ssm, state_conv, cache_ckv, cache_krope, page_table,
           norm_ffn1, ffn1_w_in, ffn1_w_out, norm_mix, norm_ffn2, ffn2_w_in, ffn2_w_out,
           hyb_w_in, hyb_w_out, hyb_conv_w, hyb_conv_b, hyb_dt_bias, hyb_a_log, hyb_d_skip,
           hyb_norm_w, mla_w_in, mla_q_norm_w, mla_kv_norm_w, mla_w_uq, mla_w_uk, mla_w_uv,
           mla_w_o, final_norm_w):
    bp, sp, d = x_prompt.shape
    bs, ss, _ = x_sample.shape
    assert ss == 1 and sp % CHUNK == 0
    depth = norm_ffn1.shape[0]
    mp = bp * sp
    xp = x_prompt.reshape(mp, d)
    xs = x_sample.reshape(bs, d)
    pos_p = jnp.arange(sp)
    pos_s = PAST_LEN + jnp.arange(1)
    fw = final_norm_w.reshape(1, d)

    ret_cos_p, ret_sin_p = _ret_tables(pos_p)
    ret_cos_s, ret_sin_s = _ret_tables(pos_s)
    mla_tabs_p = _mla_tables(pos_p, sp)
    mla_tabs_s = _mla_tables(pos_s, bs)
    mla_cos_t, mla_sin_t = (t.T for t in _rope_angles(pos_p, QK_ROPE // 2))
    log_gamma = jnp.log1p(-jnp.exp2(-5.0 - jnp.arange(RET_HEADS, dtype=F32)))
    lg_rows = jnp.broadcast_to(log_gamma[:, None], (RET_HEADS, LANES))

    outs = {k: [] for k in ("ret_p", "ret_s", "ssm_p", "ssm_s", "conv_p", "conv_s",
                            "ckv_p", "ckv_s", "kr_p", "kr_s")}
    for layer in range(depth):
        j = layer // 2
        last = layer == depth - 1
        w1i, w1o = _cast_layer(ffn1_w_in, layer), _cast_layer(ffn1_w_out, layer)
        w2i, w2o = _cast_layer(ffn2_w_in, layer), _cast_layer(ffn2_w_out, layer)
        n1 = norm_ffn1[layer].reshape(1, d)
        nm = norm_mix[layer].reshape(1, d)
        n2 = norm_ffn2[layer].reshape(1, d)
        xp = _ffn(xp, n1, w1i, w1o, fw)
        xs = _ffn(xs, n1, w1i, w1o, fw)
        if layer % 2 == 0:
            w_in = _hyb_w_in_layout(hyb_w_in[j])
            w_out = hyb_w_out[j].astype(BF16)
            conv_w = hyb_conv_w[j].astype(F32)
            conv_b = hyb_conv_b[j].reshape(1, CONV_DIM).astype(F32)
            dtb_row = _pad_lanes(hyb_dt_bias[j])
            a_row = _pad_lanes(-jnp.exp(hyb_a_log[j].astype(F32)))
            dsk_row = _pad_lanes(hyb_d_skip[j])
            gnw = hyb_norm_w[j].reshape(1, SSM_D_INNER).astype(F32)
            proj = _norm_matmul(xp, nm, w_in).reshape(bp, sp, HYB_IN_PAD)
            o_ret, r_p = _retention_prompt(proj, ret_cos_p, ret_sin_p, lg_rows.reshape(RET_HEADS, 1, LANES))
            dsk_x = jnp.repeat(hyb_d_skip[j].astype(F32), SSM_HEAD_DIM).reshape(1, SSM_D_INNER)
            o_ssd, s_p, c_p = _ssd_prompt(proj, conv_w, conv_b, dtb_row, a_row, dsk_x, gnw)
            xp = _matmul_res([o_ret.reshape(mp, -1), o_ssd.reshape(mp, -1)], w_out, xp)
            proj_s = _norm_matmul(xs, nm, w_in)
            mixed_s, r_s, s_s, c_s = _hybrid_decode(
                proj_s, state_ret[j], state_ssm[j], state_conv[j], ret_cos_s, ret_sin_s, lg_rows,
                conv_w, conv_b, dtb_row, a_row, dsk_row, gnw)
            xs = _matmul_res([mixed_s.reshape(bs, HYB_MIX)], w_out, xs)
            outs["ret_p"].append(r_p); outs["ret_s"].append(r_s)
            outs["ssm_p"].append(s_p); outs["ssm_s"].append(s_s)
            outs["conv_p"].append(c_p); outs["conv_s"].append(c_s)
        else:
            w_in = jnp.pad(mla_w_in[j], ((0, 0), (0, MLA_IN_PAD - MLA_IN))).astype(BF16)
            qnw = mla_q_norm_w[j].reshape(1, Q_LORA)
            kvnw = mla_kv_norm_w[j].reshape(1, KV_LORA)
            wq = _mla_wq_layout(mla_w_uq[j])
            w_uk2 = mla_w_uk[j].reshape(KV_LORA, MLA_HEADS * QK_NOPE)
            w_uv2 = mla_w_uv[j].reshape(KV_LORA, MLA_HEADS * V_DIM)
            krope_t = jnp.swapaxes(cache_krope, 2, 3)
            w_o = mla_w_o[j].astype(BF16)
            cq, ckv, kr, krp = _mla_in(xp, nm, w_in, qnw, kvnw, mla_tabs_p)
            qt = _mla_qt(cq, _mla_wqt_layout(mla_w_uq[j]), mla_cos_t, mla_sin_t, bp, sp)
            kk, vt = _mla_kv(ckv, krp, _mla_wuk_layout(mla_w_uk[j]), w_uv2.T.astype(BF16), bp, sp)
            o = _flash_prompt(qt, kk.reshape(bp, sp, -1), vt)
            xp = _matmul_res([o.reshape(mp, MLA_HEADS * V_DIM)], w_o, xp)
            outs["ckv_p"].append(ckv.reshape(bp, sp, KV_LORA))
            outs["kr_p"].append(kr.reshape(bp, sp, QK_ROPE))
            cq_s, ckv_s, kr_s, krp_s = _mla_in(xs, nm, w_in, qnw, kvnw, mla_tabs_s)
            qp_s = _mla_q(cq_s, wq, mla_tabs_s)
            qcat = jnp.transpose(_qlat(qp_s, w_uk2.T.astype(BF16)), (1, 0, 2))
            o_lat = _mla_decode(page_table, qcat, ckv_s, krp_s, cache_ckv, krope_t, j)
            o_s = _olat(o_lat.reshape(bs, MLA_HEADS * KV_LORA), w_uv2.astype(BF16))
            xs = _matmul_res([o_s], w_o, xs)
            outs["ckv_s"].append(ckv_s.reshape(bs, 1, KV_LORA))
            outs["kr_s"].append(kr_s.reshape(bs, 1, QK_ROPE))
        xp = _ffn(xp, n2, w2i, w2o, fw, final_norm=last)
        xs = _ffn(xs, n2, w2i, w2o, fw, final_norm=last)
    if depth == 0:
        raise ValueError("depth must be positive")
    return (xp.reshape(bp, sp, d), xs.reshape(bs, 1, d),
            jnp.stack(outs["ret_p"]), jnp.stack(outs["ret_s"]),
            jnp.stack(outs["ssm_p"]), jnp.stack(outs["ssm_s"]),
            jnp.stack(outs["conv_p"]), jnp.stack(outs["conv_s"]),
            jnp.stack(outs["ckv_p"]), jnp.stack(outs["ckv_s"]),
            jnp.stack(outs["kr_p"]), jnp.stack(outs["kr_s"]))
```
